```python
import jax, jax.numpy as jnp
from jax import lax
import numpy as np

D_MODEL = 2048
BATCH = 4
SEQ = 2048
DEPTH = 1
DEC_BATCH = 128
DEC_SEQ = 4
PAST_LEN = 16384
PAGE_SIZE = 128

MIX_WIDTH = D_MODEL
A_WIDTH = MIX_WIDTH // 2
A_GROUPS = 8
A_GROUP_DIM = A_WIDTH // A_GROUPS
A_CHUNK = 128
B_HEADS = 8
B_KEY_DIM = (MIX_WIDTH - A_WIDTH) // B_HEADS
B_VAL_DIM = B_KEY_DIM
B_WIDTH = B_HEADS * B_VAL_DIM
CONV_W = 4
CONV_DIM = 2 * B_HEADS * B_KEY_DIM + B_WIDTH
DN_CHUNK = 64
IN_DIM = 2 * A_WIDTH + CONV_DIM + B_WIDTH + 2 * B_HEADS
N_GROUPS = 4
EXPERTS_PER_GROUP = 8
N_EXPERTS = N_GROUPS * EXPERTS_PER_GROUP
TOP_K = 2
D_EXPERT = D_MODEL // 4
MOE_BLOCK = 128
PLE_DIM = 256
EPS = 1e-6

kernel_name = "hymba_chunkmlp_gdn_hmoe_ple_step"


def rms_norm(x, g):
    xf = x.astype(jnp.float32)
    y = xf * lax.rsqrt(jnp.mean(xf * xf, axis=-1, keepdims=True) + EPS)
    return (y * g.astype(jnp.float32)).astype(x.dtype)


def group_layer_norm(v, g):
    vf = v.astype(jnp.float32)
    mu = jnp.mean(vf, axis=-1, keepdims=True)
    d = vf - mu
    var = jnp.mean(d * d, axis=-1, keepdims=True)
    return (d * lax.rsqrt(var + EPS) * g.astype(jnp.float32)).astype(v.dtype)


def l2_norm(x):
    xf = x.astype(jnp.float32)
    return xf * lax.rsqrt(jnp.sum(xf * xf, axis=-1, keepdims=True) + EPS)


def chunk_spatial_gate(u, v, w_s, b_s):
    Bn, L, H, Cd = v.shape
    causal = jnp.tril(jnp.ones((A_CHUNK, A_CHUNK), dtype=bool))
    w = jnp.where(causal, w_s, 0).astype(v.dtype)
    b = b_s.astype(v.dtype)
    if L < A_CHUNK:
        mixed = jnp.einsum('hts,bshc->bthc', w[:, :L, :L], v) + b[:, :L].T[:, :, None]
    else:
        n = -(-L // A_CHUNK)
        vp = jnp.pad(v, ((0, 0), (0, n * A_CHUNK - L), (0, 0), (0, 0))).reshape(Bn, n, A_CHUNK, H, Cd)
        mixed = (jnp.einsum('hts,bnshc->bnthc', w, vp) + b.T[:, :, None]).reshape(Bn, n * A_CHUNK, H, Cd)[:, :L]
    return u * mixed


def short_conv(xs, buf, w):
    L = xs.shape[1]
    xp = jnp.concatenate([buf.astype(xs.dtype), xs], axis=1)
    y = xp[:, 0:L] * w[0].astype(xs.dtype)
    for j in range(1, CONV_W):
        y = y + xp[:, j:j + L] * w[j].astype(xs.dtype)
    return jax.nn.silu(y), xp[:, L:]


def gated_delta_rule(q, k, v, g, beta, s0):
    f32 = jnp.float32
    Bn, L, H, DK = q.shape
    DV = v.shape[-1]
    C = DN_CHUNK if L >= DN_CHUNK else L
    n = -(-L // C)
    pad = n * C - L

    def prep(t):
        t = jnp.pad(t.astype(f32), [(0, 0), (0, pad)] + [(0, 0)] * (t.ndim - 2))
        t = t.reshape((Bn, n, C) + t.shape[2:])
        return jnp.moveaxis(t, 3, 1)

    q = prep(q) * (DK ** -0.5)
    k, v, g, beta = prep(k), prep(v), prep(g), prep(beta)
    G = jnp.cumsum(g, axis=-1)
    causal = jnp.tril(jnp.ones((C, C), dtype=bool))
    strict = jnp.tril(jnp.ones((C, C), dtype=bool), -1)
    decay = jnp.exp(jnp.where(causal, G[..., :, None] - G[..., None, :], -jnp.inf))
    kb = k * beta[..., None]
    a_mat = jnp.where(strict, jnp.einsum('bhnid,bhnjd->bhnij', kb, k) * decay, 0.0)
    m = a_mat + jnp.eye(C, dtype=f32)
    u = lax.linalg.triangular_solve(m, v * beta[..., None], left_side=True, lower=True, unit_diagonal=True)
    w = lax.linalg.triangular_solve(m, kb * jnp.exp(G)[..., None], left_side=True, lower=True, unit_diagonal=True)
    qk = jnp.einsum('bhnid,bhnjd->bhnij', q, k) * decay
    qg = q * jnp.exp(G)[..., None]
    kd = k * jnp.exp(G[..., -1:] - G)[..., None]
    gl = jnp.exp(G[..., -1])

    def step(S, inp):
        qg_c, kd_c, u_c, w_c, qk_c, gl_c = inp
        v_new = u_c - jnp.einsum('bhck,bhkv->bhcv', w_c, S)
        o = jnp.einsum('bhck,bhkv->bhcv', qg_c, S) + jnp.einsum('bhij,bhjv->bhiv', qk_c, v_new)
        S = S * gl_c[..., None, None] + jnp.einsum('bhck,bhcv->bhkv', kd_c, v_new)
        return S, o

    xs = tuple(jnp.moveaxis(t, 2, 0) for t in (qg, kd, u, w, qk, gl))
    S, o = lax.scan(step, s0.astype(f32), xs)
    o = jnp.transpose(o, (1, 0, 3, 2, 4)).reshape(Bn, n * C, H, DV)[:, :L]
    return o, S


def swiglu(xb, wg, wu, wd):
    return (jax.nn.silu(xb @ wg.astype(xb.dtype)) * (xb @ wu.astype(xb.dtype))) @ wd.astype(xb.dtype)


def hier_moe(h, w_group, b_group, w_router, b_router, w_gate, w_up, w_down):
    T, D = h.shape
    hf = h.astype(jnp.float32)
    g_logits = hf @ w_group.astype(jnp.float32) + b_group.astype(jnp.float32)
    g_prob = jax.nn.softmax(g_logits, axis=-1)
    g_sel = jnp.argmax(g_logits, axis=-1)
    g_w = jnp.take_along_axis(g_prob, g_sel[:, None], axis=1)
    e_logits = (hf @ w_router.astype(jnp.float32) + b_router.astype(jnp.float32)).reshape(T, N_GROUPS, EXPERTS_PER_GROUP)
    e_in_group = jnp.take_along_axis(e_logits, g_sel[:, None, None], axis=1)[:, 0]
    top_v, top_i = lax.top_k(e_in_group, TOP_K)
    weight = g_w * jax.nn.softmax(top_v, axis=-1)
    expert_id = g_sel[:, None] * EXPERTS_PER_GROUP + top_i

    n_assign = T * TOP_K
    flat_e = expert_id.reshape(-1).astype(jnp.int32)
    flat_w = weight.reshape(-1)
    flat_tok = jnp.repeat(jnp.arange(T, dtype=jnp.int32), TOP_K)
    order = jnp.argsort(flat_e)
    e_sorted = flat_e[order]
    counts = jnp.zeros((N_EXPERTS,), jnp.int32).at[flat_e].add(1)
    padded = (counts + MOE_BLOCK - 1) // MOE_BLOCK * MOE_BLOCK
    pad_end = jnp.cumsum(padded)
    pad_start = pad_end - padded
    start = jnp.cumsum(counts) - counts
    rank = jnp.arange(n_assign, dtype=jnp.int32) - start[e_sorted]
    dest = pad_start[e_sorted] + rank
    n_blocks = -(-n_assign // MOE_BLOCK) + N_EXPERTS
    n_rows = n_blocks * MOE_BLOCK
    row_tok = jnp.zeros((n_rows,), jnp.int32).at[dest].set(flat_tok[order])
    row_w = jnp.zeros((n_rows,), h.dtype).at[dest].set(flat_w[order].astype(h.dtype))
    block_e = jnp.minimum(jnp.searchsorted(pad_end, jnp.arange(n_blocks, dtype=jnp.int32) * MOE_BLOCK, side='right'),
                          N_EXPERTS - 1).astype(jnp.int32)
    xb = h[row_tok].reshape(n_blocks, MOE_BLOCK, D)

    def block_ffn(args):
        x_blk, e = args
        return swiglu(x_blk, w_gate[e], w_up[e], w_down[e])

    yb = lax.map(block_ffn, (xb, block_e)).reshape(n_rows, D)
    return jax.ops.segment_sum(yb * row_w[:, None], row_tok, num_segments=T)


def hybrid_layer(x, p, conv_buf, s0, g_mix, w_in, w_s, b_s, g_v, conv_w, a_log, dt_bias, g_out, w_out,
                 g_ffn, w_group, b_group, w_router, b_router, w_gate, w_up, w_down, g_ple, w_ple_gate, w_ple):
    Bn, L, D = x.shape
    h = rms_norm(x, g_mix)
    proj = h @ w_in.astype(x.dtype)
    s1 = A_WIDTH
    s2 = 2 * A_WIDTH
    s3 = s2 + CONV_DIM
    s4 = s3 + B_WIDTH
    s5 = s4 + B_HEADS
    a_u, a_v, qkv, z, a_dec, b_beta = jnp.split(proj, [s1, s2, s3, s4, s5], axis=-1)

    u = jax.nn.gelu(a_u).reshape(Bn, L, A_GROUPS, A_GROUP_DIM)
    v_a = group_layer_norm(jax.nn.gelu(a_v).reshape(Bn, L, A_GROUPS, A_GROUP_DIM), g_v)
    out_a = chunk_spatial_gate(u, v_a, w_s, b_s).reshape(Bn, L, A_WIDTH)

    qkv, new_buf = short_conv(qkv, conv_buf, conv_w)
    q, k, v = jnp.split(qkv, [B_HEADS * B_KEY_DIM, 2 * B_HEADS * B_KEY_DIM], axis=-1)
    q = l2_norm(q.reshape(Bn, L, B_HEADS, B_KEY_DIM))
    k = l2_norm(k.reshape(Bn, L, B_HEADS, B_KEY_DIM))
    v = v.reshape(Bn, L, B_HEADS, B_VAL_DIM)
    beta = jax.nn.sigmoid(b_beta.astype(jnp.float32))
    g = -jnp.exp(a_log.astype(jnp.float32)) * jax.nn.softplus(a_dec.astype(jnp.float32) + dt_bias.astype(jnp.float32))
    o, s_new = gated_delta_rule(q, k, v, g, beta, s0)
    o = rms_norm(o.astype(x.dtype), g_out) * jax.nn.silu(z.reshape(Bn, L, B_HEADS, B_VAL_DIM))
    out_b = o.reshape(Bn, L, B_WIDTH)

    x = x + jnp.concatenate([out_a, out_b], axis=-1) @ w_out.astype(x.dtype)

    h2 = rms_norm(x, g_ffn).reshape(Bn * L, D)
    x = x + hier_moe(h2, w_group, b_group, w_router, b_router, w_gate, w_up, w_down).reshape(Bn, L, D)

    gate = jax.nn.sigmoid(rms_norm(x, g_ple) @ w_ple_gate.astype(x.dtype))
    x = x + (p.astype(x.dtype) @ w_ple.astype(x.dtype)) * gate
    return x, new_buf, s_new.astype(x.dtype), v_a.reshape(Bn, L, A_WIDTH)


def setup_inputs(seed: int = 0) -> dict:
    key = jax.random.key(seed)
    ks = jax.random.split(key, 32)
    f32 = jnp.float32
    nrm = lambda k, shape, s: jax.random.normal(k, shape, f32) * s
    causal_mask = jnp.tril(jnp.ones((A_CHUNK, A_CHUNK), f32))
    return {
        "x_prompt": nrm(ks[0], (BATCH, SEQ, D_MODEL), 1.0),
        "x_sample": nrm(ks[1], (DEC_BATCH, DEC_SEQ, D_MODEL), 1.0),
        "state_conv": nrm(ks[2], (DEPTH, DEC_BATCH, CONV_W - 1, CONV_DIM), 1.0),
        "state_delta": nrm(ks[3], (DEPTH, DEC_BATCH, B_HEADS, B_KEY_DIM, B_VAL_DIM), 0.5),
        "p_prompt": nrm(ks[4], (DEPTH, BATCH, SEQ, PLE_DIM), 1.0),
        "p_sample": nrm(ks[5], (DEPTH, DEC_BATCH, DEC_SEQ, PLE_DIM), 1.0),
        "g_mix": 1.0 + nrm(ks[6], (DEPTH, D_MODEL), 0.02),
        "w_in": nrm(ks[7], (DEPTH, D_MODEL, IN_DIM), D_MODEL ** -0.5),
        "w_s": nrm(ks[8], (DEPTH, A_GROUPS, A_CHUNK, A_CHUNK), A_CHUNK ** -0.5) * causal_mask,
        "b_s": 1.0 + nrm(ks[9], (DEPTH, A_GROUPS, A_CHUNK), 0.02),
        "g_v": 1.0 + nrm(ks[10], (DEPTH, A_GROUPS, A_GROUP_DIM), 0.02),
        "conv_w": nrm(ks[11], (DEPTH, CONV_W, CONV_DIM), 0.5),
        "a_log": jnp.log(jax.random.uniform(ks[12], (DEPTH, B_HEADS), f32, 1.0, 16.0)),
        "dt_bias": nrm(ks[13], (DEPTH, B_HEADS), 0.1),
        "g_out": 1.0 + nrm(ks[14], (DEPTH, B_VAL_DIM), 0.02),
        "w_out": nrm(ks[15], (DEPTH, MIX_WIDTH, D_MODEL), MIX_WIDTH ** -0.5),
        "g_ffn": 1.0 + nrm(ks[16], (DEPTH, D_MODEL), 0.02),
        "w_group": nrm(ks[17], (DEPTH, D_MODEL, N_GROUPS), D_MODEL ** -0.5),
        "b_group": nrm(ks[18], (DEPTH, N_GROUPS), 0.01),
        "w_router": nrm(ks[19], (DEPTH, D_MODEL, N_EXPERTS), D_MODEL ** -0.5),
        "b_router": nrm(ks[20], (DEPTH, N_EXPERTS), 0.01),
        "w_gate": nrm(ks[21], (DEPTH, N_EXPERTS, D_MODEL, D_EXPERT), D_MODEL ** -0.5),
        "w_up": nrm(ks[22], (DEPTH, N_EXPERTS, D_MODEL, D_EXPERT), D_MODEL ** -0.5),
        "w_down": nrm(ks[23], (DEPTH, N_EXPERTS, D_EXPERT, D_MODEL), D_EXPERT ** -0.5),
        "g_ple": 1.0 + nrm(ks[24], (DEPTH, D_MODEL), 0.02),
        "w_ple_gate": nrm(ks[25], (DEPTH, D_MODEL, D_MODEL), D_MODEL ** -0.5),
        "w_ple": nrm(ks[26], (DEPTH, PLE_DIM, D_MODEL), PLE_DIM ** -0.5),
        "g_final": 1.0 + nrm(ks[27], (D_MODEL,), 0.02),
    }


def reference(x_prompt, x_sample, state_conv, state_delta, p_prompt, p_sample, g_mix, w_in, w_s, b_s, g_v,
              conv_w, a_log, dt_bias, g_out, w_out, g_ffn, w_group, b_group, w_router, b_router, w_gate, w_up,
              w_down, g_ple, w_ple_gate, w_ple, g_final):
    bp = x_prompt.shape[0]
    zero_conv = jnp.zeros((bp, CONV_W - 1, CONV_DIM), x_prompt.dtype)
    zero_delta = jnp.zeros((bp, B_HEADS, B_KEY_DIM, B_VAL_DIM), x_prompt.dtype)
    xp, xs = x_prompt, x_sample
    conv_p, delta_p, conv_s, delta_s, chunk_v_s = [], [], [], [], []
    for i in range(DEPTH):
        lw = (g_mix[i], w_in[i], w_s[i], b_s[i], g_v[i], conv_w[i], a_log[i], dt_bias[i], g_out[i], w_out[i],
              g_ffn[i], w_group[i], b_group[i], w_router[i], b_router[i], w_gate[i], w_up[i], w_down[i],
              g_ple[i], w_ple_gate[i], w_ple[i])
        xp, cb_p, sd_p, _ = hybrid_layer(xp, p_prompt[i], zero_conv, zero_delta, *lw)
        xs, cb_s, sd_s, va_s = hybrid_layer(xs, p_sample[i], state_conv[i], state_delta[i], *lw)
        conv_p.append(cb_p)
        delta_p.append(sd_p)
        conv_s.append(cb_s)
        delta_s.append(sd_s)
        chunk_v_s.append(va_s)
    y_prompt = rms_norm(xp, g_final)
    y_sample = rms_norm(xs, g_final)
    return (y_prompt, y_sample, jnp.stack(conv_p, 0), jnp.stack(delta_p, 0), jnp.stack(conv_s, 0),
            jnp.stack(delta_s, 0), jnp.stack(chunk_v_s, 0))
```

```python
from functools import partial

import jax
import jax.numpy as jnp
from jax import lax
from jax.experimental import pallas as pl
from jax.experimental.pallas import tpu as pltpu

F32 = jnp.float32
BF16 = jnp.bfloat16
I32 = jnp.int32

EPS = 1e-6
LANES = 128
A_GROUPS = 8
A_GROUP_DIM = 128
A_CHUNK = 128
B_HEADS = 8
B_DIM = 128
CONV_W = 4
N_GROUPS = 4
EXPERTS_PER_GROUP = 8
N_EXPERTS = N_GROUPS * EXPERTS_PER_GROUP
TOKEN_TILE = 512
GDN_CHUNK = 64
MOE_BLOCK = 256
DEC_SEQ_TILE = 8
VMEM_LIMIT = 56 * 1024 * 1024


def _cparams(sem):
    return pltpu.CompilerParams(dimension_semantics=sem, vmem_limit_bytes=VMEM_LIMIT)


def _rms(x, g):
    return x * lax.rsqrt(jnp.mean(x * x, axis=-1, keepdims=True) + EPS) * g


def _softplus(x):
    return jnp.maximum(x, 0.0) + jnp.log(1.0 + jnp.exp(-jnp.abs(x)))


def _dot(a, b):
    return jnp.dot(a, b, preferred_element_type=F32)


def _dot_nt(a, b, precision=None):
    return lax.dot_general(a, b, (((1,), (1,)), ((), ())), precision=precision,
                           preferred_element_type=F32)


def _dot_tn(a, b):
    return lax.dot_general(a, b, (((0,), (0,)), ((), ())), preferred_element_type=F32)


def _inproj_kernel(x_ref, g_ref, w_ref, wt_ref, proj_ref, tail_ref, h_scr, *, time_major):
    j = pl.program_id(1)

    @pl.when(j == 0)
    def _():
        g = g_ref[...]
        if time_major:
            d = g.shape[-1]
            rows = x_ref.shape[0]
            for t in range(x_ref.shape[1] // d):
                h_scr[t * rows:(t + 1) * rows, :] = _rms(x_ref[:, t * d:(t + 1) * d], g).astype(BF16)
        else:
            h_scr[...] = _rms(x_ref[...], g).astype(BF16)
        tail_ref[...] = _dot(h_scr[...], wt_ref[...])

    proj_ref[...] = _dot(h_scr[...], w_ref[...])


def _inproj(x2d, g_mix, w_main, w_tail, n_main, *, time_major, col_tile=1024):
    d = g_mix.shape[-1]
    if time_major:
        n_tok = x2d.shape[0] * (x2d.shape[1] // d)
        tm = n_tok
        x_spec = pl.BlockSpec(x2d.shape, lambda i, j: (0, 0))
    else:
        n_tok = x2d.shape[0]
        tm = TOKEN_TILE
        x_spec = pl.BlockSpec((tm, d), lambda i, j: (i, 0))
    grid = (n_tok // tm, n_main // col_tile)
    return pl.pallas_call(
        partial(_inproj_kernel, time_major=time_major),
        grid=grid,
        in_specs=[
            x_spec,
            pl.BlockSpec((1, d), lambda i, j: (0, 0)),
            pl.BlockSpec((d, col_tile), lambda i, j: (0, j)),
            pl.BlockSpec((d, LANES), lambda i, j: (0, 0)),
        ],
        out_specs=[
            pl.BlockSpec((tm, col_tile), lambda i, j: (i, j)),
            pl.BlockSpec((tm, LANES), lambda i, j: (i, 0)),
        ],
        out_shape=[
            jax.ShapeDtypeStruct((n_tok, n_main), F32),
            jax.ShapeDtypeStruct((n_tok, LANES), F32),
        ],
        scratch_shapes=[pltpu.VMEM((tm, d), BF16)],
        compiler_params=_cparams(("parallel", "arbitrary")),
        name="inproj_tm" if time_major else "inproj",
    )(x2d, g_mix, w_main, w_tail)


def _group_ln(v, g):
    mu = jnp.mean(v, axis=-1, keepdims=True)
    dlt = v - mu
    var = jnp.mean(dlt * dlt, axis=-1, keepdims=True)
    return dlt * lax.rsqrt(var + EPS) * g


def _mixa_prompt_kernel(au_ref, av_ref, ws_ref, bt_ref, gv_ref, out_ref):
    n_chunks = au_ref.shape[0] // A_CHUNK
    for c in range(n_chunks):
        rows = slice(c * A_CHUNK, (c + 1) * A_CHUNK)
        for h in range(A_GROUPS):
            cols = slice(h * A_GROUP_DIM, (h + 1) * A_GROUP_DIM)
            u = jax.nn.gelu(au_ref[rows, cols])
            v = _group_ln(jax.nn.gelu(av_ref[rows, cols]), gv_ref[:, cols])
            mixed = _dot(ws_ref[h], v.astype(BF16)) + bt_ref[:, h:h + 1]
            out_ref[rows, cols] = (u * mixed).astype(BF16)


def _mixa_prompt(proj, ws_tril, b_t, g_v, n_tok):
    a_width = A_GROUPS * A_GROUP_DIM
    tm = TOKEN_TILE
    return pl.pallas_call(
        _mixa_prompt_kernel,
        grid=(n_tok // tm,),
        in_specs=[
            pl.BlockSpec((tm, a_width), lambda i: (i, 0)),
            pl.BlockSpec((tm, a_width), lambda i: (i, 1)),
            pl.BlockSpec((A_GROUPS, A_CHUNK, A_CHUNK), lambda i: (0, 0, 0)),
            pl.BlockSpec((A_CHUNK, A_GROUPS), lambda i: (0, 0)),
            pl.BlockSpec((1, a_width), lambda i: (0, 0)),
        ],
        out_specs=pl.BlockSpec((tm, a_width), lambda i: (i, 0)),
        out_shape=jax.ShapeDtypeStruct((n_tok, a_width), BF16),
        compiler_params=_cparams(("parallel",)),
        name="mixa_prompt",
    )(proj, proj, ws_tril, b_t, g_v)


def _mixa_sample_kernel(ws_ref, bs_ref, au_ref, av_ref, gv_ref, out_ref, va_ref, *, n_seq, n_pos):
    a_width = A_GROUPS * A_GROUP_DIM
    vs = []
    for t in range(n_pos):
        rows = slice(t * n_seq, (t + 1) * n_seq)
        per_group = []
        for h in range(A_GROUPS):
            cols = slice(h * A_GROUP_DIM, (h + 1) * A_GROUP_DIM)
            per_group.append(_group_ln(jax.nn.gelu(av_ref[rows, cols]), gv_ref[:, cols]))
        vs.append(per_group)
        for h in range(A_GROUPS):
            va_ref[:, t * a_width + h * A_GROUP_DIM:t * a_width + (h + 1) * A_GROUP_DIM] = per_group[h]
    for t in range(n_pos):
        rows = slice(t * n_seq, (t + 1) * n_seq)
        for h in range(A_GROUPS):
            cols = slice(h * A_GROUP_DIM, (h + 1) * A_GROUP_DIM)
            mixed = jnp.zeros((n_seq, A_GROUP_DIM), F32) + bs_ref[h * n_pos + t]
            for s in range(t + 1):
                mixed = mixed + ws_ref[(h * n_pos + t) * n_pos + s] * vs[s][h]
            out_ref[rows, cols] = (jax.nn.gelu(au_ref[rows, cols]) * mixed).astype(BF16)


def _mixa_sample(proj, w_s, b_s, g_v, n_seq, n_pos):
    a_width = A_GROUPS * A_GROUP_DIM
    n_tok = n_seq * n_pos
    return pl.pallas_call(
        partial(_mixa_sample_kernel, n_seq=n_seq, n_pos=n_pos),
        grid_spec=pltpu.PrefetchScalarGridSpec(
            num_scalar_prefetch=2,
            grid=(1,),
            in_specs=[
                pl.BlockSpec((n_tok, a_width), lambda i, *_: (0, 0)),
                pl.BlockSpec((n_tok, a_width), lambda i, *_: (0, 1)),
                pl.BlockSpec((1, a_width), lambda i, *_: (0, 0)),
            ],
            out_specs=[
                pl.BlockSpec((n_tok, a_width), lambda i, *_: (0, 0)),
                pl.BlockSpec((n_seq, n_pos * a_width), lambda i, *_: (0, 0)),
            ],
        ),
        out_shape=[
            jax.ShapeDtypeStruct((n_tok, a_width), BF16),
            jax.ShapeDtypeStruct((n_seq, n_pos * a_width), F32),
        ],
        compiler_params=_cparams(("arbitrary",)),
        name="mixa_sample",
    )(w_s[:, :n_pos, :n_pos].reshape(-1), b_s[:, :n_pos].reshape(-1), proj, proj, g_v)


def _gdn_prompt_kernel(q_ref, k_ref, v_ref, z_ref, tail_ref, cw_ref, alog_ref, dtb_ref, gout_ref,
                       out_ref, s_out_ref, conv_out_ref, s_scr, xbuf):
    c = pl.program_id(1)
    n_c = pl.num_programs(1)
    C = q_ref.shape[0]
    hd = B_HEADS * B_DIM
    pad = 8
    lo = pad - (CONV_W - 1)

    @pl.when(c == 0)
    def _():
        s_scr[...] = jnp.zeros_like(s_scr)
        xbuf[0:pad, :] = jnp.zeros((pad, 3 * hd), F32)

    xbuf[pad:pad + C, 0:hd] = q_ref[...]
    xbuf[pad:pad + C, hd:2 * hd] = k_ref[...]
    xbuf[pad:pad + C, 2 * hd:3 * hd] = v_ref[...]
    y = xbuf[lo:lo + C, :] * cw_ref[0:1, :]
    for j in range(1, CONV_W):
        y = y + xbuf[lo + j:lo + j + C, :] * cw_ref[j:j + 1, :]
    y = y * jax.nn.sigmoid(y)
    last = xbuf[pad + C - (CONV_W - 1):pad + C, :]
    xbuf[lo:pad, :] = last

    @pl.when(c == n_c - 1)
    def _():
        conv_out_ref[0] = last

    tail = tail_ref[...]
    g_all = -jnp.exp(alog_ref[...]) * _softplus(tail + dtb_ref[...])
    beta_all = jax.nn.sigmoid(tail)
    ri = lax.broadcasted_iota(I32, (C, C), 0)
    ci = lax.broadcasted_iota(I32, (C, C), 1)
    causal = ci <= ri
    strict = ci < ri
    tril = causal.astype(F32)
    G_all = jnp.dot(tril, g_all, precision=lax.Precision.HIGHEST, preferred_element_type=F32)
    G_rows = jnp.concatenate([G_all, jnp.zeros((LANES - C, LANES), F32)], axis=0).T if C < LANES else G_all.T
    scale = B_DIM ** -0.5
    n_levels = max(C.bit_length() - 2, 0)

    for h in range(B_HEADS):
        cols = slice(h * B_DIM, (h + 1) * B_DIM)
        q = y[:, h * B_DIM:(h + 1) * B_DIM]
        k = y[:, hd + h * B_DIM:hd + (h + 1) * B_DIM]
        v = y[:, 2 * hd + h * B_DIM:2 * hd + (h + 1) * B_DIM]
        q = q * lax.rsqrt(jnp.sum(q * q, axis=-1, keepdims=True) + EPS) * scale
        k = k * lax.rsqrt(jnp.sum(k * k, axis=-1, keepdims=True) + EPS)
        G = G_all[:, h:h + 1]
        beta = beta_all[:, B_HEADS + h:B_HEADS + h + 1]
        eG = jnp.exp(G)
        decay = jnp.exp(jnp.where(causal, G - G_rows[h:h + 1, 0:C], -jnp.inf))
        kk = _dot_nt(k, k)
        qk = _dot_nt(q, k) * decay
        N = jnp.where(strict, -(beta * kk * decay), 0.0)
        L = N
        P = N
        for _ in range(n_levels):
            P = _dot(P, P)
            L = L + P + _dot(L, P)
        rhs = jnp.concatenate([v * beta, k * (beta * eG)], axis=1)
        uw = rhs + _dot(L, rhs)
        u = uw[:, :B_DIM]
        w = uw[:, B_DIM:]
        S = s_scr[h]
        r = _dot(jnp.concatenate([w, q * eG], axis=0), S)
        v_new = u - r[:C]
        o = r[C:] + _dot(qk, v_new)
        G_last = G[C - 1:C, :]
        kd = k * jnp.exp(G_last - G)
        s_scr[h] = S * jnp.exp(G_last) + _dot_tn(kd, v_new)
        o = _rms(o, gout_ref[...])
        z = z_ref[:, cols]
        out_ref[:, cols] = (o * (z * jax.nn.sigmoid(z))).astype(BF16)

    @pl.when(c == n_c - 1)
    def _():
        s_out_ref[0] = s_scr[...]


def _gdn_prompt(proj, tail, conv_w, alog_row, dtb_row, g_out, n_batch, seq):
    C = GDN_CHUNK
    n_c = seq // C
    hd = B_HEADS * B_DIM
    row = lambda b, c: b * n_c + c
    return pl.pallas_call(
        _gdn_prompt_kernel,
        grid=(n_batch, n_c),
        in_specs=[
            pl.BlockSpec((C, hd), lambda b, c: (row(b, c), 2)),
            pl.BlockSpec((C, hd), lambda b, c: (row(b, c), 3)),
            pl.BlockSpec((C, hd), lambda b, c: (row(b, c), 4)),
            pl.BlockSpec((C, hd), lambda b, c: (row(b, c), 5)),
            pl.BlockSpec((C, LANES), lambda b, c: (row(b, c), 0)),
            pl.BlockSpec((CONV_W, 3 * hd), lambda b, c: (0, 0)),
            pl.BlockSpec((1, LANES), lambda b, c: (0, 0)),
            pl.BlockSpec((1, LANES), lambda b, c: (0, 0)),
            pl.BlockSpec((1, B_DIM), lambda b, c: (0, 0)),
        ],
        out_specs=[
            pl.BlockSpec((C, hd), lambda b, c: (row(b, c), 0)),
            pl.BlockSpec((1, B_HEADS, B_DIM, B_DIM), lambda b, c: (b, 0, 0, 0)),
            pl.BlockSpec((1, CONV_W - 1, 3 * hd), lambda b, c: (b, 0, 0)),
        ],
        out_shape=[
            jax.ShapeDtypeStruct((n_batch * seq, hd), BF16),
            jax.ShapeDtypeStruct((n_batch, B_HEADS, B_DIM, B_DIM), F32),
            jax.ShapeDtypeStruct((n_batch, CONV_W - 1, 3 * hd), F32),
        ],
        scratch_shapes=[
            pltpu.VMEM((B_HEADS, B_DIM, B_DIM), F32),
            pltpu.VMEM((8 + C, 3 * hd), F32),
        ],
        compiler_params=_cparams(("parallel", "arbitrary")),
        name="gdn_prompt",
    )(proj, proj, proj, proj, tail, conv_w, alog_row, dtb_row, g_out)


def _gdn_decode_kernel(*refs, n_pos):
    proj_refs = refs[:n_pos]
    tail_refs = refs[n_pos:2 * n_pos]
    (cbuf_ref, s_ref, cw_ref, alog_ref, dtb_ref, gout_ref,
     out_ref, s_out_ref, conv_out_ref, lhs_scr, res_scr, kd_scr, vn_scr) = refs[2 * n_pos:]
    SB = s_ref.shape[0]
    hd = B_HEADS * B_DIM
    qkv0 = 2 * hd
    z0 = qkv0 + 3 * hd
    scale = B_DIM ** -0.5

    xp = [cbuf_ref[:, j * 3 * hd:(j + 1) * 3 * hd] for j in range(CONV_W - 1)]
    xp += [proj_refs[t][:, qkv0:qkv0 + 3 * hd] for t in range(n_pos)]
    for j in range(CONV_W - 1):
        conv_out_ref[:, j * 3 * hd:(j + 1) * 3 * hd] = xp[n_pos + j]
    ys = []
    for t in range(n_pos):
        y = xp[t] * cw_ref[0:1, :]
        for j in range(1, CONV_W):
            y = y + xp[t + j] * cw_ref[j:j + 1, :]
        ys.append(y * jax.nn.sigmoid(y))
    gs, betas = [], []
    for t in range(n_pos):
        tail = tail_refs[t][...]
        gs.append(-jnp.exp(alog_ref[...]) * _softplus(tail + dtb_ref[...]))
        betas.append(jax.nn.sigmoid(tail))

    kd_scr[...] = jnp.zeros_like(kd_scr)
    vn_scr[...] = jnp.zeros_like(vn_scr)

    for h in range(B_HEADS):
        cols = slice(h * B_DIM, (h + 1) * B_DIM)
        qs, ks, vs, Gs, bs = [], [], [], [], []
        G = None
        for t in range(n_pos):
            q = ys[t][:, h * B_DIM:(h + 1) * B_DIM]
            k = ys[t][:, hd + h * B_DIM:hd + (h + 1) * B_DIM]
            qs.append(q * lax.rsqrt(jnp.sum(q * q, axis=-1, keepdims=True) + EPS) * scale)
            ks.append(k * lax.rsqrt(jnp.sum(k * k, axis=-1, keepdims=True) + EPS))
            vs.append(ys[t][:, 2 * hd + h * B_DIM:2 * hd + (h + 1) * B_DIM])
            g = gs[t][:, h:h + 1]
            G = g if G is None else G + g
            Gs.append(G)
            bs.append(betas[t][:, B_HEADS + h:B_HEADS + h + 1])
        us, ws = [], []
        for t in range(n_pos):
            u = vs[t] * bs[t]
            w = ks[t] * (bs[t] * jnp.exp(Gs[t]))
            for s in range(t):
                a = bs[t] * jnp.sum(ks[t] * ks[s], axis=-1, keepdims=True) * jnp.exp(Gs[t] - Gs[s])
                u = u - a * us[s]
                w = w - a * ws[s]
            us.append(u)
            ws.append(w)
        for t in range(n_pos):
            lhs_scr[pl.ds(t, SB, stride=2 * n_pos), :] = ws[t]
            lhs_scr[pl.ds(n_pos + t, SB, stride=2 * n_pos), :] = qs[t] * jnp.exp(Gs[t])
        for s in range(SB):
            rows = slice(s * 2 * n_pos, (s + 1) * 2 * n_pos)
            res_scr[rows, :] = _dot(lhs_scr[rows, :], s_ref[s, h])
        v_news = []
        for t in range(n_pos):
            v_news.append(us[t] - res_scr[pl.ds(t, SB, stride=2 * n_pos), :])
        G_last = Gs[n_pos - 1]
        for t in range(n_pos):
            o = res_scr[pl.ds(n_pos + t, SB, stride=2 * n_pos), :]
            for s in range(t + 1):
                qk = jnp.sum(qs[t] * ks[s], axis=-1, keepdims=True) * jnp.exp(Gs[t] - Gs[s])
                o = o + qk * v_news[s]
            o = _rms(o, gout_ref[...])
            z = proj_refs[t][:, z0 + h * B_DIM:z0 + (h + 1) * B_DIM]
            out_ref[t, :, cols] = (o * (z * jax.nn.sigmoid(z))).astype(BF16)
            kd_scr[pl.ds(t, SB, stride=2 * n_pos), :] = ks[t] * jnp.exp(G_last - Gs[t])
            vn_scr[pl.ds(t, SB, stride=2 * n_pos), :] = v_news[t]
        res_scr[0:SB, :] = jnp.broadcast_to(jnp.exp(G_last), (SB, B_DIM))
        for s in range(SB):
            rows = slice(s * 2 * n_pos, (s + 1) * 2 * n_pos)
            gl = res_scr[s:s + 1, 0:1]
            s_out_ref[s, h] = s_ref[s, h] * gl + _dot_tn(kd_scr[rows, :], vn_scr[rows, :])


def _gdn_decode(proj_s, tail_s, cbuf, s0, conv_w, alog_row, dtb_row, g_out, n_seq, n_pos):
    SB = DEC_SEQ_TILE
    hd = B_HEADS * B_DIM
    n_main = proj_s.shape[1]
    per_pos = n_seq // SB
    proj_specs = [pl.BlockSpec((SB, n_main), lambda i, t=t: (t * per_pos + i, 0)) for t in range(n_pos)]
    tail_specs = [pl.BlockSpec((SB, LANES), lambda i, t=t: (t * per_pos + i, 0)) for t in range(n_pos)]
    return pl.pallas_call(
        partial(_gdn_decode_kernel, n_pos=n_pos),
        grid=(per_pos,),
        in_specs=proj_specs + tail_specs + [
            pl.BlockSpec((SB, (CONV_W - 1) * 3 * hd), lambda i: (i, 0)),
            pl.BlockSpec((SB, B_HEADS, B_DIM, B_DIM), lambda i: (i, 0, 0, 0)),
            pl.BlockSpec((CONV_W, 3 * hd), lambda i: (0, 0)),
            pl.BlockSpec((1, LANES), lambda i: (0, 0)),
            pl.BlockSpec((1, LANES), lambda i: (0, 0)),
            pl.BlockSpec((1, B_DIM), lambda i: (0, 0)),
        ],
        out_specs=[
            pl.BlockSpec((n_pos, SB, hd), lambda i: (0, i, 0)),
            pl.BlockSpec((SB, B_HEADS, B_DIM, B_DIM), lambda i: (i, 0, 0, 0)),
            pl.BlockSpec((SB, (CONV_W - 1) * 3 * hd), lambda i: (i, 0)),
        ],
        out_shape=[
            jax.ShapeDtypeStruct((n_pos, n_seq, hd), BF16),
            jax.ShapeDtypeStruct((n_seq, B_HEADS, B_DIM, B_DIM), F32),
            jax.ShapeDtypeStruct((n_seq, (CONV_W - 1) * 3 * hd), F32),
        ],
        scratch_shapes=[
            pltpu.VMEM((SB * 2 * n_pos, B_DIM), F32),
            pltpu.VMEM((SB * 2 * n_pos, B_DIM), F32),
            pltpu.VMEM((SB * 2 * n_pos, B_DIM), F32),
            pltpu.VMEM((SB * 2 * n_pos, B_DIM), F32),
        ],
        compiler_params=_cparams(("parallel",)),
        name="gdn_decode",
    )(*([proj_s] * n_pos), *([tail_s] * n_pos), cbuf, s0, conv_w, alog_row, dtb_row, g_out)


def _outproj_kernel(x_ref, oa_ref, ob_ref, w_ref, g_ref, wr_ref, br_ref,
                    x1_ref, h2_ref, lt_ref, *, time_major):
    a_width = oa_ref.shape[1]
    acc = _dot(oa_ref[...], w_ref[0:a_width, :]) + _dot(ob_ref[...], w_ref[a_width:, :])
    if time_major:
        d = g_ref.shape[-1]
        rows = x_ref.shape[0]
        x = jnp.concatenate([x_ref[:, t * d:(t + 1) * d] for t in range(x_ref.shape[1] // d)], axis=0)
    else:
        x = x_ref[...]
    x1 = x + acc
    x1_ref[...] = x1
    h2 = _rms(x1, g_ref[...])
    h2_ref[...] = h2
    lt_ref[...] = _dot_nt(wr_ref[...], h2, precision=lax.Precision.HIGHEST) + br_ref[:, 0:1]


def _outproj(x2d, out_a, out_b, w_out, g_ffn, wr_t, br_col, *, time_major):
    d = g_ffn.shape[-1]
    a_width = out_a.shape[1]
    n_tok = out_a.shape[0]
    tm = TOKEN_TILE
    if time_major:
        x_spec = pl.BlockSpec(x2d.shape, lambda i: (0, 0))
    else:
        x_spec = pl.BlockSpec((tm, d), lambda i: (i, 0))
    return pl.pallas_call(
        partial(_outproj_kernel, time_major=time_major),
        grid=(n_tok // tm,),
        in_specs=[
            x_spec,
            pl.BlockSpec((tm, a_width), lambda i: (i, 0)),
            pl.BlockSpec((tm, out_b.shape[1]), lambda i: (i, 0)),
            pl.BlockSpec(w_out.shape, lambda i: (0, 0)),
            pl.BlockSpec((1, d), lambda i: (0, 0)),
            pl.BlockSpec(wr_t.shape, lambda i: (0, 0)),
            pl.BlockSpec(br_col.shape, lambda i: (0, 0)),
        ],
        out_specs=[
            pl.BlockSpec((tm, d), lambda i: (i, 0)),
            pl.BlockSpec((tm, d), lambda i: (i, 0)),
            pl.BlockSpec((LANES, tm), lambda i: (0, i)),
        ],
        out_shape=[
            jax.ShapeDtypeStruct((n_tok, d), F32),
            jax.ShapeDtypeStruct((n_tok, d), F32),
            jax.ShapeDtypeStruct((LANES, n_tok), F32),
        ],
        compiler_params=_cparams(("parallel",)),
        name="outproj_tm" if time_major else "outproj",
    )(x2d, out_a, out_b, w_out, g_ffn, wr_t, br_col)


def _router_kernel(lp_ref, ls_ref, dest_ref, wcol_ref, meta_ref, cnt_scr, base_scr, *, n_prompt_tiles):
    p = pl.program_id(0)
    i = pl.program_id(1)
    tm = lp_ref.shape[1]
    lt = jnp.where(i < n_prompt_tiles, lp_ref[...], ls_ref[...])

    m = lt[0:1, :]
    sel = jnp.zeros((1, tm), I32)
    for r in range(1, N_GROUPS):
        upd = lt[r:r + 1, :] > m
        sel = jnp.where(upd, r, sel)
        m = jnp.where(upd, lt[r:r + 1, :], m)
    den = jnp.zeros((1, tm), F32)
    for r in range(N_GROUPS):
        den = den + jnp.exp(lt[r:r + 1, :] - m)
    g_w = 1.0 / den

    ev = []
    for j in range(EXPERTS_PER_GROUP):
        e = jnp.zeros((1, tm), F32)
        for g in range(N_GROUPS):
            row = N_GROUPS + g * EXPERTS_PER_GROUP + j
            e = jnp.where(sel == g, lt[row:row + 1, :], e)
        ev.append(e)
    v0 = ev[0]
    i0 = jnp.zeros((1, tm), I32)
    for j in range(1, EXPERTS_PER_GROUP):
        upd = ev[j] > v0
        i0 = jnp.where(upd, j, i0)
        v0 = jnp.where(upd, ev[j], v0)
    v1 = jnp.full((1, tm), -jnp.inf, F32)
    i1 = jnp.zeros((1, tm), I32)
    for j in range(EXPERTS_PER_GROUP):
        upd = (ev[j] > v1) & (i0 != j)
        i1 = jnp.where(upd, j, i1)
        v1 = jnp.where(upd, ev[j], v1)
    t = jnp.exp(v1 - v0)
    w0 = g_w / (1.0 + t)
    w1 = g_w * t / (1.0 + t)
    e0 = sel * EXPERTS_PER_GROUP + i0
    e1 = sel * EXPERTS_PER_GROUP + i1

    eio = lax.broadcasted_iota(I32, (N_EXPERTS, tm), 0)
    hit0 = eio == e0
    hit1 = eio == e1
    onehot = (hit0 | hit1).astype(F32)
    tile_cnt = jnp.sum(onehot, axis=1, keepdims=True)

    @pl.when((p == 0) & (i == 0))
    def _():
        cnt_scr[...] = jnp.zeros_like(cnt_scr)

    @pl.when(p == 0)
    def _():
        cnt_scr[...] = cnt_scr[...] + tile_cnt

    @pl.when((p == 1) & (i == 0))
    def _():
        cnt = cnt_scr[...]
        padded = jnp.floor((cnt + (MOE_BLOCK - 1)) * (1.0 / MOE_BLOCK)) * MOE_BLOCK
        ri = lax.broadcasted_iota(I32, (N_EXPERTS, N_EXPERTS), 0)
        ci = lax.broadcasted_iota(I32, (N_EXPERTS, N_EXPERTS), 1)
        lower = (ci < ri).astype(F32)
        pad_start = jnp.dot(lower, jnp.broadcast_to(padded, (N_EXPERTS, LANES)),
                            precision=lax.Precision.HIGHEST, preferred_element_type=F32)[:, 0:1]
        base_scr[...] = pad_start
        pad_end = pad_start + padded
        blk0 = (lax.broadcasted_iota(I32, (N_EXPERTS, LANES), 1) * MOE_BLOCK).astype(F32)
        n_le = jnp.sum((pad_end <= blk0).astype(F32), axis=0, keepdims=True)
        block_e = jnp.minimum(n_le, N_EXPERTS - 1.0)
        n_active = jnp.broadcast_to(pad_end[N_EXPERTS - 1:N_EXPERTS, :] * (1.0 / MOE_BLOCK), (1, LANES))
        meta_ref[...] = jnp.concatenate(
            [block_e, n_active, jnp.zeros((6, LANES), F32)], axis=0).astype(I32)

    @pl.when(p == 1)
    def _():
        ui = lax.broadcasted_iota(I32, (tm, tm), 0)
        uj = lax.broadcasted_iota(I32, (tm, tm), 1)
        upper = (ui < uj).astype(BF16)
        excl = _dot(onehot.astype(BF16), upper)
        pos = base_scr[...] + excl
        d0 = jnp.sum(jnp.where(hit0, pos, 0.0), axis=0, keepdims=True)
        d1 = jnp.sum(jnp.where(hit1, pos, 0.0), axis=0, keepdims=True)
        dest_ref[...] = jnp.concatenate([d0, d1, jnp.zeros((6, tm), F32)], axis=0).astype(I32)
        wmat = jnp.concatenate([w0, w1, jnp.zeros((LANES - 2, tm), F32)], axis=0)
        wcol_ref[...] = wmat.T
        base_scr[...] = base_scr[...] + tile_cnt


def _router(lt_p, lt_s):
    tm = TOKEN_TILE
    n_p = lt_p.shape[1] // tm
    n_s = lt_s.shape[1] // tm
    n_tok = lt_p.shape[1] + lt_s.shape[1]
    return pl.pallas_call(
        partial(_router_kernel, n_prompt_tiles=n_p),
        grid=(2, n_p + n_s),
        in_specs=[
            pl.BlockSpec((LANES, tm), lambda p, i: (0, jnp.minimum(i, n_p - 1))),
            pl.BlockSpec((LANES, tm), lambda p, i: (0, jnp.maximum(i - n_p, 0))),
        ],
        out_specs=[
            pl.BlockSpec((8, tm), lambda p, i: (0, i * p)),
            pl.BlockSpec((tm, LANES), lambda p, i: (i * p, 0)),
            pl.BlockSpec((8, LANES), lambda p, i: (0, 0)),
        ],
        out_shape=[
            jax.ShapeDtypeStruct((8, n_tok), I32),
            jax.ShapeDtypeStruct((n_tok, LANES), F32),
            jax.ShapeDtypeStruct((8, LANES), I32),
        ],
        scratch_shapes=[pltpu.VMEM((N_EXPERTS, 1), F32), pltpu.VMEM((N_EXPERTS, 1), F32)],
        compiler_params=_cparams(("arbitrary", "arbitrary")),
        name="router",
    )(lt_p, lt_s)


def _dispatch_kernel(d0_ref, d1_ref, h_ref, xb_in_ref, xb_ref, sem):
    del xb_in_ref
    i = pl.program_id(0)
    tm = h_ref.shape[0]

    def copy(r, d_ref):
        return pltpu.make_async_copy(h_ref.at[pl.ds(r, 1)], xb_ref.at[pl.ds(d_ref[i * tm + r], 1)], sem)

    def issue(r, carry):
        copy(r, d0_ref).start()
        copy(r, d1_ref).start()
        return carry

    lax.fori_loop(0, tm, issue, 0)

    def drain(r, carry):
        copy(r, d0_ref).wait()
        copy(r, d1_ref).wait()
        return carry

    lax.fori_loop(0, tm, drain, 0)


def _dispatch(d0, d1, h2, xb):
    tm = TOKEN_TILE
    d = h2.shape[1]
    return pl.pallas_call(
        _dispatch_kernel,
        grid_spec=pltpu.PrefetchScalarGridSpec(
            num_scalar_prefetch=2,
            grid=(h2.shape[0] // tm,),
            in_specs=[
                pl.BlockSpec((tm, d), lambda i, *_: (i, 0)),
                pl.BlockSpec(memory_space=pl.ANY),
            ],
            out_specs=pl.BlockSpec(memory_space=pl.ANY),
            scratch_shapes=[pltpu.SemaphoreType.DMA(())],
        ),
        out_shape=jax.ShapeDtypeStruct(xb.shape, xb.dtype),
        input_output_aliases={3: 0},
        compiler_params=_cparams(("arbitrary",)),
        name="dispatch",
    )(d0, d1, h2, xb)


def _ffn_kernel(be_ref, na_ref, xb_ref, wg_ref, wu_ref, wd_ref, yb_ref):
    b = pl.program_id(0)

    @pl.when(b < na_ref[0])
    def _():
        x = xb_ref[...].astype(BF16)
        g = _dot(x, wg_ref[0].astype(BF16))
        u = _dot(x, wu_ref[0].astype(BF16))
        a = (g * jax.nn.sigmoid(g) * u).astype(BF16)
        yb_ref[...] = _dot(a, wd_ref[0].astype(BF16))

    @pl.when(b >= na_ref[0])
    def _():
        yb_ref[...] = jnp.zeros_like(yb_ref)


def _ffn(block_e, n_active, xb, w_gate, w_up, w_down):
    n_rows, d = xb.shape
    d_e = w_gate.shape[-1]
    n_blocks = n_rows // MOE_BLOCK

    def live(b, na):
        return jnp.minimum(b, jnp.maximum(na[0] - 1, 0))

    return pl.pallas_call(
        _ffn_kernel,
        grid_spec=pltpu.PrefetchScalarGridSpec(
            num_scalar_prefetch=2,
            grid=(n_blocks,),
            in_specs=[
                pl.BlockSpec((MOE_BLOCK, d), lambda b, be, na: (live(b, na), 0)),
                pl.BlockSpec((1, d, d_e), lambda b, be, na: (be[live(b, na)], 0, 0)),
                pl.BlockSpec((1, d, d_e), lambda b, be, na: (be[live(b, na)], 0, 0)),
                pl.BlockSpec((1, d_e, d), lambda b, be, na: (be[live(b, na)], 0, 0)),
            ],
            out_specs=pl.BlockSpec((MOE_BLOCK, d), lambda b, be, na: (b, 0)),
        ),
        out_shape=jax.ShapeDtypeStruct((n_rows, d), F32),
        compiler_params=_cparams(("arbitrary",)),
        name="moe_ffn",
    )(block_e, n_active, xb, w_gate, w_up, w_down)


def _ple_kernel(d0_ref, d1_ref, x1_ref, wcol_ref, p_ref, yb_ref, gple_ref, wpg_ref, wp_ref, gfin_ref,
                y_ref, g0_scr, g1_scr, sem, *, time_major):
    i = pl.program_id(0)
    tm = x1_ref.shape[0]

    def copy(r, d_ref, dst):
        return pltpu.make_async_copy(yb_ref.at[pl.ds(d_ref[i * tm + r], 1)], dst.at[pl.ds(r, 1)], sem)

    def issue(r, carry):
        copy(r, d0_ref, g0_scr).start()
        copy(r, d1_ref, g1_scr).start()
        return carry

    lax.fori_loop(0, tm, issue, 0)

    def drain(r, carry):
        copy(r, d0_ref, g0_scr).wait()
        copy(r, d1_ref, g1_scr).wait()
        return carry

    lax.fori_loop(0, tm, drain, 0)

    x2 = x1_ref[...] + wcol_ref[:, 0:1] * g0_scr[...] + wcol_ref[:, 1:2] * g1_scr[...]
    hn = _rms(x2, gple_ref[...]).astype(BF16)
    gate = jax.nn.sigmoid(_dot(hn, wpg_ref[...]))
    if time_major:
        pd = wp_ref.shape[0]
        pp = jnp.concatenate([p_ref[:, t * pd:(t + 1) * pd] for t in range(p_ref.shape[1] // pd)], axis=0)
    else:
        pp = p_ref[...]
    x3 = x2 + _dot(pp.astype(BF16), wp_ref[...]) * gate
    y = _rms(x3, gfin_ref[...])
    if time_major:
        d = y.shape[1]
        rows = y_ref.shape[0]
        for t in range(y_ref.shape[1] // d):
            y_ref[:, t * d:(t + 1) * d] = y[t * rows:(t + 1) * rows, :]
    else:
        y_ref[...] = y


def _ple(d0, d1, x1, wcol, p2d, yb, g_ple, w_pg, w_p, g_final, *, time_major):
    n_tok, d = x1.shape
    tm = TOKEN_TILE
    pd = w_p.shape[0]
    if time_major:
        n_seq = p2d.shape[0]
        p_spec = pl.BlockSpec(p2d.shape, lambda i, *_: (0, 0))
        y_spec = pl.BlockSpec((n_seq, (n_tok // n_seq) * d), lambda i, *_: (0, 0))
        y_shape = jax.ShapeDtypeStruct((n_seq, (n_tok // n_seq) * d), F32)
    else:
        p_spec = pl.BlockSpec((tm, pd), lambda i, *_: (i, 0))
        y_spec = pl.BlockSpec((tm, d), lambda i, *_: (i, 0))
        y_shape = jax.ShapeDtypeStruct((n_tok, d), F32)
    return pl.pallas_call(
        partial(_ple_kernel, time_major=time_major),
        grid_spec=pltpu.PrefetchScalarGridSpec(
            num_scalar_prefetch=2,
            grid=(n_tok // tm,),
            in_specs=[
                pl.BlockSpec((tm, d), lambda i, *_: (i, 0)),
                pl.BlockSpec((tm, LANES), lambda i, *_: (i, 0)),
                p_spec,
                pl.BlockSpec(memory_space=pl.ANY),
                pl.BlockSpec((1, d), lambda i, *_: (0, 0)),
                pl.BlockSpec(w_pg.shape, lambda i, *_: (0, 0)),
                pl.BlockSpec(w_p.shape, lambda i, *_: (0, 0)),
                pl.BlockSpec((1, d), lambda i, *_: (0, 0)),
            ],
            out_specs=y_spec,
            scratch_shapes=[
                pltpu.VMEM((tm, d), F32),
                pltpu.VMEM((tm, d), F32),
                pltpu.SemaphoreType.DMA(()),
            ],
        ),
        out_shape=y_shape,
        compiler_params=_cparams(("arbitrary",)),
        name="ple_tm" if time_major else "ple",
    )(d0, d1, x1, wcol, p2d, yb, g_ple, w_pg, w_p, g_final)


def _pad_lanes(v, offset=0):
    row = jnp.zeros((1, LANES), F32)
    return row.at[0, offset:offset + v.shape[0]].set(v.astype(F32))


def kernel(x_prompt, x_sample, state_conv, state_delta, p_prompt, p_sample, g_mix, w_in, w_s, b_s, g_v,
           conv_w, a_log, dt_bias, g_out, w_out, g_ffn, w_group, b_group, w_router, b_router, w_gate, w_up,
           w_down, g_ple, w_ple_gate, w_ple, g_final):
    n_batch, seq, d = x_prompt.shape
    n_seq, n_pos, _ = x_sample.shape
    depth = g_mix.shape[0]
    assert depth == 1
    a_width = A_GROUPS * A_GROUP_DIM
    hd = B_HEADS * B_DIM
    n_main = 2 * a_width + 4 * hd
    n_p = n_batch * seq
    n_s = n_seq * n_pos

    xp = x_prompt.reshape(n_p, d)
    xs = x_sample.reshape(n_seq, n_pos * d)
    l = 0

    w_main = w_in[l].astype(BF16)
    w_tail = jnp.pad(w_in[l][:, n_main:], ((0, 0), (0, LANES - 2 * B_HEADS))).astype(BF16)
    gm = g_mix[l].reshape(1, d)
    causal = jnp.tril(jnp.ones((A_CHUNK, A_CHUNK), dtype=bool))
    ws_tril = jnp.where(causal, w_s[l], 0).astype(BF16)
    b_t = b_s[l].T
    gv = g_v[l].reshape(1, a_width)
    alog_row = _pad_lanes(a_log[l])
    dtb_row = _pad_lanes(dt_bias[l])
    gout = g_out[l].reshape(1, B_DIM)
    w_o = w_out[l].astype(BF16)
    gf = g_ffn[l].reshape(1, d)
    wr_t = jnp.pad(jnp.concatenate([w_group[l], w_router[l]], axis=1).T,
                   ((0, LANES - N_GROUPS - N_EXPERTS), (0, 0)))
    br_col = jnp.broadcast_to(
        jnp.pad(jnp.concatenate([b_group[l], b_router[l]]), (0, LANES - N_GROUPS - N_EXPERTS))[:, None],
        (LANES, LANES))
    gp = g_ple[l].reshape(1, d)
    w_pg = w_ple_gate[l].astype(BF16)
    w_p = w_ple[l].astype(BF16)
    gfin = g_final.reshape(1, d)

    proj_p, tail_p = _inproj(xp, gm, w_main, w_tail, n_main, time_major=False)
    proj_s, tail_s = _inproj(xs, gm, w_main, w_tail, n_main, time_major=True)
    oa_p = _mixa_prompt(proj_p, ws_tril, b_t, gv, n_p)
    oa_s, va_s = _mixa_sample(proj_s, w_s[l], b_s[l], gv, n_seq, n_pos)
    ob_p, sd_p, cb_p = _gdn_prompt(proj_p, tail_p, conv_w[l], alog_row, dtb_row, gout, n_batch, seq)
    cbuf = state_conv[l].reshape(n_seq, (CONV_W - 1) * 3 * hd)
    ob_s, sd_s, cb_s = _gdn_decode(proj_s, tail_s, cbuf, state_delta[l], conv_w[l], alog_row, dtb_row,
                                   gout, n_seq, n_pos)
    ob_s = ob_s.reshape(n_s, hd)

    x1_p, h2_p, lt_p = _outproj(xp, oa_p, ob_p, w_o, gf, wr_t, br_col, time_major=False)
    x1_s, h2_s, lt_s = _outproj(xs, oa_s, ob_s, w_o, gf, wr_t, br_col, time_major=True)
    dest, wcol, meta = _router(lt_p, lt_s)
    d0, d1 = dest[0], dest[1]
    block_e = meta[0]
    n_active = meta[1, 0:1]

    n_tok = n_p + n_s
    n_blocks = (n_tok * 2) // MOE_BLOCK + N_EXPERTS
    xb = jnp.zeros((n_blocks * MOE_BLOCK, d), F32)
    xb = _dispatch(d0[:n_p], d1[:n_p], h2_p, xb)
    xb = _dispatch(d0[n_p:], d1[n_p:], h2_s, xb)
    yb = _ffn(block_e[:n_blocks], n_active, xb, w_gate[l], w_up[l], w_down[l])

    pp = p_prompt[l].reshape(n_p, -1)
    ps = p_sample[l].reshape(n_seq, -1)
    y_p = _ple(d0[:n_p], d1[:n_p], x1_p, wcol[:n_p], pp, yb, gp, w_pg, w_p, gfin, time_major=False)
    y_s = _ple(d0[n_p:], d1[n_p:], x1_s, wcol[n_p:], ps, yb, gp, w_pg, w_p, gfin, time_major=True)

    return (
        y_p.reshape(n_batch, seq, d),
        y_s.reshape(n_seq, n_pos, d),
        cb_p[None],
        sd_p[None],
        cb_s.reshape(n_seq, CONV_W - 1, 3 * hd)[None],
        sd_s[None],
        va_s.reshape(n_seq, n_pos, a_width)[None],
    )
```

```python
from functools import partial

import jax
import jax.numpy as jnp
from jax import lax
from jax.experimental import pallas as pl
from jax.experimental.pallas import tpu as pltpu

F32 = jnp.float32
BF16 = jnp.bfloat16
I32 = jnp.int32
U32 = jnp.uint32

EPS = 1e-6
LANES = 128
A_GROUPS = 8
A_GROUP_DIM = 128
A_CHUNK = 128
B_HEADS = 8
B_DIM = 128
CONV_W = 4
N_GROUPS = 4
EXPERTS_PER_GROUP = 8
N_EXPERTS = N_GROUPS * EXPERTS_PER_GROUP
ROUTER_ROWS = 40
TOKEN_TILE = 512
INPROJ_TILE = 1024
GDN_CHUNK = 64
MOE_BLOCK = 256
DEC_SEQ_TILE = 8
ROW_DMA_UNROLL = 8
VMEM_LIMIT = 56 * 1024 * 1024


def _cparams(sem):
    return pltpu.CompilerParams(dimension_semantics=sem, vmem_limit_bytes=VMEM_LIMIT)


def _rms(x, g):
    return x * lax.rsqrt(jnp.mean(x * x, axis=-1, keepdims=True) + EPS) * g


def _softplus(x):
    return jnp.maximum(x, 0.0) + jnp.log(1.0 + jnp.exp(-jnp.abs(x)))


def _dot(a, b):
    return jnp.dot(a, b, preferred_element_type=F32)


def _dot_nt(a, b, precision=None):
    return lax.dot_general(a, b, (((1,), (1,)), ((), ())), precision=precision,
                           preferred_element_type=F32)


def _dot_tn(a, b):
    return lax.dot_general(a, b, (((0,), (0,)), ((), ())), preferred_element_type=F32)


def _pack_bf16_pairs(x):
    n = x.shape[1] // 2
    lo = lax.bitcast_convert_type(x[:, :n].astype(BF16).astype(F32), U32) >> 16
    hi = lax.bitcast_convert_type(x[:, n:].astype(BF16).astype(F32), U32) & jnp.uint32(0xFFFF0000)
    return hi | lo


def _unpack_bf16_pairs(xu):
    lo = lax.bitcast_convert_type(xu << 16, F32).astype(BF16)
    hi = lax.bitcast_convert_type(xu & jnp.uint32(0xFFFF0000), F32).astype(BF16)
    return lo, hi


def _inproj_kernel(x_ref, g_ref, w_ref, wt_ref, proj_ref, tail_ref, h_scr, *, time_major):
    j = pl.program_id(1)

    @pl.when(j == 0)
    def _():
        g = g_ref[...]
        if time_major:
            d = g.shape[-1]
            rows = x_ref.shape[0]
            for t in range(x_ref.shape[1] // d):
                h_scr[t * rows:(t + 1) * rows, :] = _rms(x_ref[:, t * d:(t + 1) * d], g).astype(BF16)
        else:
            h_scr[...] = _rms(x_ref[...], g).astype(BF16)
        tail_ref[...] = _dot(h_scr[...], wt_ref[...])

    proj_ref[...] = _dot(h_scr[...], w_ref[...])


def _inproj(x2d, g_mix, w_main, w_tail, n_main, *, time_major, col_tile=1024):
    d = g_mix.shape[-1]
    if time_major:
        n_tok = x2d.shape[0] * (x2d.shape[1] // d)
        tm = n_tok
        x_spec = pl.BlockSpec(x2d.shape, lambda i, j: (0, 0))
    else:
        n_tok = x2d.shape[0]
        tm = min(INPROJ_TILE, n_tok)
        x_spec = pl.BlockSpec((tm, d), lambda i, j: (i, 0))
    grid = (n_tok // tm, n_main // col_tile)
    return pl.pallas_call(
        partial(_inproj_kernel, time_major=time_major),
        grid=grid,
        in_specs=[
            x_spec,
            pl.BlockSpec((1, d), lambda i, j: (0, 0)),
            pl.BlockSpec((d, col_tile), lambda i, j: (0, j)),
            pl.BlockSpec((d, LANES), lambda i, j: (0, 0)),
        ],
        out_specs=[
            pl.BlockSpec((tm, col_tile), lambda i, j: (i, j)),
            pl.BlockSpec((tm, LANES), lambda i, j: (i, 0)),
        ],
        out_shape=[
            jax.ShapeDtypeStruct((n_tok, n_main), F32),
            jax.ShapeDtypeStruct((n_tok, LANES), F32),
        ],
        scratch_shapes=[pltpu.VMEM((tm, d), BF16)],
        compiler_params=_cparams(("parallel", "arbitrary")),
        name="inproj_tm" if time_major else "inproj",
    )(x2d, g_mix, w_main, w_tail)


def _group_ln(v, g):
    mu = jnp.mean(v, axis=-1, keepdims=True)
    dlt = v - mu
    var = jnp.mean(dlt * dlt, axis=-1, keepdims=True)
    return dlt * lax.rsqrt(var + EPS) * g


def _mixa_prompt_kernel(au_ref, av_ref, ws_ref, bt_ref, gv_ref, out_ref):
    n_chunks = au_ref.shape[0] // A_CHUNK
    for c in range(n_chunks):
        rows = slice(c * A_CHUNK, (c + 1) * A_CHUNK)
        for h in range(A_GROUPS):
            cols = slice(h * A_GROUP_DIM, (h + 1) * A_GROUP_DIM)
            u = jax.nn.gelu(au_ref[rows, cols])
            v = _group_ln(jax.nn.gelu(av_ref[rows, cols]), gv_ref[:, cols])
            mixed = _dot(ws_ref[h], v.astype(BF16)) + bt_ref[:, h:h + 1]
            out_ref[rows, cols] = (u * mixed).astype(BF16)


def _mixa_prompt(proj, ws_tril, b_t, g_v, n_tok):
    a_width = A_GROUPS * A_GROUP_DIM
    tm = TOKEN_TILE
    return pl.pallas_call(
        _mixa_prompt_kernel,
        grid=(n_tok // tm,),
        in_specs=[
            pl.BlockSpec((tm, a_width), lambda i: (i, 0)),
            pl.BlockSpec((tm, a_width), lambda i: (i, 1)),
            pl.BlockSpec((A_GROUPS, A_CHUNK, A_CHUNK), lambda i: (0, 0, 0)),
            pl.BlockSpec((A_CHUNK, A_GROUPS), lambda i: (0, 0)),
            pl.BlockSpec((1, a_width), lambda i: (0, 0)),
        ],
        out_specs=pl.BlockSpec((tm, a_width), lambda i: (i, 0)),
        out_shape=jax.ShapeDtypeStruct((n_tok, a_width), BF16),
        compiler_params=_cparams(("parallel",)),
        name="mixa_prompt",
    )(proj, proj, ws_tril, b_t, g_v)


def _mixa_sample_kernel(ws_ref, bs_ref, au_ref, av_ref, gv_ref, out_ref, va_ref, *, n_seq, n_pos):
    a_width = A_GROUPS * A_GROUP_DIM
    vs = []
    for t in range(n_pos):
        rows = slice(t * n_seq, (t + 1) * n_seq)
        per_group = []
        for h in range(A_GROUPS):
            cols = slice(h * A_GROUP_DIM, (h + 1) * A_GROUP_DIM)
            per_group.append(_group_ln(jax.nn.gelu(av_ref[rows, cols]), gv_ref[:, cols]))
        vs.append(per_group)
        for h in range(A_GROUPS):
            va_ref[:, t * a_width + h * A_GROUP_DIM:t * a_width + (h + 1) * A_GROUP_DIM] = per_group[h]
    for t in range(n_pos):
        rows = slice(t * n_seq, (t + 1) * n_seq)
        for h in range(A_GROUPS):
            cols = slice(h * A_GROUP_DIM, (h + 1) * A_GROUP_DIM)
            mixed = jnp.zeros((n_seq, A_GROUP_DIM), F32) + bs_ref[h * n_pos + t]
            for s in range(t + 1):
                mixed = mixed + ws_ref[(h * n_pos + t) * n_pos + s] * vs[s][h]
            out_ref[rows, cols] = (jax.nn.gelu(au_ref[rows, cols]) * mixed).astype(BF16)


def _mixa_sample(proj, w_s, b_s, g_v, n_seq, n_pos):
    a_width = A_GROUPS * A_GROUP_DIM
    n_tok = n_seq * n_pos
    return pl.pallas_call(
        partial(_mixa_sample_kernel, n_seq=n_seq, n_pos=n_pos),
        grid_spec=pltpu.PrefetchScalarGridSpec(
            num_scalar_prefetch=2,
            grid=(1,),
            in_specs=[
                pl.BlockSpec((n_tok, a_width), lambda i, *_: (0, 0)),
                pl.BlockSpec((n_tok, a_width), lambda i, *_: (0, 1)),
                pl.BlockSpec((1, a_width), lambda i, *_: (0, 0)),
            ],
            out_specs=[
                pl.BlockSpec((n_tok, a_width), lambda i, *_: (0, 0)),
                pl.BlockSpec((n_seq, n_pos * a_width), lambda i, *_: (0, 0)),
            ],
        ),
        out_shape=[
            jax.ShapeDtypeStruct((n_tok, a_width), BF16),
            jax.ShapeDtypeStruct((n_seq, n_pos * a_width), F32),
        ],
        compiler_params=_cparams(("arbitrary",)),
        name="mixa_sample",
    )(w_s[:, :n_pos, :n_pos].reshape(-1), b_s[:, :n_pos].reshape(-1), proj, proj, g_v)


def _gdn_prompt_kernel(q_ref, k_ref, v_ref, z_ref, tail_ref, cw_ref, alog_ref, dtb_ref, gout_ref,
                       out_ref, s_out_ref, conv_out_ref, s_scr, xbuf):
    c = pl.program_id(1)
    n_c = pl.num_programs(1)
    C = q_ref.shape[0]
    hd = B_HEADS * B_DIM
    pad = 8
    lo = pad - (CONV_W - 1)

    @pl.when(c == 0)
    def _():
        s_scr[...] = jnp.zeros_like(s_scr)
        xbuf[0:pad, :] = jnp.zeros((pad, 3 * hd), F32)

    xbuf[pad:pad + C, 0:hd] = q_ref[...]
    xbuf[pad:pad + C, hd:2 * hd] = k_ref[...]
    xbuf[pad:pad + C, 2 * hd:3 * hd] = v_ref[...]
    y = xbuf[lo:lo + C, :] * cw_ref[0:1, :]
    for j in range(1, CONV_W):
        y = y + xbuf[lo + j:lo + j + C, :] * cw_ref[j:j + 1, :]
    y = y * jax.nn.sigmoid(y)
    last = xbuf[pad + C - (CONV_W - 1):pad + C, :]
    xbuf[lo:pad, :] = last

    @pl.when(c == n_c - 1)
    def _():
        conv_out_ref[0] = last

    tail = tail_ref[...]
    g_all = -jnp.exp(alog_ref[...]) * _softplus(tail + dtb_ref[...])
    beta_all = jax.nn.sigmoid(tail)
    ri = lax.broadcasted_iota(I32, (C, C), 0)
    ci = lax.broadcasted_iota(I32, (C, C), 1)
    causal = ci <= ri
    strict = ci < ri
    tril = causal.astype(F32)
    G_all = jnp.dot(tril, g_all, precision=lax.Precision.HIGHEST, preferred_element_type=F32)
    G_rows = jnp.concatenate([G_all, jnp.zeros((LANES - C, LANES), F32)], axis=0).T if C < LANES else G_all.T
    scale = B_DIM ** -0.5
    n_levels = max(C.bit_length() - 2, 0)
    heads = range(B_HEADS)

    qs, ks, Gs, eGs, rhss, qks, Ps = [], [], [], [], [], [], []
    for h in heads:
        q = y[:, h * B_DIM:(h + 1) * B_DIM]
        k = y[:, hd + h * B_DIM:hd + (h + 1) * B_DIM]
        v = y[:, 2 * hd + h * B_DIM:2 * hd + (h + 1) * B_DIM]
        q = q * lax.rsqrt(jnp.sum(q * q, axis=-1, keepdims=True) + EPS) * scale
        k = k * lax.rsqrt(jnp.sum(k * k, axis=-1, keepdims=True) + EPS)
        G = G_all[:, h:h + 1]
        beta = beta_all[:, B_HEADS + h:B_HEADS + h + 1]
        eG = jnp.exp(G)
        decay = jnp.exp(jnp.where(causal, G - G_rows[h:h + 1, 0:C], -jnp.inf))
        qk_kk = _dot_nt(jnp.concatenate([q, k], axis=0), k)
        qks.append(qk_kk[:C] * decay)
        Ps.append(jnp.where(strict, -(beta * qk_kk[C:] * decay), 0.0))
        rhss.append(jnp.concatenate([v * beta, k * (beta * eG)], axis=1))
        qs.append(q)
        ks.append(k)
        Gs.append(G)
        eGs.append(eG)

    Ls = list(Ps)
    if n_levels > 0:
        Ps = [_dot(P, P) for P in Ps]
    for lvl in range(1, n_levels + 1):
        for h in heads:
            if lvl < n_levels:
                R = _dot(jnp.concatenate([Ps[h], Ls[h]], axis=0), Ps[h])
                Ls[h] = Ls[h] + Ps[h] + R[C:]
                Ps[h] = R[:C]
            else:
                Ls[h] = Ls[h] + Ps[h] + _dot(Ls[h], Ps[h])

    uws = [rhss[h] + _dot(Ls[h], rhss[h]) for h in heads]
    Ss = [s_scr[h] for h in heads]
    rs = [_dot(jnp.concatenate([uws[h][:, B_DIM:], qs[h] * eGs[h]], axis=0), Ss[h]) for h in heads]
    v_news = [uws[h][:, :B_DIM] - rs[h][:C] for h in heads]
    os_ = [rs[h][C:] + _dot(qks[h], v_news[h]) for h in heads]
    for h in heads:
        G_last = Gs[h][C - 1:C, :]
        kd = ks[h] * jnp.exp(G_last - Gs[h])
        s_scr[h] = Ss[h] * jnp.exp(G_last) + _dot_tn(kd, v_news[h])
    for h in heads:
        cols = slice(h * B_DIM, (h + 1) * B_DIM)
        o = _rms(os_[h], gout_ref[...])
        z = z_ref[:, cols]
        out_ref[:, cols] = (o * (z * jax.nn.sigmoid(z))).astype(BF16)

    @pl.when(c == n_c - 1)
    def _():
        s_out_ref[0] = s_scr[...]


def _gdn_prompt(proj, tail, conv_w, alog_row, dtb_row, g_out, n_batch, seq):
    C = GDN_CHUNK
    n_c = seq // C
    hd = B_HEADS * B_DIM
    row = lambda b, c: b * n_c + c
    return pl.pallas_call(
        _gdn_prompt_kernel,
        grid=(n_batch, n_c),
        in_specs=[
            pl.BlockSpec((C, hd), lambda b, c: (row(b, c), 2)),
            pl.BlockSpec((C, hd), lambda b, c: (row(b, c), 3)),
            pl.BlockSpec((C, hd), lambda b, c: (row(b, c), 4)),
            pl.BlockSpec((C, hd), lambda b, c: (row(b, c), 5)),
            pl.BlockSpec((C, LANES), lambda b, c: (row(b, c), 0)),
            pl.BlockSpec((CONV_W, 3 * hd), lambda b, c: (0, 0)),
            pl.BlockSpec((1, LANES), lambda b, c: (0, 0)),
            pl.BlockSpec((1, LANES), lambda b, c: (0, 0)),
            pl.BlockSpec((1, B_DIM), lambda b, c: (0, 0)),
        ],
        out_specs=[
            pl.BlockSpec((C, hd), lambda b, c: (row(b, c), 0)),
            pl.BlockSpec((1, B_HEADS, B_DIM, B_DIM), lambda b, c: (b, 0, 0, 0)),
            pl.BlockSpec((1, CONV_W - 1, 3 * hd), lambda b, c: (b, 0, 0)),
        ],
        out_shape=[
            jax.ShapeDtypeStruct((n_batch * seq, hd), BF16),
            jax.ShapeDtypeStruct((n_batch, B_HEADS, B_DIM, B_DIM), F32),
            jax.ShapeDtypeStruct((n_batch, CONV_W - 1, 3 * hd), F32),
        ],
        scratch_shapes=[
            pltpu.VMEM((B_HEADS, B_DIM, B_DIM), F32),
            pltpu.VMEM((8 + C, 3 * hd), F32),
        ],
        compiler_params=_cparams(("parallel", "arbitrary")),
        name="gdn_prompt",
    )(proj, proj, proj, proj, tail, conv_w, alog_row, dtb_row, g_out)


def _gdn_decode_kernel(*refs, n_pos):
    proj_refs = refs[:n_pos]
    tail_refs = refs[n_pos:2 * n_pos]
    (cbuf_ref, s_ref, cw_ref, alog_ref, dtb_ref, gout_ref,
     out_ref, s_out_ref, conv_out_ref, lhs_scr, res_scr, kd_scr, vn_scr) = refs[2 * n_pos:]
    SB = s_ref.shape[0]
    hd = B_HEADS * B_DIM
    qkv0 = 2 * hd
    z0 = qkv0 + 3 * hd
    scale = B_DIM ** -0.5

    xp = [cbuf_ref[:, j * 3 * hd:(j + 1) * 3 * hd] for j in range(CONV_W - 1)]
    xp += [proj_refs[t][:, qkv0:qkv0 + 3 * hd] for t in range(n_pos)]
    for j in range(CONV_W - 1):
        conv_out_ref[:, j * 3 * hd:(j + 1) * 3 * hd] = xp[n_pos + j]
    ys = []
    for t in range(n_pos):
        y = xp[t] * cw_ref[0:1, :]
        for j in range(1, CONV_W):
            y = y + xp[t + j] * cw_ref[j:j + 1, :]
        ys.append(y * jax.nn.sigmoid(y))
    gs, betas = [], []
    for t in range(n_pos):
        tail = tail_refs[t][...]
        gs.append(-jnp.exp(alog_ref[...]) * _softplus(tail + dtb_ref[...]))
        betas.append(jax.nn.sigmoid(tail))

    kd_scr[...] = jnp.zeros_like(kd_scr)
    vn_scr[...] = jnp.zeros_like(vn_scr)

    for h in range(B_HEADS):
        cols = slice(h * B_DIM, (h + 1) * B_DIM)
        qs, ks, vs, Gs, bs = [], [], [], [], []
        G = None
        for t in range(n_pos):
            q = ys[t][:, h * B_DIM:(h + 1) * B_DIM]
            k = ys[t][:, hd + h * B_DIM:hd + (h + 1) * B_DIM]
            qs.append(q * lax.rsqrt(jnp.sum(q * q, axis=-1, keepdims=True) + EPS) * scale)
            ks.append(k * lax.rsqrt(jnp.sum(k * k, axis=-1, keepdims=True) + EPS))
            vs.append(ys[t][:, 2 * hd + h * B_DIM:2 * hd + (h + 1) * B_DIM])
            g = gs[t][:, h:h + 1]
            G = g if G is None else G + g
            Gs.append(G)
            bs.append(betas[t][:, B_HEADS + h:B_HEADS + h + 1])
        us, ws = [], []
        for t in range(n_pos):
            u = vs[t] * bs[t]
            w = ks[t] * (bs[t] * jnp.exp(Gs[t]))
            for s in range(t):
                a = bs[t] * jnp.sum(ks[t] * ks[s], axis=-1, keepdims=True) * jnp.exp(Gs[t] - Gs[s])
                u = u - a * us[s]
                w = w - a * ws[s]
            us.append(u)
            ws.append(w)
        for t in range(n_pos):
            lhs_scr[pl.ds(t, SB, stride=2 * n_pos), :] = ws[t]
            lhs_scr[pl.ds(n_pos + t, SB, stride=2 * n_pos), :] = qs[t] * jnp.exp(Gs[t])
        for s in range(SB):
            rows = slice(s * 2 * n_pos, (s + 1) * 2 * n_pos)
            res_scr[rows, :] = _dot(lhs_scr[rows, :], s_ref[s, h])
        v_news = []
        for t in range(n_pos):
            v_news.append(us[t] - res_scr[pl.ds(t, SB, stride=2 * n_pos), :])
        G_last = Gs[n_pos - 1]
        for t in range(n_pos):
            o = res_scr[pl.ds(n_pos + t, SB, stride=2 * n_pos), :]
            for s in range(t + 1):
                qk = jnp.sum(qs[t] * ks[s], axis=-1, keepdims=True) * jnp.exp(Gs[t] - Gs[s])
                o = o + qk * v_news[s]
            o = _rms(o, gout_ref[...])
            z = proj_refs[t][:, z0 + h * B_DIM:z0 + (h + 1) * B_DIM]
            out_ref[t, :, cols] = (o * (z * jax.nn.sigmoid(z))).astype(BF16)
            kd_scr[pl.ds(t, SB, stride=2 * n_pos), :] = ks[t] * jnp.exp(G_last - Gs[t])
            vn_scr[pl.ds(t, SB, stride=2 * n_pos), :] = v_news[t]
        res_scr[0:SB, :] = jnp.broadcast_to(jnp.exp(G_last), (SB, B_DIM))
        for s in range(SB):
            rows = slice(s * 2 * n_pos, (s + 1) * 2 * n_pos)
            gl = res_scr[s:s + 1, 0:1]
            s_out_ref[s, h] = s_ref[s, h] * gl + _dot_tn(kd_scr[rows, :], vn_scr[rows, :])


def _gdn_decode(proj_s, tail_s, cbuf, s0, conv_w, alog_row, dtb_row, g_out, n_seq, n_pos):
    SB = DEC_SEQ_TILE
    hd = B_HEADS * B_DIM
    n_main = proj_s.shape[1]
    per_pos = n_seq // SB
    proj_specs = [pl.BlockSpec((SB, n_main), lambda i, t=t: (t * per_pos + i, 0)) for t in range(n_pos)]
    tail_specs = [pl.BlockSpec((SB, LANES), lambda i, t=t: (t * per_pos + i, 0)) for t in range(n_pos)]
    return pl.pallas_call(
        partial(_gdn_decode_kernel, n_pos=n_pos),
        grid=(per_pos,),
        in_specs=proj_specs + tail_specs + [
            pl.BlockSpec((SB, (CONV_W - 1) * 3 * hd), lambda i: (i, 0)),
            pl.BlockSpec((SB, B_HEADS, B_DIM, B_DIM), lambda i: (i, 0, 0, 0)),
            pl.BlockSpec((CONV_W, 3 * hd), lambda i: (0, 0)),
            pl.BlockSpec((1, LANES), lambda i: (0, 0)),
            pl.BlockSpec((1, LANES), lambda i: (0, 0)),
            pl.BlockSpec((1, B_DIM), lambda i: (0, 0)),
        ],
        out_specs=[
            pl.BlockSpec((n_pos, SB, hd), lambda i: (0, i, 0)),
            pl.BlockSpec((SB, B_HEADS, B_DIM, B_DIM), lambda i: (i, 0, 0, 0)),
            pl.BlockSpec((SB, (CONV_W - 1) * 3 * hd), lambda i: (i, 0)),
        ],
        out_shape=[
            jax.ShapeDtypeStruct((n_pos, n_seq, hd), BF16),
            jax.ShapeDtypeStruct((n_seq, B_HEADS, B_DIM, B_DIM), F32),
            jax.ShapeDtypeStruct((n_seq, (CONV_W - 1) * 3 * hd), F32),
        ],
        scratch_shapes=[
            pltpu.VMEM((SB * 2 * n_pos, B_DIM), F32),
            pltpu.VMEM((SB * 2 * n_pos, B_DIM), F32),
            pltpu.VMEM((SB * 2 * n_pos, B_DIM), F32),
            pltpu.VMEM((SB * 2 * n_pos, B_DIM), F32),
        ],
        compiler_params=_cparams(("parallel",)),
        name="gdn_decode",
    )(*([proj_s] * n_pos), *([tail_s] * n_pos), cbuf, s0, conv_w, alog_row, dtb_row, g_out)


def _outproj_kernel(x_ref, oa_ref, ob_ref, w_ref, g_ref, wr_ref, br_ref,
                    x1_ref, h2_ref, lt_ref, *, time_major):
    a_width = oa_ref.shape[1]
    acc = _dot(oa_ref[...], w_ref[0:a_width, :]) + _dot(ob_ref[...], w_ref[a_width:, :])
    if time_major:
        d = g_ref.shape[-1]
        rows = x_ref.shape[0]
        x = jnp.concatenate([x_ref[:, t * d:(t + 1) * d] for t in range(x_ref.shape[1] // d)], axis=0)
    else:
        x = x_ref[...]
    x1 = x + acc
    x1_ref[...] = x1
    h2 = _rms(x1, g_ref[...])
    h2_ref[...] = _pack_bf16_pairs(h2)
    h_hi = h2.astype(BF16)
    h_lo = (h2 - h_hi.astype(F32)).astype(BF16)
    tm = h2.shape[0]
    lg = _dot(jnp.concatenate([h_hi, h_lo], axis=0), wr_ref[...])
    lg = lg[:tm, :LANES] + lg[:tm, LANES:] + lg[tm:, :LANES] + lg[tm:, LANES:]
    lt_ref[...] = lg.T[0:ROUTER_ROWS, :] + br_ref[:, 0:1]


def _outproj(x2d, out_a, out_b, w_out, g_ffn, wr_t, br_col, *, time_major):
    d = g_ffn.shape[-1]
    a_width = out_a.shape[1]
    n_tok = out_a.shape[0]
    tm = TOKEN_TILE
    if time_major:
        x_spec = pl.BlockSpec(x2d.shape, lambda i: (0, 0))
    else:
        x_spec = pl.BlockSpec((tm, d), lambda i: (i, 0))
    return pl.pallas_call(
        partial(_outproj_kernel, time_major=time_major),
        grid=(n_tok // tm,),
        in_specs=[
            x_spec,
            pl.BlockSpec((tm, a_width), lambda i: (i, 0)),
            pl.BlockSpec((tm, out_b.shape[1]), lambda i: (i, 0)),
            pl.BlockSpec(w_out.shape, lambda i: (0, 0)),
            pl.BlockSpec((1, d), lambda i: (0, 0)),
            pl.BlockSpec(wr_t.shape, lambda i: (0, 0)),
            pl.BlockSpec(br_col.shape, lambda i: (0, 0)),
        ],
        out_specs=[
            pl.BlockSpec((tm, d), lambda i: (i, 0)),
            pl.BlockSpec((tm, d // 2), lambda i: (i, 0)),
            pl.BlockSpec((ROUTER_ROWS, tm), lambda i: (0, i)),
        ],
        out_shape=[
            jax.ShapeDtypeStruct((n_tok, d), F32),
            jax.ShapeDtypeStruct((n_tok, d // 2), U32),
            jax.ShapeDtypeStruct((ROUTER_ROWS, n_tok), F32),
        ],
        compiler_params=_cparams(("parallel",)),
        name="outproj_tm" if time_major else "outproj",
    )(x2d, out_a, out_b, w_out, g_ffn, wr_t, br_col)


def _router_kernel(lp_ref, ls_ref, dest_ref, wcol_ref, meta_ref, cnt_scr, base_scr, *, n_prompt_tiles):
    p = pl.program_id(0)
    i = pl.program_id(1)
    tm = lp_ref.shape[1]
    lt = jnp.where(i < n_prompt_tiles, lp_ref[...], ls_ref[...])

    m = lt[0:1, :]
    sel = jnp.zeros((1, tm), I32)
    for r in range(1, N_GROUPS):
        upd = lt[r:r + 1, :] > m
        sel = jnp.where(upd, r, sel)
        m = jnp.where(upd, lt[r:r + 1, :], m)
    den = jnp.zeros((1, tm), F32)
    for r in range(N_GROUPS):
        den = den + jnp.exp(lt[r:r + 1, :] - m)
    g_w = 1.0 / den

    ev = []
    for j in range(EXPERTS_PER_GROUP):
        e = jnp.zeros((1, tm), F32)
        for g in range(N_GROUPS):
            row = N_GROUPS + g * EXPERTS_PER_GROUP + j
            e = jnp.where(sel == g, lt[row:row + 1, :], e)
        ev.append(e)
    v0 = ev[0]
    i0 = jnp.zeros((1, tm), I32)
    for j in range(1, EXPERTS_PER_GROUP):
        upd = ev[j] > v0
        i0 = jnp.where(upd, j, i0)
        v0 = jnp.where(upd, ev[j], v0)
    v1 = jnp.full((1, tm), -jnp.inf, F32)
    i1 = jnp.zeros((1, tm), I32)
    for j in range(EXPERTS_PER_GROUP):
        upd = (ev[j] > v1) & (i0 != j)
        i1 = jnp.where(upd, j, i1)
        v1 = jnp.where(upd, ev[j], v1)
    t = jnp.exp(v1 - v0)
    w0 = g_w / (1.0 + t)
    w1 = g_w * t / (1.0 + t)
    e0 = sel * EXPERTS_PER_GROUP + i0
    e1 = sel * EXPERTS_PER_GROUP + i1

    eio = lax.broadcasted_iota(I32, (N_EXPERTS, tm), 0)
    hit0 = eio == e0
    hit1 = eio == e1
    onehot = (hit0 | hit1).astype(F32)
    tile_cnt = jnp.sum(onehot, axis=1, keepdims=True)

    @pl.when((p == 0) & (i == 0))
    def _():
        cnt_scr[...] = jnp.zeros_like(cnt_scr)

    @pl.when(p == 0)
    def _():
        cnt_scr[...] = cnt_scr[...] + tile_cnt

    @pl.when((p == 1) & (i == 0))
    def _():
        cnt = cnt_scr[...]
        padded = jnp.floor((cnt + (MOE_BLOCK - 1)) * (1.0 / MOE_BLOCK)) * MOE_BLOCK
        ri = lax.broadcasted_iota(I32, (N_EXPERTS, N_EXPERTS), 0)
        ci = lax.broadcasted_iota(I32, (N_EXPERTS, N_EXPERTS), 1)
        lower = (ci < ri).astype(F32)
        pad_start = jnp.dot(lower, jnp.broadcast_to(padded, (N_EXPERTS, LANES)),
                            precision=lax.Precision.HIGHEST, preferred_element_type=F32)[:, 0:1]
        base_scr[...] = pad_start
        pad_end = pad_start + padded
        blk0 = (lax.broadcasted_iota(I32, (N_EXPERTS, LANES), 1) * MOE_BLOCK).astype(F32)
        n_le = jnp.sum((pad_end <= blk0).astype(F32), axis=0, keepdims=True)
        block_e = jnp.minimum(n_le, N_EXPERTS - 1.0)
        n_active = jnp.broadcast_to(pad_end[N_EXPERTS - 1:N_EXPERTS, :] * (1.0 / MOE_BLOCK), (1, LANES))
        meta_ref[...] = jnp.concatenate(
            [block_e, n_active, jnp.zeros((6, LANES), F32)], axis=0).astype(I32)

    @pl.when(p == 1)
    def _():
        ui = lax.broadcasted_iota(I32, (tm, tm), 0)
        uj = lax.broadcasted_iota(I32, (tm, tm), 1)
        upper = (ui < uj).astype(BF16)
        excl = _dot(onehot.astype(BF16), upper)
        pos = base_scr[...] + excl
        d0 = jnp.sum(jnp.where(hit0, pos, 0.0), axis=0, keepdims=True)
        d1 = jnp.sum(jnp.where(hit1, pos, 0.0), axis=0, keepdims=True)
        dest_ref[...] = jnp.concatenate([d0, d1, jnp.zeros((6, tm), F32)], axis=0).astype(I32)
        wmat = jnp.concatenate([w0, w1, jnp.zeros((LANES - 2, tm), F32)], axis=0)
        wcol_ref[...] = wmat.T
        base_scr[...] = base_scr[...] + tile_cnt


def _router(lt_p, lt_s):
    tm = TOKEN_TILE
    n_p = lt_p.shape[1] // tm
    n_s = lt_s.shape[1] // tm
    n_tok = lt_p.shape[1] + lt_s.shape[1]
    return pl.pallas_call(
        partial(_router_kernel, n_prompt_tiles=n_p),
        grid=(2, n_p + n_s),
        in_specs=[
            pl.BlockSpec((ROUTER_ROWS, tm), lambda p, i: (0, jnp.minimum(i, n_p - 1))),
            pl.BlockSpec((ROUTER_ROWS, tm), lambda p, i: (0, jnp.maximum(i - n_p, 0))),
        ],
        out_specs=[
            pl.BlockSpec((8, tm), lambda p, i: (0, i * p)),
            pl.BlockSpec((tm, LANES), lambda p, i: (i * p, 0)),
            pl.BlockSpec((8, LANES), lambda p, i: (0, 0)),
        ],
        out_shape=[
            jax.ShapeDtypeStruct((8, n_tok), I32),
            jax.ShapeDtypeStruct((n_tok, LANES), F32),
            jax.ShapeDtypeStruct((8, LANES), I32),
        ],
        scratch_shapes=[pltpu.VMEM((N_EXPERTS, 1), F32), pltpu.VMEM((N_EXPERTS, 1), F32)],
        compiler_params=_cparams(("arbitrary", "arbitrary")),
        name="router",
    )(lt_p, lt_s)


def _dispatch_kernel(d0_ref, d1_ref, h_ref, xb_in_ref, xb_ref, sem):
    del xb_in_ref
    i = pl.program_id(0)
    tm = h_ref.shape[0]

    def copy(r, d_ref):
        return pltpu.make_async_copy(h_ref.at[pl.ds(r, 1)], xb_ref.at[pl.ds(d_ref[i * tm + r], 1)], sem)

    def issue(r, carry):
        copy(r, d0_ref).start()
        copy(r, d1_ref).start()
        return carry

    lax.fori_loop(0, tm, issue, 0, unroll=ROW_DMA_UNROLL)

    def drain(r, carry):
        copy(r, d0_ref).wait()
        copy(r, d1_ref).wait()
        return carry

    lax.fori_loop(0, tm, drain, 0, unroll=ROW_DMA_UNROLL)


def _dispatch(d0, d1, h2, xb):
    tm = TOKEN_TILE
    d = h2.shape[1]
    return pl.pallas_call(
        _dispatch_kernel,
        grid_spec=pltpu.PrefetchScalarGridSpec(
            num_scalar_prefetch=2,
            grid=(h2.shape[0] // tm,),
            in_specs=[
                pl.BlockSpec((tm, d), lambda i, *_: (i, 0)),
                pl.BlockSpec(memory_space=pl.ANY),
            ],
            out_specs=pl.BlockSpec(memory_space=pl.ANY),
            scratch_shapes=[pltpu.SemaphoreType.DMA(())],
        ),
        out_shape=jax.ShapeDtypeStruct(xb.shape, xb.dtype),
        input_output_aliases={3: 0},
        compiler_params=_cparams(("arbitrary",)),
        name="dispatch",
    )(d0, d1, h2, xb)


def _ffn_kernel(be_ref, na_ref, xb_ref, wg_ref, wu_ref, wd_ref, yb_ref):
    b = pl.program_id(0)

    @pl.when(b < na_ref[0])
    def _():
        x_lo, x_hi = _unpack_bf16_pairs(xb_ref[...])
        half = x_lo.shape[1]
        g = _dot(x_lo, wg_ref[0, 0:half, :].astype(BF16)) + _dot(x_hi, wg_ref[0, half:, :].astype(BF16))
        u = _dot(x_lo, wu_ref[0, 0:half, :].astype(BF16)) + _dot(x_hi, wu_ref[0, half:, :].astype(BF16))
        a = (g * jax.nn.sigmoid(g) * u).astype(BF16)
        yb_ref[...] = _dot(a, wd_ref[0].astype(BF16))

    @pl.when(b >= na_ref[0])
    def _():
        yb_ref[...] = jnp.zeros_like(yb_ref)


def _ffn(block_e, n_active, xb, w_gate, w_up, w_down):
    n_rows, d_packed = xb.shape
    d = w_gate.shape[1]
    d_e = w_gate.shape[-1]
    n_blocks = n_rows // MOE_BLOCK

    def live(b, na):
        return jnp.minimum(b, jnp.maximum(na[0] - 1, 0))

    return pl.pallas_call(
        _ffn_kernel,
        grid_spec=pltpu.PrefetchScalarGridSpec(
            num_scalar_prefetch=2,
            grid=(n_blocks,),
            in_specs=[
                pl.BlockSpec((MOE_BLOCK, d_packed), lambda b, be, na: (live(b, na), 0)),
                pl.BlockSpec((1, d, d_e), lambda b, be, na: (be[live(b, na)], 0, 0)),
                pl.BlockSpec((1, d, d_e), lambda b, be, na: (be[live(b, na)], 0, 0)),
                pl.BlockSpec((1, d_e, d), lambda b, be, na: (be[live(b, na)], 0, 0)),
            ],
            out_specs=pl.BlockSpec((MOE_BLOCK, d), lambda b, be, na: (b, 0)),
        ),
        out_shape=jax.ShapeDtypeStruct((n_rows, d), F32),
        compiler_params=_cparams(("arbitrary",)),
        name="moe_ffn",
    )(block_e, n_active, xb, w_gate, w_up, w_down)


def _ple_kernel(d0_ref, d1_ref, x1_ref, wcol_ref, p_ref, yb_ref, gple_ref, wpg_ref, wp_ref, gfin_ref,
                y_ref, g0_scr, g1_scr, sem, *, time_major):
    i = pl.program_id(0)
    tm = x1_ref.shape[0]

    def copy(r, d_ref, dst):
        return pltpu.make_async_copy(yb_ref.at[pl.ds(d_ref[i * tm + r], 1)], dst.at[pl.ds(r, 1)], sem)

    def issue(r, carry):
        copy(r, d0_ref, g0_scr).start()
        copy(r, d1_ref, g1_scr).start()
        return carry

    lax.fori_loop(0, tm, issue, 0, unroll=ROW_DMA_UNROLL)

    def drain(r, carry):
        copy(r, d0_ref, g0_scr).wait()
        copy(r, d1_ref, g1_scr).wait()
        return carry

    lax.fori_loop(0, tm, drain, 0, unroll=ROW_DMA_UNROLL)

    x2 = x1_ref[...] + wcol_ref[:, 0:1] * g0_scr[...] + wcol_ref[:, 1:2] * g1_scr[...]
    hn = _rms(x2, gple_ref[...]).astype(BF16)
    gate = jax.nn.sigmoid(_dot(hn, wpg_ref[...]))
    if time_major:
        pd = wp_ref.shape[0]
        pp = jnp.concatenate([p_ref[:, t * pd:(t + 1) * pd] for t in range(p_ref.shape[1] // pd)], axis=0)
    else:
        pp = p_ref[...]
    x3 = x2 + _dot(pp.astype(BF16), wp_ref[...]) * gate
    y = _rms(x3, gfin_ref[...])
    if time_major:
        d = y.shape[1]
        rows = y_ref.shape[0]
        for t in range(y_ref.shape[1] // d):
            y_ref[:, t * d:(t + 1) * d] = y[t * rows:(t + 1) * rows, :]
    else:
        y_ref[...] = y


def _ple(d0, d1, x1, wcol, p2d, yb, g_ple, w_pg, w_p, g_final, *, time_major):
    n_tok, d = x1.shape
    tm = TOKEN_TILE
    pd = w_p.shape[0]
    if time_major:
        n_seq = p2d.shape[0]
        p_spec = pl.BlockSpec(p2d.shape, lambda i, *_: (0, 0))
        y_spec = pl.BlockSpec((n_seq, (n_tok // n_seq) * d), lambda i, *_: (0, 0))
        y_shape = jax.ShapeDtypeStruct((n_seq, (n_tok // n_seq) * d), F32)
    else:
        p_spec = pl.BlockSpec((tm, pd), lambda i, *_: (i, 0))
        y_spec = pl.BlockSpec((tm, d), lambda i, *_: (i, 0))
        y_shape = jax.ShapeDtypeStruct((n_tok, d), F32)
    return pl.pallas_call(
        partial(_ple_kernel, time_major=time_major),
        grid_spec=pltpu.PrefetchScalarGridSpec(
            num_scalar_prefetch=2,
            grid=(n_tok // tm,),
            in_specs=[
                pl.BlockSpec((tm, d), lambda i, *_: (i, 0)),
                pl.BlockSpec((tm, LANES), lambda i, *_: (i, 0)),
                p_spec,
                pl.BlockSpec(memory_space=pl.ANY),
                pl.BlockSpec((1, d), lambda i, *_: (0, 0)),
                pl.BlockSpec(w_pg.shape, lambda i, *_: (0, 0)),
                pl.BlockSpec(w_p.shape, lambda i, *_: (0, 0)),
                pl.BlockSpec((1, d), lambda i, *_: (0, 0)),
            ],
            out_specs=y_spec,
            scratch_shapes=[
                pltpu.VMEM((tm, d), F32),
                pltpu.VMEM((tm, d), F32),
                pltpu.SemaphoreType.DMA(()),
            ],
        ),
        out_shape=y_shape,
        compiler_params=_cparams(("arbitrary",)),
        name="ple_tm" if time_major else "ple",
    )(d0, d1, x1, wcol, p2d, yb, g_ple, w_pg, w_p, g_final)


def _pad_lanes(v, offset=0):
    row = jnp.zeros((1, LANES), F32)
    return row.at[0, offset:offset + v.shape[0]].set(v.astype(F32))


def kernel(x_prompt, x_sample, state_conv, state_delta, p_prompt, p_sample, g_mix, w_in, w_s, b_s, g_v,
           conv_w, a_log, dt_bias, g_out, w_out, g_ffn, w_group, b_group, w_router, b_router, w_gate, w_up,
           w_down, g_ple, w_ple_gate, w_ple, g_final):
    n_batch, seq, d = x_prompt.shape
    n_seq, n_pos, _ = x_sample.shape
    depth = g_mix.shape[0]
    assert depth == 1
    a_width = A_GROUPS * A_GROUP_DIM
    hd = B_HEADS * B_DIM
    n_main = 2 * a_width + 4 * hd
    n_p = n_batch * seq
    n_s = n_seq * n_pos

    xp = x_prompt.reshape(n_p, d)
    xs = x_sample.reshape(n_seq, n_pos * d)
    l = 0

    w_main = w_in[l].astype(BF16)
    w_tail = jnp.pad(w_in[l][:, n_main:], ((0, 0), (0, LANES - 2 * B_HEADS))).astype(BF16)
    gm = g_mix[l].reshape(1, d)
    causal = jnp.tril(jnp.ones((A_CHUNK, A_CHUNK), dtype=bool))
    ws_tril = jnp.where(causal, w_s[l], 0).astype(BF16)
    b_t = b_s[l].T
    gv = g_v[l].reshape(1, a_width)
    alog_row = _pad_lanes(a_log[l])
    dtb_row = _pad_lanes(dt_bias[l])
    gout = g_out[l].reshape(1, B_DIM)
    w_o = w_out[l].astype(BF16)
    gf = g_ffn[l].reshape(1, d)
    wr = jnp.pad(jnp.concatenate([w_group[l], w_router[l]], axis=1),
                 ((0, 0), (0, LANES - N_GROUPS - N_EXPERTS)))
    wr_hi = wr.astype(BF16)
    wr_t = jnp.concatenate([wr_hi, (wr - wr_hi.astype(F32)).astype(BF16)], axis=1)
    br_col = jnp.broadcast_to(
        jnp.pad(jnp.concatenate([b_group[l], b_router[l]]), (0, ROUTER_ROWS - N_GROUPS - N_EXPERTS))[:, None],
        (ROUTER_ROWS, LANES))
    gp = g_ple[l].reshape(1, d)
    w_pg = w_ple_gate[l].astype(BF16)
    w_p = w_ple[l].astype(BF16)
    gfin = g_final.reshape(1, d)

    proj_p, tail_p = _inproj(xp, gm, w_main, w_tail, n_main, time_major=False)
    proj_s, tail_s = _inproj(xs, gm, w_main, w_tail, n_main, time_major=True)
    oa_p = _mixa_prompt(proj_p, ws_tril, b_t, gv, n_p)
    oa_s, va_s = _mixa_sample(proj_s, w_s[l], b_s[l], gv, n_seq, n_pos)
    ob_p, sd_p, cb_p = _gdn_prompt(proj_p, tail_p, conv_w[l], alog_row, dtb_row, gout, n_batch, seq)
    cbuf = state_conv[l].reshape(n_seq, (CONV_W - 1) * 3 * hd)
    ob_s, sd_s, cb_s = _gdn_decode(proj_s, tail_s, cbuf, state_delta[l], conv_w[l], alog_row, dtb_row,
                                   gout, n_seq, n_pos)
    ob_s = ob_s.reshape(n_s, hd)

    x1_p, h2_p, lt_p = _outproj(xp, oa_p, ob_p, w_o, gf, wr_t, br_col, time_major=False)
    x1_s, h2_s, lt_s = _outproj(xs, oa_s, ob_s, w_o, gf, wr_t, br_col, time_major=True)
    dest, wcol, meta = _router(lt_p, lt_s)
    d0, d1 = dest[0], dest[1]
    block_e = meta[0]
    n_active = meta[1, 0:1]

    n_tok = n_p + n_s
    n_blocks = (n_tok * 2) // MOE_BLOCK + N_EXPERTS
    xb = jnp.zeros((n_blocks * MOE_BLOCK, d // 2), U32)
    xb = _dispatch(d0[:n_p], d1[:n_p], h2_p, xb)
    xb = _dispatch(d0[n_p:], d1[n_p:], h2_s, xb)
    yb = _ffn(block_e[:n_blocks], n_active, xb, w_gate[l], w_up[l], w_down[l])

    pp = p_prompt[l].reshape(n_p, -1)
    ps = p_sample[l].reshape(n_seq, -1)
    y_p = _ple(d0[:n_p], d1[:n_p], x1_p, wcol[:n_p], pp, yb, gp, w_pg, w_p, gfin, time_major=False)
    y_s = _ple(d0[n_p:], d1[n_p:], x1_s, wcol[n_p:], ps, yb, gp, w_pg, w_p, gfin, time_major=True)

    return (
        y_p.reshape(n_batch, seq, d),
        y_s.reshape(n_seq, n_pos, d),
        cb_p[None],
        sd_p[None],
        cb_s.reshape(n_seq, CONV_W - 1, 3 * hd)[None],
        sd_s[None],
        va_s.reshape(n_seq, n_pos, a_width)[None],
    )
```

```python
from functools import partial

import jax
import jax.numpy as jnp
from jax import lax
from jax.experimental import pallas as pl
from jax.experimental.pallas import tpu as pltpu

F32 = jnp.float32
BF16 = jnp.bfloat16
I32 = jnp.int32
U32 = jnp.uint32

EPS = 1e-6
LANES = 128
A_GROUPS = 8
A_GROUP_DIM = 128
A_CHUNK = 128
B_HEADS = 8
B_DIM = 128
CONV_W = 4
N_GROUPS = 4
EXPERTS_PER_GROUP = 8
N_EXPERTS = N_GROUPS * EXPERTS_PER_GROUP
ROUTER_ROWS = 40
TOKEN_TILE = 512
INPROJ_TILE = 1024
PLE_TILE = 256
GDN_CHUNK = 64
MOE_BLOCK = 256
DEC_SEQ_TILE = 8
ROW_DMA_UNROLL = 8
VMEM_LIMIT = 56 * 1024 * 1024


def _cparams(sem):
    return pltpu.CompilerParams(dimension_semantics=sem, vmem_limit_bytes=VMEM_LIMIT)


def _rms(x, g):
    return x * lax.rsqrt(jnp.mean(x * x, axis=-1, keepdims=True) + EPS) * g


def _softplus(x):
    return jnp.maximum(x, 0.0) + jnp.log(1.0 + jnp.exp(-jnp.abs(x)))


def _dot(a, b):
    return jnp.dot(a, b, preferred_element_type=F32)


def _dot_nt(a, b, precision=None):
    return lax.dot_general(a, b, (((1,), (1,)), ((), ())), precision=precision,
                           preferred_element_type=F32)


def _dot_tn(a, b):
    return lax.dot_general(a, b, (((0,), (0,)), ((), ())), preferred_element_type=F32)


def _pack_bf16_pairs(x):
    n = x.shape[1] // 2
    lo = lax.bitcast_convert_type(x[:, :n].astype(BF16).astype(F32), U32) >> 16
    hi = lax.bitcast_convert_type(x[:, n:].astype(BF16).astype(F32), U32) & jnp.uint32(0xFFFF0000)
    return hi | lo


def _unpack_bf16_pairs(xu):
    lo = lax.bitcast_convert_type(xu << 16, F32).astype(BF16)
    hi = lax.bitcast_convert_type(xu & jnp.uint32(0xFFFF0000), F32).astype(BF16)
    return lo, hi


def _cast_kernel(w_ref, o_ref):
    o_ref[...] = w_ref[0].astype(BF16)


def _cast_bf16(w3d, n_cols, col_tile=1024):
    rows = w3d.shape[1]
    return pl.pallas_call(
        _cast_kernel,
        grid=(n_cols // col_tile,),
        in_specs=[pl.BlockSpec((1, rows, col_tile), lambda j: (0, 0, j))],
        out_specs=pl.BlockSpec((rows, col_tile), lambda j: (0, j)),
        out_shape=jax.ShapeDtypeStruct((rows, n_cols), BF16),
        compiler_params=_cparams(("parallel",)),
        name="cast_bf16",
    )(w3d)


def _inproj_kernel(x_ref, g_ref, w_ref, wt_ref, proj_ref, tail_ref, h_scr, *, time_major):
    j = pl.program_id(1)

    @pl.when(j == 0)
    def _():
        g = g_ref[...]
        if time_major:
            d = g.shape[-1]
            rows = x_ref.shape[0]
            for t in range(x_ref.shape[1] // d):
                h_scr[t * rows:(t + 1) * rows, :] = _rms(x_ref[:, t * d:(t + 1) * d], g).astype(BF16)
        else:
            h_scr[...] = _rms(x_ref[...], g).astype(BF16)
        tail_ref[...] = _dot(h_scr[...], wt_ref[...])

    proj_ref[...] = _dot(h_scr[...], w_ref[...])


def _inproj(x2d, g_mix, w_main, w_tail, n_main, *, time_major, col_tile=1024):
    d = g_mix.shape[-1]
    if time_major:
        n_tok = x2d.shape[0] * (x2d.shape[1] // d)
        tm = n_tok
        x_spec = pl.BlockSpec(x2d.shape, lambda i, j: (0, 0))
    else:
        n_tok = x2d.shape[0]
        tm = min(INPROJ_TILE, n_tok)
        x_spec = pl.BlockSpec((tm, d), lambda i, j: (i, 0))
    grid = (n_tok // tm, n_main // col_tile)
    return pl.pallas_call(
        partial(_inproj_kernel, time_major=time_major),
        grid=grid,
        in_specs=[
            x_spec,
            pl.BlockSpec((1, d), lambda i, j: (0, 0)),
            pl.BlockSpec((d, col_tile), lambda i, j: (0, j)),
            pl.BlockSpec((d, LANES), lambda i, j: (0, 0)),
        ],
        out_specs=[
            pl.BlockSpec((tm, col_tile), lambda i, j: (i, j)),
            pl.BlockSpec((tm, LANES), lambda i, j: (i, 0)),
        ],
        out_shape=[
            jax.ShapeDtypeStruct((n_tok, n_main), F32),
            jax.ShapeDtypeStruct((n_tok, LANES), F32),
        ],
        scratch_shapes=[pltpu.VMEM((tm, d), BF16)],
        compiler_params=_cparams(("parallel", "arbitrary")),
        name="inproj_tm" if time_major else "inproj",
    )(x2d, g_mix, w_main, w_tail)


def _group_ln(v, g):
    mu = jnp.mean(v, axis=-1, keepdims=True)
    dlt = v - mu
    var = jnp.mean(dlt * dlt, axis=-1, keepdims=True)
    return dlt * lax.rsqrt(var + EPS) * g


def _mixa_prompt_kernel(au_ref, av_ref, ws_ref, bt_ref, gv_ref, out_ref):
    n_chunks = au_ref.shape[0] // A_CHUNK
    for c in range(n_chunks):
        rows = slice(c * A_CHUNK, (c + 1) * A_CHUNK)
        for h in range(A_GROUPS):
            cols = slice(h * A_GROUP_DIM, (h + 1) * A_GROUP_DIM)
            u = jax.nn.gelu(au_ref[rows, cols])
            v = _group_ln(jax.nn.gelu(av_ref[rows, cols]), gv_ref[:, cols])
            mixed = _dot(ws_ref[h], v.astype(BF16)) + bt_ref[:, h:h + 1]
            out_ref[rows, cols] = (u * mixed).astype(BF16)


def _mixa_prompt(proj, ws_tril, b_t, g_v, n_tok):
    a_width = A_GROUPS * A_GROUP_DIM
    tm = TOKEN_TILE
    return pl.pallas_call(
        _mixa_prompt_kernel,
        grid=(n_tok // tm,),
        in_specs=[
            pl.BlockSpec((tm, a_width), lambda i: (i, 0)),
            pl.BlockSpec((tm, a_width), lambda i: (i, 1)),
            pl.BlockSpec((A_GROUPS, A_CHUNK, A_CHUNK), lambda i: (0, 0, 0)),
            pl.BlockSpec((A_CHUNK, A_GROUPS), lambda i: (0, 0)),
            pl.BlockSpec((1, a_width), lambda i: (0, 0)),
        ],
        out_specs=pl.BlockSpec((tm, a_width), lambda i: (i, 0)),
        out_shape=jax.ShapeDtypeStruct((n_tok, a_width), BF16),
        compiler_params=_cparams(("parallel",)),
        name="mixa_prompt",
    )(proj, proj, ws_tril, b_t, g_v)


def _mixa_sample_kernel(ws_ref, bs_ref, au_ref, av_ref, gv_ref, out_ref, va_ref, *, n_seq, n_pos):
    a_width = A_GROUPS * A_GROUP_DIM
    vs = []
    for t in range(n_pos):
        rows = slice(t * n_seq, (t + 1) * n_seq)
        per_group = []
        for h in range(A_GROUPS):
            cols = slice(h * A_GROUP_DIM, (h + 1) * A_GROUP_DIM)
            per_group.append(_group_ln(jax.nn.gelu(av_ref[rows, cols]), gv_ref[:, cols]))
        vs.append(per_group)
        for h in range(A_GROUPS):
            va_ref[:, t * a_width + h * A_GROUP_DIM:t * a_width + (h + 1) * A_GROUP_DIM] = per_group[h]
    for t in range(n_pos):
        rows = slice(t * n_seq, (t + 1) * n_seq)
        for h in range(A_GROUPS):
            cols = slice(h * A_GROUP_DIM, (h + 1) * A_GROUP_DIM)
            mixed = jnp.zeros((n_seq, A_GROUP_DIM), F32) + bs_ref[h * n_pos + t]
            for s in range(t + 1):
                mixed = mixed + ws_ref[(h * n_pos + t) * n_pos + s] * vs[s][h]
            out_ref[rows, cols] = (jax.nn.gelu(au_ref[rows, cols]) * mixed).astype(BF16)


def _mixa_sample(proj, w_s, b_s, g_v, n_seq, n_pos):
    a_width = A_GROUPS * A_GROUP_DIM
    n_tok = n_seq * n_pos
    return pl.pallas_call(
        partial(_mixa_sample_kernel, n_seq=n_seq, n_pos=n_pos),
        grid_spec=pltpu.PrefetchScalarGridSpec(
            num_scalar_prefetch=2,
            grid=(1,),
            in_specs=[
                pl.BlockSpec((n_tok, a_width), lambda i, *_: (0, 0)),
                pl.BlockSpec((n_tok, a_width), lambda i, *_: (0, 1)),
                pl.BlockSpec((1, a_width), lambda i, *_: (0, 0)),
            ],
            out_specs=[
                pl.BlockSpec((n_tok, a_width), lambda i, *_: (0, 0)),
                pl.BlockSpec((n_seq, n_pos * a_width), lambda i, *_: (0, 0)),
            ],
        ),
        out_shape=[
            jax.ShapeDtypeStruct((n_tok, a_width), BF16),
            jax.ShapeDtypeStruct((n_seq, n_pos * a_width), F32),
        ],
        compiler_params=_cparams(("arbitrary",)),
        name="mixa_sample",
    )(w_s[:, :n_pos, :n_pos].reshape(-1), b_s[:, :n_pos].reshape(-1), proj, proj, g_v)


def _gdn_prompt_kernel(q_ref, k_ref, v_ref, z_ref, tail_ref, cw_ref, alog_ref, dtb_ref, gout_ref,
                       out_ref, s_out_ref, conv_out_ref, s_scr, xbuf):
    c = pl.program_id(1)
    n_c = pl.num_programs(1)
    C = q_ref.shape[0]
    hd = B_HEADS * B_DIM
    pad = 8
    lo = pad - (CONV_W - 1)

    @pl.when(c == 0)
    def _():
        s_scr[...] = jnp.zeros_like(s_scr)
        xbuf[0:pad, :] = jnp.zeros((pad, 3 * hd), F32)

    xbuf[pad:pad + C, 0:hd] = q_ref[...]
    xbuf[pad:pad + C, hd:2 * hd] = k_ref[...]
    xbuf[pad:pad + C, 2 * hd:3 * hd] = v_ref[...]
    y = xbuf[lo:lo + C, :] * cw_ref[0:1, :]
    for j in range(1, CONV_W):
        y = y + xbuf[lo + j:lo + j + C, :] * cw_ref[j:j + 1, :]
    y = y * jax.nn.sigmoid(y)
    last = xbuf[pad + C - (CONV_W - 1):pad + C, :]
    xbuf[lo:pad, :] = last

    @pl.when(c == n_c - 1)
    def _():
        conv_out_ref[0] = last

    tail = tail_ref[...]
    g_all = -jnp.exp(alog_ref[...]) * _softplus(tail + dtb_ref[...])
    beta_all = jax.nn.sigmoid(tail)
    ri = lax.broadcasted_iota(I32, (C, C), 0)
    ci = lax.broadcasted_iota(I32, (C, C), 1)
    causal = ci <= ri
    strict = ci < ri
    tril = causal.astype(F32)
    G_all = jnp.dot(tril, g_all, precision=lax.Precision.HIGHEST, preferred_element_type=F32)
    G_rows = jnp.concatenate([G_all, jnp.zeros((LANES - C, LANES), F32)], axis=0).T if C < LANES else G_all.T
    scale = B_DIM ** -0.5
    n_levels = max(C.bit_length() - 2, 0)
    heads = range(B_HEADS)

    qs, ks, Gs, eGs, rhss, qks, Ps = [], [], [], [], [], [], []
    for h in heads:
        q = y[:, h * B_DIM:(h + 1) * B_DIM]
        k = y[:, hd + h * B_DIM:hd + (h + 1) * B_DIM]
        v = y[:, 2 * hd + h * B_DIM:2 * hd + (h + 1) * B_DIM]
        q = q * lax.rsqrt(jnp.sum(q * q, axis=-1, keepdims=True) + EPS) * scale
        k = k * lax.rsqrt(jnp.sum(k * k, axis=-1, keepdims=True) + EPS)
        G = G_all[:, h:h + 1]
        beta = beta_all[:, B_HEADS + h:B_HEADS + h + 1]
        eG = jnp.exp(G)
        decay = jnp.exp(jnp.where(causal, G - G_rows[h:h + 1, 0:C], -jnp.inf))
        qk_kk = _dot_nt(jnp.concatenate([q, k], axis=0), k)
        qks.append(qk_kk[:C] * decay)
        Ps.append(jnp.where(strict, -(beta * qk_kk[C:] * decay), 0.0))
        rhss.append(jnp.concatenate([v * beta, k * (beta * eG)], axis=1))
        qs.append(q)
        ks.append(k)
        Gs.append(G)
        eGs.append(eG)

    Ls = list(Ps)
    if n_levels > 0:
        Ps = [_dot(P, P) for P in Ps]
    for lvl in range(1, n_levels + 1):
        for h in heads:
            if lvl < n_levels:
                R = _dot(jnp.concatenate([Ps[h], Ls[h]], axis=0), Ps[h])
                Ls[h] = Ls[h] + Ps[h] + R[C:]
                Ps[h] = R[:C]
            else:
                Ls[h] = Ls[h] + Ps[h] + _dot(Ls[h], Ps[h])

    uws = [rhss[h] + _dot(Ls[h], rhss[h]) for h in heads]
    Ss = [s_scr[h] for h in heads]
    rs = [_dot(jnp.concatenate([uws[h][:, B_DIM:], qs[h] * eGs[h]], axis=0), Ss[h]) for h in heads]
    v_news = [uws[h][:, :B_DIM] - rs[h][:C] for h in heads]
    os_ = [rs[h][C:] + _dot(qks[h], v_news[h]) for h in heads]
    for h in heads:
        G_last = Gs[h][C - 1:C, :]
        kd = ks[h] * jnp.exp(G_last - Gs[h])
        s_scr[h] = Ss[h] * jnp.exp(G_last) + _dot_tn(kd, v_news[h])
    for h in heads:
        cols = slice(h * B_DIM, (h + 1) * B_DIM)
        o = _rms(os_[h], gout_ref[...])
        z = z_ref[:, cols]
        out_ref[:, cols] = (o * (z * jax.nn.sigmoid(z))).astype(BF16)

    @pl.when(c == n_c - 1)
    def _():
        s_out_ref[0] = s_scr[...]


def _gdn_prompt(proj, tail, conv_w, alog_row, dtb_row, g_out, n_batch, seq):
    C = GDN_CHUNK
    n_c = seq // C
    hd = B_HEADS * B_DIM
    row = lambda b, c: b * n_c + c
    return pl.pallas_call(
        _gdn_prompt_kernel,
        grid=(n_batch, n_c),
        in_specs=[
            pl.BlockSpec((C, hd), lambda b, c: (row(b, c), 2)),
            pl.BlockSpec((C, hd), lambda b, c: (row(b, c), 3)),
            pl.BlockSpec((C, hd), lambda b, c: (row(b, c), 4)),
            pl.BlockSpec((C, hd), lambda b, c: (row(b, c), 5)),
            pl.BlockSpec((C, LANES), lambda b, c: (row(b, c), 0)),
            pl.BlockSpec((CONV_W, 3 * hd), lambda b, c: (0, 0)),
            pl.BlockSpec((1, LANES), lambda b, c: (0, 0)),
            pl.BlockSpec((1, LANES), lambda b, c: (0, 0)),
            pl.BlockSpec((1, B_DIM), lambda b, c: (0, 0)),
        ],
        out_specs=[
            pl.BlockSpec((C, hd), lambda b, c: (row(b, c), 0)),
            pl.BlockSpec((1, B_HEADS, B_DIM, B_DIM), lambda b, c: (b, 0, 0, 0)),
            pl.BlockSpec((1, CONV_W - 1, 3 * hd), lambda b, c: (b, 0, 0)),
        ],
        out_shape=[
            jax.ShapeDtypeStruct((n_batch * seq, hd), BF16),
            jax.ShapeDtypeStruct((n_batch, B_HEADS, B_DIM, B_DIM), F32),
            jax.ShapeDtypeStruct((n_batch, CONV_W - 1, 3 * hd), F32),
        ],
        scratch_shapes=[
            pltpu.VMEM((B_HEADS, B_DIM, B_DIM), F32),
            pltpu.VMEM((8 + C, 3 * hd), F32),
        ],
        compiler_params=_cparams(("parallel", "arbitrary")),
        name="gdn_prompt",
    )(proj, proj, proj, proj, tail, conv_w, alog_row, dtb_row, g_out)


def _gdn_decode_kernel(*refs, n_pos):
    proj_refs = refs[:n_pos]
    tail_refs = refs[n_pos:2 * n_pos]
    (cbuf_ref, s_ref, cw_ref, alog_ref, dtb_ref, gout_ref,
     out_ref, s_out_ref, conv_out_ref, lhs_scr, res_scr, kd_scr, vn_scr, gl_scr) = refs[2 * n_pos:]
    SB = s_ref.shape[0]
    hd = B_HEADS * B_DIM
    qkv0 = 2 * hd
    z0 = qkv0 + 3 * hd
    scale = B_DIM ** -0.5

    xp = [cbuf_ref[:, j * 3 * hd:(j + 1) * 3 * hd] for j in range(CONV_W - 1)]
    xp += [proj_refs[t][:, qkv0:qkv0 + 3 * hd] for t in range(n_pos)]
    for j in range(CONV_W - 1):
        conv_out_ref[:, j * 3 * hd:(j + 1) * 3 * hd] = xp[n_pos + j]
    ys = []
    for t in range(n_pos):
        y = xp[t] * cw_ref[0:1, :]
        for j in range(1, CONV_W):
            y = y + xp[t + j] * cw_ref[j:j + 1, :]
        ys.append(y * jax.nn.sigmoid(y))
    gs, betas = [], []
    for t in range(n_pos):
        tail = tail_refs[t][...]
        gs.append(-jnp.exp(alog_ref[...]) * _softplus(tail + dtb_ref[...]))
        betas.append(jax.nn.sigmoid(tail))

    kd_scr[...] = jnp.zeros_like(kd_scr)
    vn_scr[...] = jnp.zeros_like(vn_scr)
    heads = range(B_HEADS)
    stride = 2 * n_pos

    qs, ks, Gs, us = [], [], [], []
    for h in heads:
        q_h, k_h, v_h, G_h, b_h = [], [], [], [], []
        G = None
        for t in range(n_pos):
            q = ys[t][:, h * B_DIM:(h + 1) * B_DIM]
            k = ys[t][:, hd + h * B_DIM:hd + (h + 1) * B_DIM]
            q_h.append(q * lax.rsqrt(jnp.sum(q * q, axis=-1, keepdims=True) + EPS) * scale)
            k_h.append(k * lax.rsqrt(jnp.sum(k * k, axis=-1, keepdims=True) + EPS))
            v_h.append(ys[t][:, 2 * hd + h * B_DIM:2 * hd + (h + 1) * B_DIM])
            g = gs[t][:, h:h + 1]
            G = g if G is None else G + g
            G_h.append(G)
            b_h.append(betas[t][:, B_HEADS + h:B_HEADS + h + 1])
        u_h, w_h = [], []
        for t in range(n_pos):
            u = v_h[t] * b_h[t]
            w = k_h[t] * (b_h[t] * jnp.exp(G_h[t]))
            for s in range(t):
                a = b_h[t] * jnp.sum(k_h[t] * k_h[s], axis=-1, keepdims=True) * jnp.exp(G_h[t] - G_h[s])
                u = u - a * u_h[s]
                w = w - a * w_h[s]
            u_h.append(u)
            w_h.append(w)
        for t in range(n_pos):
            lhs_scr[h, pl.ds(t, SB, stride=stride), :] = w_h[t]
            lhs_scr[h, pl.ds(n_pos + t, SB, stride=stride), :] = q_h[t] * jnp.exp(G_h[t])
        qs.append(q_h)
        ks.append(k_h)
        Gs.append(G_h)
        us.append(u_h)

    for h in heads:
        for s in range(SB):
            rows = slice(s * stride, (s + 1) * stride)
            res_scr[h, rows, :] = _dot(lhs_scr[h, rows, :], s_ref[s, h])

    for h in heads:
        cols = slice(h * B_DIM, (h + 1) * B_DIM)
        v_news = [us[h][t] - res_scr[h, pl.ds(t, SB, stride=stride), :] for t in range(n_pos)]
        G_last = Gs[h][n_pos - 1]
        for t in range(n_pos):
            o = res_scr[h, pl.ds(n_pos + t, SB, stride=stride), :]
            for s in range(t + 1):
                qk = jnp.sum(qs[h][t] * ks[h][s], axis=-1, keepdims=True) * jnp.exp(Gs[h][t] - Gs[h][s])
                o = o + qk * v_news[s]
            o = _rms(o, gout_ref[...])
            z = proj_refs[t][:, z0 + h * B_DIM:z0 + (h + 1) * B_DIM]
            out_ref[t, :, cols] = (o * (z * jax.nn.sigmoid(z))).astype(BF16)
            kd_scr[h, pl.ds(t, SB, stride=stride), :] = ks[h][t] * jnp.exp(G_last - Gs[h][t])
            vn_scr[h, pl.ds(t, SB, stride=stride), :] = v_news[t]
        gl_scr[h] = jnp.broadcast_to(jnp.exp(G_last), (SB, B_DIM))

    for h in heads:
        for s in range(SB):
            rows = slice(s * stride, (s + 1) * stride)
            s_out_ref[s, h] = (s_ref[s, h] * gl_scr[h, s:s + 1, 0:1]
                               + _dot_tn(kd_scr[h, rows, :], vn_scr[h, rows, :]))


def _gdn_decode(proj_s, tail_s, cbuf, s0, conv_w, alog_row, dtb_row, g_out, n_seq, n_pos):
    SB = DEC_SEQ_TILE
    hd = B_HEADS * B_DIM
    n_main = proj_s.shape[1]
    per_pos = n_seq // SB
    proj_specs = [pl.BlockSpec((SB, n_main), lambda i, t=t: (t * per_pos + i, 0)) for t in range(n_pos)]
    tail_specs = [pl.BlockSpec((SB, LANES), lambda i, t=t: (t * per_pos + i, 0)) for t in range(n_pos)]
    return pl.pallas_call(
        partial(_gdn_decode_kernel, n_pos=n_pos),
        grid=(per_pos,),
        in_specs=proj_specs + tail_specs + [
            pl.BlockSpec((SB, (CONV_W - 1) * 3 * hd), lambda i: (i, 0)),
            pl.BlockSpec((SB, B_HEADS, B_DIM, B_DIM), lambda i: (i, 0, 0, 0)),
            pl.BlockSpec((CONV_W, 3 * hd), lambda i: (0, 0)),
            pl.BlockSpec((1, LANES), lambda i: (0, 0)),
            pl.BlockSpec((1, LANES), lambda i: (0, 0)),
            pl.BlockSpec((1, B_DIM), lambda i: (0, 0)),
        ],
        out_specs=[
            pl.BlockSpec((n_pos, SB, hd), lambda i: (0, i, 0)),
            pl.BlockSpec((SB, B_HEADS, B_DIM, B_DIM), lambda i: (i, 0, 0, 0)),
            pl.BlockSpec((SB, (CONV_W - 1) * 3 * hd), lambda i: (i, 0)),
        ],
        out_shape=[
            jax.ShapeDtypeStruct((n_pos, n_seq, hd), BF16),
            jax.ShapeDtypeStruct((n_seq, B_HEADS, B_DIM, B_DIM), F32),
            jax.ShapeDtypeStruct((n_seq, (CONV_W - 1) * 3 * hd), F32),
        ],
        scratch_shapes=[
            pltpu.VMEM((B_HEADS, SB * 2 * n_pos, B_DIM), F32),
            pltpu.VMEM((B_HEADS, SB * 2 * n_pos, B_DIM), F32),
            pltpu.VMEM((B_HEADS, SB * 2 * n_pos, B_DIM), F32),
            pltpu.VMEM((B_HEADS, SB * 2 * n_pos, B_DIM), F32),
            pltpu.VMEM((B_HEADS, SB, B_DIM), F32),
        ],
        compiler_params=_cparams(("parallel",)),
        name="gdn_decode",
    )(*([proj_s] * n_pos), *([tail_s] * n_pos), cbuf, s0, conv_w, alog_row, dtb_row, g_out)


def _outproj_kernel(x_ref, oa_ref, ob_ref, w_ref, g_ref, wr_ref, br_ref,
                    x1_ref, h2_ref, lt_ref, *, time_major):
    a_width = oa_ref.shape[1]
    acc = _dot(oa_ref[...], w_ref[0:a_width, :]) + _dot(ob_ref[...], w_ref[a_width:, :])
    if time_major:
        d = g_ref.shape[-1]
        rows = x_ref.shape[0]
        x = jnp.concatenate([x_ref[:, t * d:(t + 1) * d] for t in range(x_ref.shape[1] // d)], axis=0)
    else:
        x = x_ref[...]
    x1 = x + acc
    x1_ref[...] = x1
    h2 = _rms(x1, g_ref[...])
    h2_ref[...] = _pack_bf16_pairs(h2)
    h_hi = h2.astype(BF16)
    h_lo = (h2 - h_hi.astype(F32)).astype(BF16)
    tm = h2.shape[0]
    lg = _dot(jnp.concatenate([h_hi, h_lo], axis=0), wr_ref[...])
    lg = lg[:tm, :LANES] + lg[:tm, LANES:] + lg[tm:, :LANES] + lg[tm:, LANES:]
    lt_ref[...] = lg.T[0:ROUTER_ROWS, :] + br_ref[:, 0:1]


def _outproj(x2d, out_a, out_b, w_out, g_ffn, wr_t, br_col, *, time_major):
    d = g_ffn.shape[-1]
    a_width = out_a.shape[1]
    n_tok = out_a.shape[0]
    tm = TOKEN_TILE
    if time_major:
        x_spec = pl.BlockSpec(x2d.shape, lambda i: (0, 0))
    else:
        x_spec = pl.BlockSpec((tm, d), lambda i: (i, 0))
    return pl.pallas_call(
        partial(_outproj_kernel, time_major=time_major),
        grid=(n_tok // tm,),
        in_specs=[
            x_spec,
            pl.BlockSpec((tm, a_width), lambda i: (i, 0)),
            pl.BlockSpec((tm, out_b.shape[1]), lambda i: (i, 0)),
            pl.BlockSpec(w_out.shape, lambda i: (0, 0)),
            pl.BlockSpec((1, d), lambda i: (0, 0)),
            pl.BlockSpec(wr_t.shape, lambda i: (0, 0)),
            pl.BlockSpec(br_col.shape, lambda i: (0, 0)),
        ],
        out_specs=[
            pl.BlockSpec((tm, d), lambda i: (i, 0)),
            pl.BlockSpec((tm, d // 2), lambda i: (i, 0)),
            pl.BlockSpec((ROUTER_ROWS, tm), lambda i: (0, i)),
        ],
        out_shape=[
            jax.ShapeDtypeStruct((n_tok, d), F32),
            jax.ShapeDtypeStruct((n_tok, d // 2), U32),
            jax.ShapeDtypeStruct((ROUTER_ROWS, n_tok), F32),
        ],
        compiler_params=_cparams(("parallel",)),
        name="outproj_tm" if time_major else "outproj",
    )(x2d, out_a, out_b, w_out, g_ffn, wr_t, br_col)


def _router_kernel(lp_ref, ls_ref, dest_ref, wcol_ref, meta_ref, cnt_scr, base_scr, *, n_prompt_tiles):
    p = pl.program_id(0)
    i = pl.program_id(1)
    tm = lp_ref.shape[1]
    lt = jnp.where(i < n_prompt_tiles, lp_ref[...], ls_ref[...])

    m = lt[0:1, :]
    sel = jnp.zeros((1, tm), I32)
    for r in range(1, N_GROUPS):
        upd = lt[r:r + 1, :] > m
        sel = jnp.where(upd, r, sel)
        m = jnp.where(upd, lt[r:r + 1, :], m)
    den = jnp.zeros((1, tm), F32)
    for r in range(N_GROUPS):
        den = den + jnp.exp(lt[r:r + 1, :] - m)
    g_w = 1.0 / den

    ev = []
    for j in range(EXPERTS_PER_GROUP):
        e = jnp.zeros((1, tm), F32)
        for g in range(N_GROUPS):
            row = N_GROUPS + g * EXPERTS_PER_GROUP + j
            e = jnp.where(sel == g, lt[row:row + 1, :], e)
        ev.append(e)
    v0 = ev[0]
    i0 = jnp.zeros((1, tm), I32)
    for j in range(1, EXPERTS_PER_GROUP):
        upd = ev[j] > v0
        i0 = jnp.where(upd, j, i0)
        v0 = jnp.where(upd, ev[j], v0)
    v1 = jnp.full((1, tm), -jnp.inf, F32)
    i1 = jnp.zeros((1, tm), I32)
    for j in range(EXPERTS_PER_GROUP):
        upd = (ev[j] > v1) & (i0 != j)
        i1 = jnp.where(upd, j, i1)
        v1 = jnp.where(upd, ev[j], v1)
    t = jnp.exp(v1 - v0)
    w0 = g_w / (1.0 + t)
    w1 = g_w * t / (1.0 + t)
    e0 = sel * EXPERTS_PER_GROUP + i0
    e1 = sel * EXPERTS_PER_GROUP + i1

    eio = lax.broadcasted_iota(I32, (N_EXPERTS, tm), 0)
    hit0 = eio == e0
    hit1 = eio == e1
    onehot = (hit0 | hit1).astype(F32)
    tile_cnt = jnp.sum(onehot, axis=1, keepdims=True)

    @pl.when((p == 0) & (i == 0))
    def _():
        cnt_scr[...] = jnp.zeros_like(cnt_scr)

    @pl.when(p == 0)
    def _():
        cnt_scr[...] = cnt_scr[...] + tile_cnt

    @pl.when((p == 1) & (i == 0))
    def _():
        cnt = cnt_scr[...]
        padded = jnp.floor((cnt + (MOE_BLOCK - 1)) * (1.0 / MOE_BLOCK)) * MOE_BLOCK
        ri = lax.broadcasted_iota(I32, (N_EXPERTS, N_EXPERTS), 0)
        ci = lax.broadcasted_iota(I32, (N_EXPERTS, N_EXPERTS), 1)
        lower = (ci < ri).astype(F32)
        pad_start = jnp.dot(lower, jnp.broadcast_to(padded, (N_EXPERTS, LANES)),
                            precision=lax.Precision.HIGHEST, preferred_element_type=F32)[:, 0:1]
        base_scr[...] = pad_start
        pad_end = pad_start + padded
        blk0 = (lax.broadcasted_iota(I32, (N_EXPERTS, LANES), 1) * MOE_BLOCK).astype(F32)
        n_le = jnp.sum((pad_end <= blk0).astype(F32), axis=0, keepdims=True)
        block_e = jnp.minimum(n_le, N_EXPERTS - 1.0)
        n_active = jnp.broadcast_to(pad_end[N_EXPERTS - 1:N_EXPERTS, :] * (1.0 / MOE_BLOCK), (1, LANES))
        diag = (lax.broadcasted_iota(I32, (N_EXPERTS, LANES), 0)
                == lax.broadcasted_iota(I32, (N_EXPERTS, LANES), 1))
        blk_start = jnp.sum(jnp.where(diag, pad_start * (1.0 / MOE_BLOCK), 0.0), axis=0, keepdims=True)
        blk_count = jnp.sum(jnp.where(diag, padded * (1.0 / MOE_BLOCK), 0.0), axis=0, keepdims=True)
        meta_ref[...] = jnp.concatenate(
            [block_e, n_active, blk_start, blk_count, jnp.zeros((4, LANES), F32)], axis=0).astype(I32)

    @pl.when(p == 1)
    def _():
        ui = lax.broadcasted_iota(I32, (tm, tm), 0)
        uj = lax.broadcasted_iota(I32, (tm, tm), 1)
        upper = (ui < uj).astype(BF16)
        excl = _dot(onehot.astype(BF16), upper)
        pos = base_scr[...] + excl
        d0 = jnp.sum(jnp.where(hit0, pos, 0.0), axis=0, keepdims=True)
        d1 = jnp.sum(jnp.where(hit1, pos, 0.0), axis=0, keepdims=True)
        dest_ref[...] = jnp.concatenate([d0, d1, jnp.zeros((6, tm), F32)], axis=0).astype(I32)
        wmat = jnp.concatenate([w0, w1, jnp.zeros((LANES - 2, tm), F32)], axis=0)
        wcol_ref[...] = wmat.T
        base_scr[...] = base_scr[...] + tile_cnt


def _router(lt_p, lt_s):
    tm = TOKEN_TILE
    n_p = lt_p.shape[1] // tm
    n_s = lt_s.shape[1] // tm
    n_tok = lt_p.shape[1] + lt_s.shape[1]
    return pl.pallas_call(
        partial(_router_kernel, n_prompt_tiles=n_p),
        grid=(2, n_p + n_s),
        in_specs=[
            pl.BlockSpec((ROUTER_ROWS, tm), lambda p, i: (0, jnp.minimum(i, n_p - 1))),
            pl.BlockSpec((ROUTER_ROWS, tm), lambda p, i: (0, jnp.maximum(i - n_p, 0))),
        ],
        out_specs=[
            pl.BlockSpec((8, tm), lambda p, i: (0, i * p)),
            pl.BlockSpec((tm, LANES), lambda p, i: (i * p, 0)),
            pl.BlockSpec((8, LANES), lambda p, i: (0, 0)),
        ],
        out_shape=[
            jax.ShapeDtypeStruct((8, n_tok), I32),
            jax.ShapeDtypeStruct((n_tok, LANES), F32),
            jax.ShapeDtypeStruct((8, LANES), I32),
        ],
        scratch_shapes=[pltpu.VMEM((N_EXPERTS, 1), F32), pltpu.VMEM((N_EXPERTS, 1), F32)],
        compiler_params=_cparams(("arbitrary", "arbitrary")),
        name="router",
    )(lt_p, lt_s)


def _dispatch_kernel(d0_ref, d1_ref, bs_ref, nb_ref, na_ref, hp_ref, hs_ref, xb_ref, sem, zbuf, zsem,
                     *, n_prompt_tiles, n_blocks):
    i = pl.program_id(0)
    tm = hp_ref.shape[0]

    @pl.when(i == 0)
    def _():
        zbuf[...] = jnp.zeros_like(zbuf)

        def zero_copy(blk):
            rows = pl.ds(pl.multiple_of(blk * MOE_BLOCK, MOE_BLOCK), MOE_BLOCK)
            return pltpu.make_async_copy(zbuf, xb_ref.at[rows], zsem)

        def last_block(e):
            return bs_ref[e] + jnp.maximum(nb_ref[e], 1) - 1

        def start_e(e, carry):
            @pl.when(nb_ref[e] > 0)
            def _():
                zero_copy(last_block(e)).start()
            return carry

        def wait_e(e, carry):
            @pl.when(nb_ref[e] > 0)
            def _():
                zero_copy(last_block(e)).wait()
            return carry

        def start_b(b, carry):
            zero_copy(b).start()
            return carry

        def wait_b(b, carry):
            zero_copy(b).wait()
            return carry

        lax.fori_loop(0, N_EXPERTS, start_e, 0)
        lax.fori_loop(na_ref[0], n_blocks, start_b, 0)
        lax.fori_loop(0, N_EXPERTS, wait_e, 0)
        lax.fori_loop(na_ref[0], n_blocks, wait_b, 0)

    def scatter(h_ref):
        def copy(r, d_ref):
            return pltpu.make_async_copy(h_ref.at[pl.ds(r, 1)], xb_ref.at[pl.ds(d_ref[i * tm + r], 1)], sem)

        def issue(r, carry):
            copy(r, d0_ref).start()
            copy(r, d1_ref).start()
            return carry

        def drain(r, carry):
            copy(r, d0_ref).wait()
            copy(r, d1_ref).wait()
            return carry

        lax.fori_loop(0, tm, issue, 0, unroll=ROW_DMA_UNROLL)
        lax.fori_loop(0, tm, drain, 0, unroll=ROW_DMA_UNROLL)

    @pl.when(i < n_prompt_tiles)
    def _():
        scatter(hp_ref)

    @pl.when(i >= n_prompt_tiles)
    def _():
        scatter(hs_ref)


def _dispatch(d0, d1, block_start, block_count, n_active, h2_p, h2_s, n_blocks):
    tm = TOKEN_TILE
    d = h2_p.shape[1]
    n_p = h2_p.shape[0] // tm
    n_s = h2_s.shape[0] // tm
    return pl.pallas_call(
        partial(_dispatch_kernel, n_prompt_tiles=n_p, n_blocks=n_blocks),
        grid_spec=pltpu.PrefetchScalarGridSpec(
            num_scalar_prefetch=5,
            grid=(n_p + n_s,),
            in_specs=[
                pl.BlockSpec((tm, d), lambda i, *_: (jnp.minimum(i, n_p - 1), 0)),
                pl.BlockSpec((tm, d), lambda i, *_: (jnp.maximum(i - n_p, 0), 0)),
            ],
            out_specs=pl.BlockSpec(memory_space=pl.ANY),
            scratch_shapes=[
                pltpu.SemaphoreType.DMA(()),
                pltpu.VMEM((MOE_BLOCK, d), U32),
                pltpu.SemaphoreType.DMA(()),
            ],
        ),
        out_shape=jax.ShapeDtypeStruct((n_blocks * MOE_BLOCK, d), U32),
        compiler_params=_cparams(("arbitrary",)),
        name="dispatch",
    )(d0, d1, block_start, block_count, n_active, h2_p, h2_s)


def _ffn_kernel(bs_ref, nb_ref, na_ref, xb_ref, wg_ref, wu_ref, wd_ref, yb_ref,
                xbuf, ybuf, xsem, ysem, *, n_blocks):
    e = pl.program_id(0)
    nb = nb_ref[e]
    b0 = bs_ref[e]

    def x_copy(blk, slot):
        rows = pl.ds(pl.multiple_of(blk * MOE_BLOCK, MOE_BLOCK), MOE_BLOCK)
        return pltpu.make_async_copy(xb_ref.at[rows], xbuf.at[slot], xsem.at[slot])

    def y_copy(blk, slot):
        rows = pl.ds(pl.multiple_of(blk * MOE_BLOCK, MOE_BLOCK), MOE_BLOCK)
        return pltpu.make_async_copy(ybuf.at[slot], yb_ref.at[rows], ysem.at[slot])

    @pl.when(nb > 0)
    def _():
        x_copy(b0, 0).start()

    def block(i, carry):
        slot = lax.rem(i, 2)
        x_copy(b0 + i, slot).wait()

        @pl.when(i + 1 < nb)
        def _():
            x_copy(b0 + i + 1, 1 - slot).start()

        x_lo, x_hi = _unpack_bf16_pairs(xbuf[slot])
        half = x_lo.shape[1]
        g = _dot(x_lo, wg_ref[0, 0:half, :].astype(BF16)) + _dot(x_hi, wg_ref[0, half:, :].astype(BF16))
        u = _dot(x_lo, wu_ref[0, 0:half, :].astype(BF16)) + _dot(x_hi, wu_ref[0, half:, :].astype(BF16))
        a = (g * jax.nn.sigmoid(g) * u).astype(BF16)
        y = _dot(a, wd_ref[0].astype(BF16))

        @pl.when(i >= 2)
        def _():
            y_copy(b0 + i - 2, slot).wait()

        ybuf[slot] = y
        y_copy(b0 + i, slot).start()
        return carry

    lax.fori_loop(0, nb, block, 0)

    @pl.when(nb >= 2)
    def _():
        y_copy(b0 + nb - 2, lax.rem(nb, 2)).wait()

    @pl.when(nb >= 1)
    def _():
        y_copy(b0 + nb - 1, lax.rem(nb + 1, 2)).wait()

    @pl.when(e == pl.num_programs(0) - 1)
    def _():
        ybuf[0] = jnp.zeros(ybuf.shape[1:], F32)

        def zero_block(b, carry):
            cp = y_copy(b, 0)
            cp.start()
            cp.wait()
            return carry

        lax.fori_loop(na_ref[0], n_blocks, zero_block, 0)


def _ffn(block_start, block_count, n_active, xb, w_gate, w_up, w_down):
    n_rows, d_packed = xb.shape
    n_exp, d, d_e = w_gate.shape
    n_blocks = n_rows // MOE_BLOCK
    return pl.pallas_call(
        partial(_ffn_kernel, n_blocks=n_blocks),
        grid_spec=pltpu.PrefetchScalarGridSpec(
            num_scalar_prefetch=3,
            grid=(n_exp,),
            in_specs=[
                pl.BlockSpec(memory_space=pl.ANY),
                pl.BlockSpec((1, d, d_e), lambda e, *_: (e, 0, 0)),
                pl.BlockSpec((1, d, d_e), lambda e, *_: (e, 0, 0)),
                pl.BlockSpec((1, d_e, d), lambda e, *_: (e, 0, 0)),
            ],
            out_specs=pl.BlockSpec(memory_space=pl.ANY),
            scratch_shapes=[
                pltpu.VMEM((2, MOE_BLOCK, d_packed), U32),
                pltpu.VMEM((2, MOE_BLOCK, d), F32),
                pltpu.SemaphoreType.DMA((2,)),
                pltpu.SemaphoreType.DMA((2,)),
            ],
        ),
        out_shape=jax.ShapeDtypeStruct((n_rows, d), F32),
        compiler_params=_cparams(("arbitrary",)),
        name="moe_ffn",
    )(block_start, block_count, n_active, xb, w_gate, w_up, w_down)


def _ple_kernel(d0_ref, d1_ref, x1_ref, wcol_ref, p_ref, yb_ref, gple_ref, wpg_ref, wp_ref, gfin_ref,
                y_ref, g_scr, sem, *, time_major, n_steps):
    i = pl.program_id(0)
    tm = x1_ref.shape[0]
    slot = lax.rem(i, 2) if n_steps > 1 else 0

    def copy(tile, r, k, to_slot):
        d_ref = d0_ref if k == 0 else d1_ref
        return pltpu.make_async_copy(yb_ref.at[pl.ds(d_ref[tile * tm + r], 1)],
                                     g_scr.at[to_slot, k, pl.ds(r, 1)], sem.at[to_slot])

    def issue(tile, to_slot):
        def body(r, carry):
            copy(tile, r, 0, to_slot).start()
            copy(tile, r, 1, to_slot).start()
            return carry
        lax.fori_loop(0, tm, body, 0, unroll=ROW_DMA_UNROLL)

    def drain(tile, from_slot):
        def body(r, carry):
            copy(tile, r, 0, from_slot).wait()
            copy(tile, r, 1, from_slot).wait()
            return carry
        lax.fori_loop(0, tm, body, 0, unroll=ROW_DMA_UNROLL)

    @pl.when(i == 0)
    def _():
        issue(0, 0)

    if n_steps > 1:
        @pl.when(i + 1 < n_steps)
        def _():
            issue(i + 1, 1 - slot)

    drain(i, slot)

    x2 = x1_ref[...] + wcol_ref[:, 0:1] * g_scr[slot, 0] + wcol_ref[:, 1:2] * g_scr[slot, 1]
    hn = _rms(x2, gple_ref[...]).astype(BF16)
    gate = jax.nn.sigmoid(_dot(hn, wpg_ref[...]))
    if time_major:
        pd = wp_ref.shape[0]
        pp = jnp.concatenate([p_ref[:, t * pd:(t + 1) * pd] for t in range(p_ref.shape[1] // pd)], axis=0)
    else:
        pp = p_ref[...]
    x3 = x2 + _dot(pp.astype(BF16), wp_ref[...]) * gate
    y = _rms(x3, gfin_ref[...])
    if time_major:
        d = y.shape[1]
        rows = y_ref.shape[0]
        for t in range(y_ref.shape[1] // d):
            y_ref[:, t * d:(t + 1) * d] = y[t * rows:(t + 1) * rows, :]
    else:
        y_ref[...] = y


def _ple(d0, d1, x1, wcol, p2d, yb, g_ple, w_pg, w_p, g_final, *, time_major):
    n_tok, d = x1.shape
    tm = n_tok if time_major else PLE_TILE
    n_steps = n_tok // tm
    pd = w_p.shape[0]
    if time_major:
        n_seq = p2d.shape[0]
        p_spec = pl.BlockSpec(p2d.shape, lambda i, *_: (0, 0))
        y_spec = pl.BlockSpec((n_seq, (n_tok // n_seq) * d), lambda i, *_: (0, 0))
        y_shape = jax.ShapeDtypeStruct((n_seq, (n_tok // n_seq) * d), F32)
    else:
        p_spec = pl.BlockSpec((tm, pd), lambda i, *_: (i, 0))
        y_spec = pl.BlockSpec((tm, d), lambda i, *_: (i, 0))
        y_shape = jax.ShapeDtypeStruct((n_tok, d), F32)
    return pl.pallas_call(
        partial(_ple_kernel, time_major=time_major, n_steps=n_steps),
        grid_spec=pltpu.PrefetchScalarGridSpec(
            num_scalar_prefetch=2,
            grid=(n_steps,),
            in_specs=[
                pl.BlockSpec((tm, d), lambda i, *_: (i, 0)),
                pl.BlockSpec((tm, LANES), lambda i, *_: (i, 0)),
                p_spec,
                pl.BlockSpec(memory_space=pl.ANY),
                pl.BlockSpec((1, d), lambda i, *_: (0, 0)),
                pl.BlockSpec(w_pg.shape, lambda i, *_: (0, 0)),
                pl.BlockSpec(w_p.shape, lambda i, *_: (0, 0)),
                pl.BlockSpec((1, d), lambda i, *_: (0, 0)),
            ],
            out_specs=y_spec,
            scratch_shapes=[
                pltpu.VMEM((min(n_steps, 2), 2, tm, d), F32),
                pltpu.SemaphoreType.DMA((min(n_steps, 2),)),
            ],
        ),
        out_shape=y_shape,
        compiler_params=_cparams(("arbitrary",)),
        name="ple_tm" if time_major else "ple",
    )(d0, d1, x1, wcol, p2d, yb, g_ple, w_pg, w_p, g_final)


def _pad_lanes(v, offset=0):
    row = jnp.zeros((1, LANES), F32)
    return row.at[0, offset:offset + v.shape[0]].set(v.astype(F32))


def kernel(x_prompt, x_sample, state_conv, state_delta, p_prompt, p_sample, g_mix, w_in, w_s, b_s, g_v,
           conv_w, a_log, dt_bias, g_out, w_out, g_ffn, w_group, b_group, w_router, b_router, w_gate, w_up,
           w_down, g_ple, w_ple_gate, w_ple, g_final):
    n_batch, seq, d = x_prompt.shape
    n_seq, n_pos, _ = x_sample.shape
    depth = g_mix.shape[0]
    assert depth == 1
    a_width = A_GROUPS * A_GROUP_DIM
    hd = B_HEADS * B_DIM
    n_main = 2 * a_width + 4 * hd
    n_p = n_batch * seq
    n_s = n_seq * n_pos

    xp = x_prompt.reshape(n_p, d)
    xs = x_sample.reshape(n_seq, n_pos * d)
    l = 0

    w_main = _cast_bf16(w_in, n_main)
    w_tail = jnp.pad(w_in[l][:, n_main:], ((0, 0), (0, LANES - 2 * B_HEADS))).astype(BF16)
    gm = g_mix[l].reshape(1, d)
    causal = jnp.tril(jnp.ones((A_CHUNK, A_CHUNK), dtype=bool))
    ws_tril = jnp.where(causal, w_s[l], 0).astype(BF16)
    b_t = b_s[l].T
    gv = g_v[l].reshape(1, a_width)
    alog_row = _pad_lanes(a_log[l])
    dtb_row = _pad_lanes(dt_bias[l])
    gout = g_out[l].reshape(1, B_DIM)
    w_o = w_out[l].astype(BF16)
    gf = g_ffn[l].reshape(1, d)
    wr = jnp.pad(jnp.concatenate([w_group[l], w_router[l]], axis=1),
                 ((0, 0), (0, LANES - N_GROUPS - N_EXPERTS)))
    wr_hi = wr.astype(BF16)
    wr_t = jnp.concatenate([wr_hi, (wr - wr_hi.astype(F32)).astype(BF16)], axis=1)
    br_col = jnp.broadcast_to(
        jnp.pad(jnp.concatenate([b_group[l], b_router[l]]), (0, ROUTER_ROWS - N_GROUPS - N_EXPERTS))[:, None],
        (ROUTER_ROWS, LANES))
    gp = g_ple[l].reshape(1, d)
    w_pg = w_ple_gate[l].astype(BF16)
    w_p = w_ple[l].astype(BF16)
    gfin = g_final.reshape(1, d)

    proj_p, tail_p = _inproj(xp, gm, w_main, w_tail, n_main, time_major=False)
    proj_s, tail_s = _inproj(xs, gm, w_main, w_tail, n_main, time_major=True)
    oa_p = _mixa_prompt(proj_p, ws_tril, b_t, gv, n_p)
    oa_s, va_s = _mixa_sample(proj_s, w_s[l], b_s[l], gv, n_seq, n_pos)
    ob_p, sd_p, cb_p = _gdn_prompt(proj_p, tail_p, conv_w[l], alog_row, dtb_row, gout, n_batch, seq)
    cbuf = state_conv[l].reshape(n_seq, (CONV_W - 1) * 3 * hd)
    ob_s, sd_s, cb_s = _gdn_decode(proj_s, tail_s, cbuf, state_delta[l], conv_w[l], alog_row, dtb_row,
                                   gout, n_seq, n_pos)
    ob_s = ob_s.reshape(n_s, hd)

    x1_p, h2_p, lt_p = _outproj(xp, oa_p, ob_p, w_o, gf, wr_t, br_col, time_major=False)
    x1_s, h2_s, lt_s = _outproj(xs, oa_s, ob_s, w_o, gf, wr_t, br_col, time_major=True)
    dest, wcol, meta = _router(lt_p, lt_s)
    d0, d1 = dest[0], dest[1]
    n_active = meta[1, 0:1]
    block_start = meta[2, :N_EXPERTS]
    block_count = meta[3, :N_EXPERTS]

    n_tok = n_p + n_s
    n_blocks = (n_tok * 2) // MOE_BLOCK + N_EXPERTS
    xb = _dispatch(d0, d1, block_start, block_count, n_active, h2_p, h2_s, n_blocks)
    yb = _ffn(block_start, block_count, n_active, xb, w_gate[l], w_up[l], w_down[l])

    pp = p_prompt[l].reshape(n_p, -1)
    ps = p_sample[l].reshape(n_seq, -1)
    y_p = _ple(d0[:n_p], d1[:n_p], x1_p, wcol[:n_p], pp, yb, gp, w_pg, w_p, gfin, time_major=False)
    y_s = _ple(d0[n_p:], d1[n_p:], x1_s, wcol[n_p:], ps, yb, gp, w_pg, w_p, gfin, time_major=True)

    return (
        y_p.reshape(n_batch, seq, d),
        y_s.reshape(n_seq, n_pos, d),
        cb_p[None],
        sd_p[None],
        cb_s.reshape(n_seq, CONV_W - 1, 3 * hd)[None],
        sd_s[None],
        va_s.reshape(n_seq, n_pos, a_width)[None],
    )
```

```python
from functools import partial

import jax
import jax.numpy as jnp
from jax import lax
from jax.experimental import pallas as pl
from jax.experimental.pallas import tpu as pltpu

F32 = jnp.float32
BF16 = jnp.bfloat16
I32 = jnp.int32
U32 = jnp.uint32

EPS = 1e-6
LANES = 128
A_GROUPS = 8
A_GROUP_DIM = 128
A_CHUNK = 128
B_HEADS = 8
B_DIM = 128
CONV_W = 4
N_GROUPS = 4
EXPERTS_PER_GROUP = 8
N_EXPERTS = N_GROUPS * EXPERTS_PER_GROUP
ROUTER_ROWS = 40
TOKEN_TILE = 512
INPROJ_TILE = 1024
PLE_TILE = 256
GDN_CHUNK = 64
MOE_BLOCK = 256
DEC_SEQ_TILE = 8
ROW_DMA_UNROLL = 8
VMEM_LIMIT = 56 * 1024 * 1024


def _cparams(sem):
    return pltpu.CompilerParams(dimension_semantics=sem, vmem_limit_bytes=VMEM_LIMIT)


def _rms(x, g):
    return x * lax.rsqrt(jnp.mean(x * x, axis=-1, keepdims=True) + EPS) * g


def _softplus(x):
    return jnp.maximum(x, 0.0) + jnp.log(1.0 + jnp.exp(-jnp.abs(x)))


def _dot(a, b):
    return jnp.dot(a, b, preferred_element_type=F32)


def _dot_nt(a, b, precision=None):
    return lax.dot_general(a, b, (((1,), (1,)), ((), ())), precision=precision,
                           preferred_element_type=F32)


def _dot_tn(a, b):
    return lax.dot_general(a, b, (((0,), (0,)), ((), ())), preferred_element_type=F32)


def _pack_bf16_pairs(x):
    n = x.shape[1] // 2
    lo = lax.bitcast_convert_type(x[:, :n].astype(BF16).astype(F32), U32) >> 16
    hi = lax.bitcast_convert_type(x[:, n:].astype(BF16).astype(F32), U32) & jnp.uint32(0xFFFF0000)
    return hi | lo


def _unpack_bf16_pairs(xu):
    lo = lax.bitcast_convert_type(xu << 16, F32).astype(BF16)
    hi = lax.bitcast_convert_type(xu & jnp.uint32(0xFFFF0000), F32).astype(BF16)
    return lo, hi


def _cast_kernel(w_ref, t_ref, o_ref, ot_ref, *, n_tail):
    o_ref[...] = w_ref[0].astype(BF16)
    lane = lax.broadcasted_iota(I32, t_ref.shape[1:], 1)
    ot_ref[...] = jnp.where(lane < n_tail, t_ref[0], 0.0).astype(BF16)


def _cast_bf16(w3d, n_cols, col_tile=1024):
    rows, total = w3d.shape[1], w3d.shape[2]
    return pl.pallas_call(
        partial(_cast_kernel, n_tail=total - n_cols),
        grid=(n_cols // col_tile,),
        in_specs=[
            pl.BlockSpec((1, rows, col_tile), lambda j: (0, 0, j)),
            pl.BlockSpec((1, rows, LANES), lambda j: (0, 0, n_cols // LANES)),
        ],
        out_specs=[
            pl.BlockSpec((rows, col_tile), lambda j: (0, j)),
            pl.BlockSpec((rows, LANES), lambda j: (0, 0)),
        ],
        out_shape=[
            jax.ShapeDtypeStruct((rows, n_cols), BF16),
            jax.ShapeDtypeStruct((rows, LANES), BF16),
        ],
        compiler_params=_cparams(("arbitrary",)),
        name="cast_bf16",
    )(w3d, w3d)


def _inproj_kernel(x_ref, g_ref, w_ref, wt_ref, proj_ref, tail_ref, h_scr, *, time_major):
    j = pl.program_id(1)

    @pl.when(j == 0)
    def _():
        g = g_ref[...]
        if time_major:
            d = g.shape[-1]
            rows = x_ref.shape[0]
            for t in range(x_ref.shape[1] // d):
                h_scr[t * rows:(t + 1) * rows, :] = _rms(x_ref[:, t * d:(t + 1) * d], g).astype(BF16)
        else:
            h_scr[...] = _rms(x_ref[...], g).astype(BF16)
        tail_ref[...] = _dot(h_scr[...], wt_ref[...])

    proj_ref[...] = _dot(h_scr[...], w_ref[...])


def _inproj(x2d, g_mix, w_main, w_tail, n_main, *, time_major, col_tile=1024):
    d = g_mix.shape[-1]
    if time_major:
        n_tok = x2d.shape[0] * (x2d.shape[1] // d)
        tm = n_tok
        x_spec = pl.BlockSpec(x2d.shape, lambda i, j: (0, 0))
    else:
        n_tok = x2d.shape[0]
        tm = min(INPROJ_TILE, n_tok)
        x_spec = pl.BlockSpec((tm, d), lambda i, j: (i, 0))
    grid = (n_tok // tm, n_main // col_tile)
    return pl.pallas_call(
        partial(_inproj_kernel, time_major=time_major),
        grid=grid,
        in_specs=[
            x_spec,
            pl.BlockSpec((1, d), lambda i, j: (0, 0)),
            pl.BlockSpec((d, col_tile), lambda i, j: (0, j)),
            pl.BlockSpec((d, LANES), lambda i, j: (0, 0)),
        ],
        out_specs=[
            pl.BlockSpec((tm, col_tile), lambda i, j: (i, j)),
            pl.BlockSpec((tm, LANES), lambda i, j: (i, 0)),
        ],
        out_shape=[
            jax.ShapeDtypeStruct((n_tok, n_main), F32),
            jax.ShapeDtypeStruct((n_tok, LANES), F32),
        ],
        scratch_shapes=[pltpu.VMEM((tm, d), BF16)],
        compiler_params=_cparams(("parallel", "arbitrary")),
        name="inproj_tm" if time_major else "inproj",
    )(x2d, g_mix, w_main, w_tail)


def _group_ln(v, g):
    mu = jnp.mean(v, axis=-1, keepdims=True)
    dlt = v - mu
    var = jnp.mean(dlt * dlt, axis=-1, keepdims=True)
    return dlt * lax.rsqrt(var + EPS) * g


def _mixa_prompt_kernel(au_ref, av_ref, ws_ref, bt_ref, gv_ref, out_ref):
    n_chunks = au_ref.shape[0] // A_CHUNK
    for c in range(n_chunks):
        rows = slice(c * A_CHUNK, (c + 1) * A_CHUNK)
        for h in range(A_GROUPS):
            cols = slice(h * A_GROUP_DIM, (h + 1) * A_GROUP_DIM)
            u = jax.nn.gelu(au_ref[rows, cols])
            v = _group_ln(jax.nn.gelu(av_ref[rows, cols]), gv_ref[:, cols])
            mixed = _dot(ws_ref[h], v.astype(BF16)) + bt_ref[:, h:h + 1]
            out_ref[rows, cols] = (u * mixed).astype(BF16)


def _mixa_prompt(proj, ws_tril, b_t, g_v, n_tok):
    a_width = A_GROUPS * A_GROUP_DIM
    tm = TOKEN_TILE
    return pl.pallas_call(
        _mixa_prompt_kernel,
        grid=(n_tok // tm,),
        in_specs=[
            pl.BlockSpec((tm, a_width), lambda i: (i, 0)),
            pl.BlockSpec((tm, a_width), lambda i: (i, 1)),
            pl.BlockSpec((A_GROUPS, A_CHUNK, A_CHUNK), lambda i: (0, 0, 0)),
            pl.BlockSpec((A_CHUNK, A_GROUPS), lambda i: (0, 0)),
            pl.BlockSpec((1, a_width), lambda i: (0, 0)),
        ],
        out_specs=pl.BlockSpec((tm, a_width), lambda i: (i, 0)),
        out_shape=jax.ShapeDtypeStruct((n_tok, a_width), BF16),
        compiler_params=_cparams(("parallel",)),
        name="mixa_prompt",
    )(proj, proj, ws_tril, b_t, g_v)


def _mixa_sample_kernel(ws_ref, bs_ref, au_ref, av_ref, gv_ref, out_ref, va_ref, *, n_seq, n_pos):
    a_width = A_GROUPS * A_GROUP_DIM
    vs = []
    for t in range(n_pos):
        rows = slice(t * n_seq, (t + 1) * n_seq)
        per_group = []
        for h in range(A_GROUPS):
            cols = slice(h * A_GROUP_DIM, (h + 1) * A_GROUP_DIM)
            per_group.append(_group_ln(jax.nn.gelu(av_ref[rows, cols]), gv_ref[:, cols]))
        vs.append(per_group)
        for h in range(A_GROUPS):
            va_ref[:, t * a_width + h * A_GROUP_DIM:t * a_width + (h + 1) * A_GROUP_DIM] = per_group[h]
    for t in range(n_pos):
        rows = slice(t * n_seq, (t + 1) * n_seq)
        for h in range(A_GROUPS):
            cols = slice(h * A_GROUP_DIM, (h + 1) * A_GROUP_DIM)
            mixed = jnp.zeros((n_seq, A_GROUP_DIM), F32) + bs_ref[h * n_pos + t]
            for s in range(t + 1):
                mixed = mixed + ws_ref[(h * n_pos + t) * n_pos + s] * vs[s][h]
            out_ref[rows, cols] = (jax.nn.gelu(au_ref[rows, cols]) * mixed).astype(BF16)


def _mixa_sample(proj, w_s, b_s, g_v, n_seq, n_pos):
    a_width = A_GROUPS * A_GROUP_DIM
    n_tok = n_seq * n_pos
    return pl.pallas_call(
        partial(_mixa_sample_kernel, n_seq=n_seq, n_pos=n_pos),
        grid_spec=pltpu.PrefetchScalarGridSpec(
            num_scalar_prefetch=2,
            grid=(1,),
            in_specs=[
                pl.BlockSpec((n_tok, a_width), lambda i, *_: (0, 0)),
                pl.BlockSpec((n_tok, a_width), lambda i, *_: (0, 1)),
                pl.BlockSpec((1, a_width), lambda i, *_: (0, 0)),
            ],
            out_specs=[
                pl.BlockSpec((n_tok, a_width), lambda i, *_: (0, 0)),
                pl.BlockSpec((n_seq, n_pos * a_width), lambda i, *_: (0, 0)),
            ],
        ),
        out_shape=[
            jax.ShapeDtypeStruct((n_tok, a_width), BF16),
            jax.ShapeDtypeStruct((n_seq, n_pos * a_width), F32),
        ],
        compiler_params=_cparams(("arbitrary",)),
        name="mixa_sample",
    )(w_s[:, :n_pos, :n_pos].reshape(-1), b_s[:, :n_pos].reshape(-1), proj, proj, g_v)


def _gdn_prompt_kernel(q_ref, k_ref, v_ref, z_ref, tail_ref, cw_ref, alog_ref, dtb_ref, gout_ref,
                       out_ref, s_out_ref, conv_out_ref, s_scr, xbuf):
    c = pl.program_id(1)
    n_c = pl.num_programs(1)
    C = q_ref.shape[0]
    hd = B_HEADS * B_DIM
    pad = 8
    lo = pad - (CONV_W - 1)

    @pl.when(c == 0)
    def _():
        s_scr[...] = jnp.zeros_like(s_scr)
        xbuf[0:pad, :] = jnp.zeros((pad, 3 * hd), F32)

    xbuf[pad:pad + C, 0:hd] = q_ref[...]
    xbuf[pad:pad + C, hd:2 * hd] = k_ref[...]
    xbuf[pad:pad + C, 2 * hd:3 * hd] = v_ref[...]
    y = xbuf[lo:lo + C, :] * cw_ref[0:1, :]
    for j in range(1, CONV_W):
        y = y + xbuf[lo + j:lo + j + C, :] * cw_ref[j:j + 1, :]
    y = y * jax.nn.sigmoid(y)
    last = xbuf[pad + C - (CONV_W - 1):pad + C, :]
    xbuf[lo:pad, :] = last

    @pl.when(c == n_c - 1)
    def _():
        conv_out_ref[0] = last

    tail = tail_ref[...]
    g_all = -jnp.exp(alog_ref[...]) * _softplus(tail + dtb_ref[...])
    beta_all = jax.nn.sigmoid(tail)
    ri = lax.broadcasted_iota(I32, (C, C), 0)
    ci = lax.broadcasted_iota(I32, (C, C), 1)
    causal = ci <= ri
    strict = ci < ri
    tril = causal.astype(F32)
    G_all = jnp.dot(tril, g_all, precision=lax.Precision.HIGHEST, preferred_element_type=F32)
    G_rows = jnp.concatenate([G_all, jnp.zeros((LANES - C, LANES), F32)], axis=0).T if C < LANES else G_all.T
    scale = B_DIM ** -0.5
    n_levels = max(C.bit_length() - 2, 0)
    heads = range(B_HEADS)

    qs, ks, Gs, eGs, rhss, qks, Ps = [], [], [], [], [], [], []
    for h in heads:
        q = y[:, h * B_DIM:(h + 1) * B_DIM]
        k = y[:, hd + h * B_DIM:hd + (h + 1) * B_DIM]
        v = y[:, 2 * hd + h * B_DIM:2 * hd + (h + 1) * B_DIM]
        q = q * lax.rsqrt(jnp.sum(q * q, axis=-1, keepdims=True) + EPS) * scale
        k = k * lax.rsqrt(jnp.sum(k * k, axis=-1, keepdims=True) + EPS)
        G = G_all[:, h:h + 1]
        beta = beta_all[:, B_HEADS + h:B_HEADS + h + 1]
        eG = jnp.exp(G)
        decay = jnp.exp(jnp.where(causal, G - G_rows[h:h + 1, 0:C], -jnp.inf))
        qk_kk = _dot_nt(jnp.concatenate([q, k], axis=0), k)
        qks.append(qk_kk[:C] * decay)
        Ps.append(jnp.where(strict, -(beta * qk_kk[C:] * decay), 0.0))
        rhss.append(jnp.concatenate([v * beta, k * (beta * eG)], axis=1))
        qs.append(q)
        ks.append(k)
        Gs.append(G)
        eGs.append(eG)

    Ls = list(Ps)
    if n_levels > 0:
        Ps = [_dot(P, P) for P in Ps]
    for lvl in range(1, n_levels + 1):
        for h in heads:
            if lvl < n_levels:
                R = _dot(jnp.concatenate([Ps[h], Ls[h]], axis=0), Ps[h])
                Ls[h] = Ls[h] + Ps[h] + R[C:]
                Ps[h] = R[:C]
            else:
                Ls[h] = Ls[h] + Ps[h] + _dot(Ls[h], Ps[h])

    uws = [rhss[h] + _dot(Ls[h], rhss[h]) for h in heads]
    Ss = [s_scr[h] for h in heads]
    rs = [_dot(jnp.concatenate([uws[h][:, B_DIM:], qs[h] * eGs[h]], axis=0), Ss[h]) for h in heads]
    v_news = [uws[h][:, :B_DIM] - rs[h][:C] for h in heads]
    os_ = [rs[h][C:] + _dot(qks[h], v_news[h]) for h in heads]
    for h in heads:
        G_last = Gs[h][C - 1:C, :]
        kd = ks[h] * jnp.exp(G_last - Gs[h])
        s_scr[h] = Ss[h] * jnp.exp(G_last) + _dot_tn(kd, v_news[h])
    for h in heads:
        cols = slice(h * B_DIM, (h + 1) * B_DIM)
        o = _rms(os_[h], gout_ref[...])
        z = z_ref[:, cols]
        out_ref[:, cols] = (o * (z * jax.nn.sigmoid(z))).astype(BF16)

    @pl.when(c == n_c - 1)
    def _():
        s_out_ref[0] = s_scr[...]


def _gdn_prompt(proj, tail, conv_w, alog_row, dtb_row, g_out, n_batch, seq):
    C = GDN_CHUNK
    n_c = seq // C
    hd = B_HEADS * B_DIM
    row = lambda b, c: b * n_c + c
    return pl.pallas_call(
        _gdn_prompt_kernel,
        grid=(n_batch, n_c),
        in_specs=[
            pl.BlockSpec((C, hd), lambda b, c: (row(b, c), 2)),
            pl.BlockSpec((C, hd), lambda b, c: (row(b, c), 3)),
            pl.BlockSpec((C, hd), lambda b, c: (row(b, c), 4)),
            pl.BlockSpec((C, hd), lambda b, c: (row(b, c), 5)),
            pl.BlockSpec((C, LANES), lambda b, c: (row(b, c), 0)),
            pl.BlockSpec((CONV_W, 3 * hd), lambda b, c: (0, 0)),
            pl.BlockSpec((1, LANES), lambda b, c: (0, 0)),
            pl.BlockSpec((1, LANES), lambda b, c: (0, 0)),
            pl.BlockSpec((1, B_DIM), lambda b, c: (0, 0)),
        ],
        out_specs=[
            pl.BlockSpec((C, hd), lambda b, c: (row(b, c), 0)),
            pl.BlockSpec((1, B_HEADS, B_DIM, B_DIM), lambda b, c: (b, 0, 0, 0)),
            pl.BlockSpec((1, CONV_W - 1, 3 * hd), lambda b, c: (b, 0, 0)),
        ],
        out_shape=[
            jax.ShapeDtypeStruct((n_batch * seq, hd), BF16),
            jax.ShapeDtypeStruct((n_batch, B_HEADS, B_DIM, B_DIM), F32),
            jax.ShapeDtypeStruct((n_batch, CONV_W - 1, 3 * hd), F32),
        ],
        scratch_shapes=[
            pltpu.VMEM((B_HEADS, B_DIM, B_DIM), F32),
            pltpu.VMEM((8 + C, 3 * hd), F32),
        ],
        compiler_params=_cparams(("parallel", "arbitrary")),
        name="gdn_prompt",
    )(proj, proj, proj, proj, tail, conv_w, alog_row, dtb_row, g_out)


def _gdn_decode_kernel(*refs, n_pos):
    proj_refs = refs[:n_pos]
    tail_refs = refs[n_pos:2 * n_pos]
    (cbuf_ref, s_ref, cw_ref, alog_ref, dtb_ref, gout_ref,
     out_ref, s_out_ref, conv_out_ref, lhs_scr, res_scr, kd_scr, vn_scr, gl_scr) = refs[2 * n_pos:]
    SB = s_ref.shape[0]
    hd = B_HEADS * B_DIM
    qkv0 = 2 * hd
    z0 = qkv0 + 3 * hd
    scale = B_DIM ** -0.5

    xp = [cbuf_ref[:, j * 3 * hd:(j + 1) * 3 * hd] for j in range(CONV_W - 1)]
    xp += [proj_refs[t][:, qkv0:qkv0 + 3 * hd] for t in range(n_pos)]
    for j in range(CONV_W - 1):
        conv_out_ref[:, j * 3 * hd:(j + 1) * 3 * hd] = xp[n_pos + j]
    ys = []
    for t in range(n_pos):
        y = xp[t] * cw_ref[0:1, :]
        for j in range(1, CONV_W):
            y = y + xp[t + j] * cw_ref[j:j + 1, :]
        ys.append(y * jax.nn.sigmoid(y))
    gs, betas = [], []
    for t in range(n_pos):
        tail = tail_refs[t][...]
        gs.append(-jnp.exp(alog_ref[...]) * _softplus(tail + dtb_ref[...]))
        betas.append(jax.nn.sigmoid(tail))

    kd_scr[...] = jnp.zeros_like(kd_scr)
    vn_scr[...] = jnp.zeros_like(vn_scr)
    heads = range(B_HEADS)
    stride = 2 * n_pos

    qs, ks, Gs, us = [], [], [], []
    for h in heads:
        q_h, k_h, v_h, G_h, b_h = [], [], [], [], []
        G = None
        for t in range(n_pos):
            q = ys[t][:, h * B_DIM:(h + 1) * B_DIM]
            k = ys[t][:, hd + h * B_DIM:hd + (h + 1) * B_DIM]
            q_h.append(q * lax.rsqrt(jnp.sum(q * q, axis=-1, keepdims=True) + EPS) * scale)
            k_h.append(k * lax.rsqrt(jnp.sum(k * k, axis=-1, keepdims=True) + EPS))
            v_h.append(ys[t][:, 2 * hd + h * B_DIM:2 * hd + (h + 1) * B_DIM])
            g = gs[t][:, h:h + 1]
            G = g if G is None else G + g
            G_h.append(G)
            b_h.append(betas[t][:, B_HEADS + h:B_HEADS + h + 1])
        u_h, w_h = [], []
        for t in range(n_pos):
            u = v_h[t] * b_h[t]
            w = k_h[t] * (b_h[t] * jnp.exp(G_h[t]))
            for s in range(t):
                a = b_h[t] * jnp.sum(k_h[t] * k_h[s], axis=-1, keepdims=True) * jnp.exp(G_h[t] - G_h[s])
                u = u - a * u_h[s]
                w = w - a * w_h[s]
            u_h.append(u)
            w_h.append(w)
        for t in range(n_pos):
            lhs_scr[h, pl.ds(t, SB, stride=stride), :] = w_h[t]
            lhs_scr[h, pl.ds(n_pos + t, SB, stride=stride), :] = q_h[t] * jnp.exp(G_h[t])
        qs.append(q_h)
        ks.append(k_h)
        Gs.append(G_h)
        us.append(u_h)

    for h in heads:
        for s in range(SB):
            rows = slice(s * stride, (s + 1) * stride)
            res_scr[h, rows, :] = _dot(lhs_scr[h, rows, :], s_ref[s, h])

    for h in heads:
        cols = slice(h * B_DIM, (h + 1) * B_DIM)
        v_news = [us[h][t] - res_scr[h, pl.ds(t, SB, stride=stride), :] for t in range(n_pos)]
        G_last = Gs[h][n_pos - 1]
        for t in range(n_pos):
            o = res_scr[h, pl.ds(n_pos + t, SB, stride=stride), :]
            for s in range(t + 1):
                qk = jnp.sum(qs[h][t] * ks[h][s], axis=-1, keepdims=True) * jnp.exp(Gs[h][t] - Gs[h][s])
                o = o + qk * v_news[s]
            o = _rms(o, gout_ref[...])
            z = proj_refs[t][:, z0 + h * B_DIM:z0 + (h + 1) * B_DIM]
            out_ref[t, :, cols] = (o * (z * jax.nn.sigmoid(z))).astype(BF16)
            kd_scr[h, pl.ds(t, SB, stride=stride), :] = ks[h][t] * jnp.exp(G_last - Gs[h][t])
            vn_scr[h, pl.ds(t, SB, stride=stride), :] = v_news[t]
        gl_scr[h] = jnp.broadcast_to(jnp.exp(G_last), (SB, B_DIM))

    for h in heads:
        for s in range(SB):
            rows = slice(s * stride, (s + 1) * stride)
            s_out_ref[s, h] = (s_ref[s, h] * gl_scr[h, s:s + 1, 0:1]
                               + _dot_tn(kd_scr[h, rows, :], vn_scr[h, rows, :]))


def _gdn_decode(proj_s, tail_s, cbuf, s0, conv_w, alog_row, dtb_row, g_out, n_seq, n_pos):
    SB = DEC_SEQ_TILE
    hd = B_HEADS * B_DIM
    n_main = proj_s.shape[1]
    per_pos = n_seq // SB
    proj_specs = [pl.BlockSpec((SB, n_main), lambda i, t=t: (t * per_pos + i, 0)) for t in range(n_pos)]
    tail_specs = [pl.BlockSpec((SB, LANES), lambda i, t=t: (t * per_pos + i, 0)) for t in range(n_pos)]
    return pl.pallas_call(
        partial(_gdn_decode_kernel, n_pos=n_pos),
        grid=(per_pos,),
        in_specs=proj_specs + tail_specs + [
            pl.BlockSpec((SB, (CONV_W - 1) * 3 * hd), lambda i: (i, 0)),
            pl.BlockSpec((SB, B_HEADS, B_DIM, B_DIM), lambda i: (i, 0, 0, 0)),
            pl.BlockSpec((CONV_W, 3 * hd), lambda i: (0, 0)),
            pl.BlockSpec((1, LANES), lambda i: (0, 0)),
            pl.BlockSpec((1, LANES), lambda i: (0, 0)),
            pl.BlockSpec((1, B_DIM), lambda i: (0, 0)),
        ],
        out_specs=[
            pl.BlockSpec((n_pos, SB, hd), lambda i: (0, i, 0)),
            pl.BlockSpec((SB, B_HEADS, B_DIM, B_DIM), lambda i: (i, 0, 0, 0)),
            pl.BlockSpec((SB, (CONV_W - 1) * 3 * hd), lambda i: (i, 0)),
        ],
        out_shape=[
            jax.ShapeDtypeStruct((n_pos, n_seq, hd), BF16),
            jax.ShapeDtypeStruct((n_seq, B_HEADS, B_DIM, B_DIM), F32),
            jax.ShapeDtypeStruct((n_seq, (CONV_W - 1) * 3 * hd), F32),
        ],
        scratch_shapes=[
            pltpu.VMEM((B_HEADS, SB * 2 * n_pos, B_DIM), F32),
            pltpu.VMEM((B_HEADS, SB * 2 * n_pos, B_DIM), F32),
            pltpu.VMEM((B_HEADS, SB * 2 * n_pos, B_DIM), F32),
            pltpu.VMEM((B_HEADS, SB * 2 * n_pos, B_DIM), F32),
            pltpu.VMEM((B_HEADS, SB, B_DIM), F32),
        ],
        compiler_params=_cparams(("parallel",)),
        name="gdn_decode",
    )(*([proj_s] * n_pos), *([tail_s] * n_pos), cbuf, s0, conv_w, alog_row, dtb_row, g_out)


def _outproj_kernel(x_ref, oa_ref, ob_ref, w_ref, g_ref, wr_ref, br_ref,
                    x1_ref, h2_ref, lt_ref, *, time_major):
    a_width = oa_ref.shape[1]
    acc = _dot(oa_ref[...], w_ref[0:a_width, :]) + _dot(ob_ref[...], w_ref[a_width:, :])
    if time_major:
        d = g_ref.shape[-1]
        rows = x_ref.shape[0]
        x = jnp.concatenate([x_ref[:, t * d:(t + 1) * d] for t in range(x_ref.shape[1] // d)], axis=0)
    else:
        x = x_ref[...]
    x1 = x + acc
    x1_ref[...] = x1
    h2 = _rms(x1, g_ref[...])
    h2_ref[...] = _pack_bf16_pairs(h2)
    h_hi = h2.astype(BF16)
    h_lo = (h2 - h_hi.astype(F32)).astype(BF16)
    tm = h2.shape[0]
    lg = _dot(jnp.concatenate([h_hi, h_lo], axis=0), wr_ref[...])
    lg = lg[:tm, :LANES] + lg[:tm, LANES:] + lg[tm:, :LANES] + lg[tm:, LANES:]
    lt_ref[...] = lg.T[0:ROUTER_ROWS, :] + br_ref[:, 0:1]


def _outproj(x2d, out_a, out_b, w_out, g_ffn, wr_t, br_col, *, time_major):
    d = g_ffn.shape[-1]
    a_width = out_a.shape[1]
    n_tok = out_a.shape[0]
    tm = TOKEN_TILE
    if time_major:
        x_spec = pl.BlockSpec(x2d.shape, lambda i: (0, 0))
    else:
        x_spec = pl.BlockSpec((tm, d), lambda i: (i, 0))
    return pl.pallas_call(
        partial(_outproj_kernel, time_major=time_major),
        grid=(n_tok // tm,),
        in_specs=[
            x_spec,
            pl.BlockSpec((tm, a_width), lambda i: (i, 0)),
            pl.BlockSpec((tm, out_b.shape[1]), lambda i: (i, 0)),
            pl.BlockSpec(w_out.shape, lambda i: (0, 0)),
            pl.BlockSpec((1, d), lambda i: (0, 0)),
            pl.BlockSpec(wr_t.shape, lambda i: (0, 0)),
            pl.BlockSpec(br_col.shape, lambda i: (0, 0)),
        ],
        out_specs=[
            pl.BlockSpec((tm, d), lambda i: (i, 0)),
            pl.BlockSpec((tm, d // 2), lambda i: (i, 0)),
            pl.BlockSpec((ROUTER_ROWS, tm), lambda i: (0, i)),
        ],
        out_shape=[
            jax.ShapeDtypeStruct((n_tok, d), F32),
            jax.ShapeDtypeStruct((n_tok, d // 2), U32),
            jax.ShapeDtypeStruct((ROUTER_ROWS, n_tok), F32),
        ],
        compiler_params=_cparams(("parallel",)),
        name="outproj_tm" if time_major else "outproj",
    )(x2d, out_a, out_b, w_out, g_ffn, wr_t, br_col)


def _router_kernel(lp_ref, ls_ref, dest_ref, wcol_ref, meta_ref, cnt_scr, base_scr, *, n_prompt_tiles):
    p = pl.program_id(0)
    i = pl.program_id(1)
    tm = lp_ref.shape[1]
    lt = jnp.where(i < n_prompt_tiles, lp_ref[...], ls_ref[...])

    m = lt[0:1, :]
    sel = jnp.zeros((1, tm), I32)
    for r in range(1, N_GROUPS):
        upd = lt[r:r + 1, :] > m
        sel = jnp.where(upd, r, sel)
        m = jnp.where(upd, lt[r:r + 1, :], m)
    den = jnp.zeros((1, tm), F32)
    for r in range(N_GROUPS):
        den = den + jnp.exp(lt[r:r + 1, :] - m)
    g_w = 1.0 / den

    ev = []
    for j in range(EXPERTS_PER_GROUP):
        e = jnp.zeros((1, tm), F32)
        for g in range(N_GROUPS):
            row = N_GROUPS + g * EXPERTS_PER_GROUP + j
            e = jnp.where(sel == g, lt[row:row + 1, :], e)
        ev.append(e)
    v0 = ev[0]
    i0 = jnp.zeros((1, tm), I32)
    for j in range(1, EXPERTS_PER_GROUP):
        upd = ev[j] > v0
        i0 = jnp.where(upd, j, i0)
        v0 = jnp.where(upd, ev[j], v0)
    v1 = jnp.full((1, tm), -jnp.inf, F32)
    i1 = jnp.zeros((1, tm), I32)
    for j in range(EXPERTS_PER_GROUP):
        upd = (ev[j] > v1) & (i0 != j)
        i1 = jnp.where(upd, j, i1)
        v1 = jnp.where(upd, ev[j], v1)
    t = jnp.exp(v1 - v0)
    w0 = g_w / (1.0 + t)
    w1 = g_w * t / (1.0 + t)
    e0 = sel * EXPERTS_PER_GROUP + i0
    e1 = sel * EXPERTS_PER_GROUP + i1

    eio = lax.broadcasted_iota(I32, (N_EXPERTS, tm), 0)
    hit0 = eio == e0
    hit1 = eio == e1
    onehot = (hit0 | hit1).astype(F32)
    tile_cnt = jnp.sum(onehot, axis=1, keepdims=True)

    @pl.when((p == 0) & (i == 0))
    def _():
        cnt_scr[...] = jnp.zeros_like(cnt_scr)

    @pl.when(p == 0)
    def _():
        cnt_scr[...] = cnt_scr[...] + tile_cnt

    @pl.when((p == 1) & (i == 0))
    def _():
        cnt = cnt_scr[...]
        padded = jnp.floor((cnt + (MOE_BLOCK - 1)) * (1.0 / MOE_BLOCK)) * MOE_BLOCK
        ri = lax.broadcasted_iota(I32, (N_EXPERTS, N_EXPERTS), 0)
        ci = lax.broadcasted_iota(I32, (N_EXPERTS, N_EXPERTS), 1)
        lower = (ci < ri).astype(F32)
        pad_start = jnp.dot(lower, jnp.broadcast_to(padded, (N_EXPERTS, LANES)),
                            precision=lax.Precision.HIGHEST, preferred_element_type=F32)[:, 0:1]
        base_scr[...] = pad_start
        pad_end = pad_start + padded
        blk0 = (lax.broadcasted_iota(I32, (N_EXPERTS, LANES), 1) * MOE_BLOCK).astype(F32)
        n_le = jnp.sum((pad_end <= blk0).astype(F32), axis=0, keepdims=True)
        block_e = jnp.minimum(n_le, N_EXPERTS - 1.0)
        n_active = jnp.broadcast_to(pad_end[N_EXPERTS - 1:N_EXPERTS, :] * (1.0 / MOE_BLOCK), (1, LANES))
        diag = (lax.broadcasted_iota(I32, (N_EXPERTS, LANES), 0)
                == lax.broadcasted_iota(I32, (N_EXPERTS, LANES), 1))
        blk_start = jnp.sum(jnp.where(diag, pad_start * (1.0 / MOE_BLOCK), 0.0), axis=0, keepdims=True)
        blk_count = jnp.sum(jnp.where(diag, padded * (1.0 / MOE_BLOCK), 0.0), axis=0, keepdims=True)
        meta_ref[...] = jnp.concatenate(
            [block_e, n_active, blk_start, blk_count, jnp.zeros((4, LANES), F32)], axis=0).astype(I32)

    @pl.when(p == 1)
    def _():
        ui = lax.broadcasted_iota(I32, (tm, tm), 0)
        uj = lax.broadcasted_iota(I32, (tm, tm), 1)
        upper = (ui < uj).astype(BF16)
        excl = _dot(onehot.astype(BF16), upper)
        pos = base_scr[...] + excl
        d0 = jnp.sum(jnp.where(hit0, pos, 0.0), axis=0, keepdims=True)
        d1 = jnp.sum(jnp.where(hit1, pos, 0.0), axis=0, keepdims=True)
        dest_ref[...] = jnp.concatenate([d0, d1, jnp.zeros((6, tm), F32)], axis=0).astype(I32)
        wmat = jnp.concatenate([w0, w1, jnp.zeros((LANES - 2, tm), F32)], axis=0)
        wcol_ref[...] = wmat.T
        base_scr[...] = base_scr[...] + tile_cnt


def _router(lt_p, lt_s):
    tm = TOKEN_TILE
    n_p = lt_p.shape[1] // tm
    n_s = lt_s.shape[1] // tm
    n_tok = lt_p.shape[1] + lt_s.shape[1]
    return pl.pallas_call(
        partial(_router_kernel, n_prompt_tiles=n_p),
        grid=(2, n_p + n_s),
        in_specs=[
            pl.BlockSpec((ROUTER_ROWS, tm), lambda p, i: (0, jnp.minimum(i, n_p - 1))),
            pl.BlockSpec((ROUTER_ROWS, tm), lambda p, i: (0, jnp.maximum(i - n_p, 0))),
        ],
        out_specs=[
            pl.BlockSpec((8, tm), lambda p, i: (0, i * p)),
            pl.BlockSpec((tm, LANES), lambda p, i: (i * p, 0)),
            pl.BlockSpec((8, LANES), lambda p, i: (0, 0)),
        ],
        out_shape=[
            jax.ShapeDtypeStruct((8, n_tok), I32),
            jax.ShapeDtypeStruct((n_tok, LANES), F32),
            jax.ShapeDtypeStruct((8, LANES), I32),
        ],
        scratch_shapes=[pltpu.VMEM((N_EXPERTS, 1), F32), pltpu.VMEM((N_EXPERTS, 1), F32)],
        compiler_params=_cparams(("arbitrary", "arbitrary")),
        name="router",
    )(lt_p, lt_s)


def _dispatch_kernel(d0_ref, d1_ref, bs_ref, nb_ref, na_ref, hp_ref, hs_ref, xb_ref, sem, zbuf, zsem,
                     *, n_prompt_tiles, n_blocks):
    i = pl.program_id(0)
    tm = hp_ref.shape[0]

    @pl.when(i == 0)
    def _():
        zbuf[...] = jnp.zeros_like(zbuf)

        def zero_copy(blk):
            rows = pl.ds(pl.multiple_of(blk * MOE_BLOCK, MOE_BLOCK), MOE_BLOCK)
            return pltpu.make_async_copy(zbuf, xb_ref.at[rows], zsem)

        def last_block(e):
            return bs_ref[e] + jnp.maximum(nb_ref[e], 1) - 1

        def start_e(e, carry):
            @pl.when(nb_ref[e] > 0)
            def _():
                zero_copy(last_block(e)).start()
            return carry

        def wait_e(e, carry):
            @pl.when(nb_ref[e] > 0)
            def _():
                zero_copy(last_block(e)).wait()
            return carry

        def start_b(b, carry):
            zero_copy(b).start()
            return carry

        def wait_b(b, carry):
            zero_copy(b).wait()
            return carry

        lax.fori_loop(0, N_EXPERTS, start_e, 0)
        lax.fori_loop(na_ref[0], n_blocks, start_b, 0)
        lax.fori_loop(0, N_EXPERTS, wait_e, 0)
        lax.fori_loop(na_ref[0], n_blocks, wait_b, 0)

    def scatter(h_ref):
        def copy(r, d_ref):
            return pltpu.make_async_copy(h_ref.at[pl.ds(r, 1)], xb_ref.at[pl.ds(d_ref[i * tm + r], 1)], sem)

        def issue(r, carry):
            copy(r, d0_ref).start()
            copy(r, d1_ref).start()
            return carry

        def drain(r, carry):
            copy(r, d0_ref).wait()
            copy(r, d1_ref).wait()
            return carry

        lax.fori_loop(0, tm, issue, 0, unroll=ROW_DMA_UNROLL)
        lax.fori_loop(0, tm, drain, 0, unroll=ROW_DMA_UNROLL)

    @pl.when(i < n_prompt_tiles)
    def _():
        scatter(hp_ref)

    @pl.when(i >= n_prompt_tiles)
    def _():
        scatter(hs_ref)


def _dispatch(d0, d1, block_start, block_count, n_active, h2_p, h2_s, n_blocks):
    tm = TOKEN_TILE
    d = h2_p.shape[1]
    n_p = h2_p.shape[0] // tm
    n_s = h2_s.shape[0] // tm
    return pl.pallas_call(
        partial(_dispatch_kernel, n_prompt_tiles=n_p, n_blocks=n_blocks),
        grid_spec=pltpu.PrefetchScalarGridSpec(
            num_scalar_prefetch=5,
            grid=(n_p + n_s,),
            in_specs=[
                pl.BlockSpec((tm, d), lambda i, *_: (jnp.minimum(i, n_p - 1), 0)),
                pl.BlockSpec((tm, d), lambda i, *_: (jnp.maximum(i - n_p, 0), 0)),
            ],
            out_specs=pl.BlockSpec(memory_space=pl.ANY),
            scratch_shapes=[
                pltpu.SemaphoreType.DMA(()),
                pltpu.VMEM((MOE_BLOCK, d), U32),
                pltpu.SemaphoreType.DMA(()),
            ],
        ),
        out_shape=jax.ShapeDtypeStruct((n_blocks * MOE_BLOCK, d), U32),
        compiler_params=_cparams(("arbitrary",)),
        name="dispatch",
    )(d0, d1, block_start, block_count, n_active, h2_p, h2_s)


def _ffn_kernel(bs_ref, nb_ref, na_ref, xb_ref, wg_ref, wu_ref, wd_ref, yb_ref,
                xbuf, ybuf, xsem, ysem, *, n_blocks):
    e = pl.program_id(0)
    nb = nb_ref[e]
    b0 = bs_ref[e]
    na = na_ref[0]

    def x_copy(blk, slot):
        rows = pl.ds(pl.multiple_of(blk * MOE_BLOCK, MOE_BLOCK), MOE_BLOCK)
        return pltpu.make_async_copy(xb_ref.at[rows], xbuf.at[slot], xsem.at[slot])

    def y_copy(blk, slot):
        rows = pl.ds(pl.multiple_of(blk * MOE_BLOCK, MOE_BLOCK), MOE_BLOCK)
        return pltpu.make_async_copy(ybuf.at[slot], yb_ref.at[rows], ysem.at[slot])

    @pl.when((e == 0) & (na > 0))
    def _():
        x_copy(0, 0).start()

    def block(i, carry):
        blk = b0 + i
        slot = lax.rem(blk, 2)
        x_copy(blk, slot).wait()

        @pl.when(blk + 1 < na)
        def _():
            x_copy(blk + 1, 1 - slot).start()

        x_lo, x_hi = _unpack_bf16_pairs(xbuf[slot])
        half = x_lo.shape[1]
        g = _dot(x_lo, wg_ref[0, 0:half, :].astype(BF16)) + _dot(x_hi, wg_ref[0, half:, :].astype(BF16))
        u = _dot(x_lo, wu_ref[0, 0:half, :].astype(BF16)) + _dot(x_hi, wu_ref[0, half:, :].astype(BF16))
        a = (g * jax.nn.sigmoid(g) * u).astype(BF16)
        y = _dot(a, wd_ref[0].astype(BF16))

        @pl.when(blk >= 2)
        def _():
            y_copy(blk - 2, slot).wait()

        ybuf[slot] = y
        y_copy(blk, slot).start()
        return carry

    lax.fori_loop(0, nb, block, 0)

    @pl.when(e == pl.num_programs(0) - 1)
    def _():
        @pl.when(na >= 2)
        def _():
            y_copy(na - 2, lax.rem(na, 2)).wait()

        @pl.when(na >= 1)
        def _():
            y_copy(na - 1, lax.rem(na + 1, 2)).wait()

        ybuf[0] = jnp.zeros(ybuf.shape[1:], F32)

        def zero_block(b, carry):
            cp = y_copy(b, 0)
            cp.start()
            cp.wait()
            return carry

        lax.fori_loop(na_ref[0], n_blocks, zero_block, 0)


def _ffn(block_start, block_count, n_active, xb, w_gate, w_up, w_down):
    n_rows, d_packed = xb.shape
    n_exp, d, d_e = w_gate.shape
    n_blocks = n_rows // MOE_BLOCK
    return pl.pallas_call(
        partial(_ffn_kernel, n_blocks=n_blocks),
        grid_spec=pltpu.PrefetchScalarGridSpec(
            num_scalar_prefetch=3,
            grid=(n_exp,),
            in_specs=[
                pl.BlockSpec(memory_space=pl.ANY),
                pl.BlockSpec((1, d, d_e), lambda e, *_: (e, 0, 0)),
                pl.BlockSpec((1, d, d_e), lambda e, *_: (e, 0, 0)),
                pl.BlockSpec((1, d_e, d), lambda e, *_: (e, 0, 0)),
            ],
            out_specs=pl.BlockSpec(memory_space=pl.ANY),
            scratch_shapes=[
                pltpu.VMEM((2, MOE_BLOCK, d_packed), U32),
                pltpu.VMEM((2, MOE_BLOCK, d), F32),
                pltpu.SemaphoreType.DMA((2,)),
                pltpu.SemaphoreType.DMA((2,)),
            ],
        ),
        out_shape=jax.ShapeDtypeStruct((n_rows, d), F32),
        compiler_params=_cparams(("arbitrary",)),
        name="moe_ffn",
    )(block_start, block_count, n_active, xb, w_gate, w_up, w_down)


def _ple_kernel(d0_ref, d1_ref, x1_ref, wcol_ref, p_ref, yb_ref, gple_ref, wpg_ref, wp_ref, gfin_ref,
                y_ref, g_scr, sem, *, time_major, n_steps):
    i = pl.program_id(0)
    tm = x1_ref.shape[0]
    slot = lax.rem(i, 2) if n_steps > 1 else 0

    def copy(tile, r, k, to_slot):
        d_ref = d0_ref if k == 0 else d1_ref
        return pltpu.make_async_copy(yb_ref.at[pl.ds(d_ref[tile * tm + r], 1)],
                                     g_scr.at[to_slot, k, pl.ds(r, 1)], sem.at[to_slot])

    def issue(tile, to_slot):
        def body(r, carry):
            copy(tile, r, 0, to_slot).start()
            copy(tile, r, 1, to_slot).start()
            return carry
        lax.fori_loop(0, tm, body, 0, unroll=ROW_DMA_UNROLL)

    def drain(tile, from_slot):
        def body(r, carry):
            copy(tile, r, 0, from_slot).wait()
            copy(tile, r, 1, from_slot).wait()
            return carry
        lax.fori_loop(0, tm, body, 0, unroll=ROW_DMA_UNROLL)

    @pl.when(i == 0)
    def _():
        issue(0, 0)

    if n_steps > 1:
        @pl.when(i + 1 < n_steps)
        def _():
            issue(i + 1, 1 - slot)

    drain(i, slot)

    x2 = x1_ref[...] + wcol_ref[:, 0:1] * g_scr[slot, 0] + wcol_ref[:, 1:2] * g_scr[slot, 1]
    hn = _rms(x2, gple_ref[...]).astype(BF16)
    gate = jax.nn.sigmoid(_dot(hn, wpg_ref[...]))
    if time_major:
        pd = wp_ref.shape[0]
        pp = jnp.concatenate([p_ref[:, t * pd:(t + 1) * pd] for t in range(p_ref.shape[1] // pd)], axis=0)
    else:
        pp = p_ref[...]
    x3 = x2 + _dot(pp.astype(BF16), wp_ref[...]) * gate
    y = _rms(x3, gfin_ref[...])
    if time_major:
        d = y.shape[1]
        rows = y_ref.shape[0]
        for t in range(y_ref.shape[1] // d):
            y_ref[:, t * d:(t + 1) * d] = y[t * rows:(t + 1) * rows, :]
    else:
        y_ref[...] = y


def _ple(d0, d1, x1, wcol, p2d, yb, g_ple, w_pg, w_p, g_final, *, time_major):
    n_tok, d = x1.shape
    tm = n_tok if time_major else PLE_TILE
    n_steps = n_tok // tm
    pd = w_p.shape[0]
    if time_major:
        n_seq = p2d.shape[0]
        p_spec = pl.BlockSpec(p2d.shape, lambda i, *_: (0, 0))
        y_spec = pl.BlockSpec((n_seq, (n_tok // n_seq) * d), lambda i, *_: (0, 0))
        y_shape = jax.ShapeDtypeStruct((n_seq, (n_tok // n_seq) * d), F32)
    else:
        p_spec = pl.BlockSpec((tm, pd), lambda i, *_: (i, 0))
        y_spec = pl.BlockSpec((tm, d), lambda i, *_: (i, 0))
        y_shape = jax.ShapeDtypeStruct((n_tok, d), F32)
    return pl.pallas_call(
        partial(_ple_kernel, time_major=time_major, n_steps=n_steps),
        grid_spec=pltpu.PrefetchScalarGridSpec(
            num_scalar_prefetch=2,
            grid=(n_steps,),
            in_specs=[
                pl.BlockSpec((tm, d), lambda i, *_: (i, 0)),
                pl.BlockSpec((tm, LANES), lambda i, *_: (i, 0)),
                p_spec,
                pl.BlockSpec(memory_space=pl.ANY),
                pl.BlockSpec((1, d), lambda i, *_: (0, 0)),
                pl.BlockSpec(w_pg.shape, lambda i, *_: (0, 0)),
                pl.BlockSpec(w_p.shape, lambda i, *_: (0, 0)),
                pl.BlockSpec((1, d), lambda i, *_: (0, 0)),
            ],
            out_specs=y_spec,
            scratch_shapes=[
                pltpu.VMEM((min(n_steps, 2), 2, tm, d), F32),
                pltpu.SemaphoreType.DMA((min(n_steps, 2),)),
            ],
        ),
        out_shape=y_shape,
        compiler_params=_cparams(("arbitrary",)),
        name="ple_tm" if time_major else "ple",
    )(d0, d1, x1, wcol, p2d, yb, g_ple, w_pg, w_p, g_final)


def _pad_lanes(v, offset=0):
    row = jnp.zeros((1, LANES), F32)
    return row.at[0, offset:offset + v.shape[0]].set(v.astype(F32))


def kernel(x_prompt, x_sample, state_conv, state_delta, p_prompt, p_sample, g_mix, w_in, w_s, b_s, g_v,
           conv_w, a_log, dt_bias, g_out, w_out, g_ffn, w_group, b_group, w_router, b_router, w_gate, w_up,
           w_down, g_ple, w_ple_gate, w_ple, g_final):
    n_batch, seq, d = x_prompt.shape
    n_seq, n_pos, _ = x_sample.shape
    depth = g_mix.shape[0]
    assert depth == 1
    a_width = A_GROUPS * A_GROUP_DIM
    hd = B_HEADS * B_DIM
    n_main = 2 * a_width + 4 * hd
    n_p = n_batch * seq
    n_s = n_seq * n_pos

    xp = x_prompt.reshape(n_p, d)
    xs = x_sample.reshape(n_seq, n_pos * d)
    l = 0

    w_main, w_tail = _cast_bf16(w_in, n_main)
    gm = g_mix[l].reshape(1, d)
    causal = jnp.tril(jnp.ones((A_CHUNK, A_CHUNK), dtype=bool))
    ws_tril = jnp.where(causal, w_s[l], 0).astype(BF16)
    b_t = b_s[l].T
    gv = g_v[l].reshape(1, a_width)
    alog_row = _pad_lanes(a_log[l])
    dtb_row = _pad_lanes(dt_bias[l])
    gout = g_out[l].reshape(1, B_DIM)
    w_o = w_out[l].astype(BF16)
    gf = g_ffn[l].reshape(1, d)
    wr = jnp.pad(jnp.concatenate([w_group[l], w_router[l]], axis=1),
                 ((0, 0), (0, LANES - N_GROUPS - N_EXPERTS)))
    wr_hi = wr.astype(BF16)
    wr_t = jnp.concatenate([wr_hi, (wr - wr_hi.astype(F32)).astype(BF16)], axis=1)
    br_col = jnp.broadcast_to(
        jnp.pad(jnp.concatenate([b_group[l], b_router[l]]), (0, ROUTER_ROWS - N_GROUPS - N_EXPERTS))[:, None],
        (ROUTER_ROWS, LANES))
    gp = g_ple[l].reshape(1, d)
    w_pg = w_ple_gate[l].astype(BF16)
    w_p = w_ple[l].astype(BF16)
    gfin = g_final.reshape(1, d)

    proj_p, tail_p = _inproj(xp, gm, w_main, w_tail, n_main, time_major=False)
    proj_s, tail_s = _inproj(xs, gm, w_main, w_tail, n_main, time_major=True)
    oa_p = _mixa_prompt(proj_p, ws_tril, b_t, gv, n_p)
    oa_s, va_s = _mixa_sample(proj_s, w_s[l], b_s[l], gv, n_seq, n_pos)
    ob_p, sd_p, cb_p = _gdn_prompt(proj_p, tail_p, conv_w[l], alog_row, dtb_row, gout, n_batch, seq)
    cbuf = state_conv[l].reshape(n_seq, (CONV_W - 1) * 3 * hd)
    ob_s, sd_s, cb_s = _gdn_decode(proj_s, tail_s, cbuf, state_delta[l], conv_w[l], alog_row, dtb_row,
                                   gout, n_seq, n_pos)
    ob_s = ob_s.reshape(n_s, hd)

    x1_p, h2_p, lt_p = _outproj(xp, oa_p, ob_p, w_o, gf, wr_t, br_col, time_major=False)
    x1_s, h2_s, lt_s = _outproj(xs, oa_s, ob_s, w_o, gf, wr_t, br_col, time_major=True)
    dest, wcol, meta = _router(lt_p, lt_s)
    d0, d1 = dest[0], dest[1]
    n_active = meta[1, 0:1]
    block_start = meta[2, :N_EXPERTS]
    block_count = meta[3, :N_EXPERTS]

    n_tok = n_p + n_s
    n_blocks = (n_tok * 2) // MOE_BLOCK + N_EXPERTS
    xb = _dispatch(d0, d1, block_start, block_count, n_active, h2_p, h2_s, n_blocks)
    yb = _ffn(block_start, block_count, n_active, xb, w_gate[l], w_up[l], w_down[l])

    pp = p_prompt[l].reshape(n_p, -1)
    ps = p_sample[l].reshape(n_seq, -1)
    y_p = _ple(d0[:n_p], d1[:n_p], x1_p, wcol[:n_p], pp, yb, gp, w_pg, w_p, gfin, time_major=False)
    y_s = _ple(d0[n_p:], d1[n_p:], x1_s, wcol[n_p:], ps, yb, gp, w_pg, w_p, gfin, time_major=True)

    return (
        y_p.reshape(n_batch, seq, d),
        y_s.reshape(n_seq, n_pos, d),
        cb_p[None],
        sd_p[None],
        cb_s.reshape(n_seq, CONV_W - 1, 3 * hd)[None],
        sd_s[None],
        va_s.reshape(n_seq, n_pos, a_width)[None],
    )
```

```python
from functools import partial

import jax
import jax.numpy as jnp
from jax import lax
from jax.experimental import pallas as pl
from jax.experimental.pallas import tpu as pltpu

F32 = jnp.float32
BF16 = jnp.bfloat16
I32 = jnp.int32
U32 = jnp.uint32

EPS = 1e-6
LANES = 128
A_GROUPS = 8
A_GROUP_DIM = 128
A_CHUNK = 128
B_HEADS = 8
B_DIM = 128
CONV_W = 4
N_GROUPS = 4
EXPERTS_PER_GROUP = 8
N_EXPERTS = N_GROUPS * EXPERTS_PER_GROUP
ROUTER_ROWS = 40
TOKEN_TILE = 512
INPROJ_TILE = 1024
PLE_TILE = 256
GDN_CHUNK = 64
MOE_BLOCK = 128
FFN_RING = 4
DEC_SEQ_TILE = 8
ROW_DMA_UNROLL = 8
VMEM_LIMIT = 56 * 1024 * 1024


def _cparams(sem):
    return pltpu.CompilerParams(dimension_semantics=sem, vmem_limit_bytes=VMEM_LIMIT)


def _rms(x, g):
    return x * lax.rsqrt(jnp.mean(x * x, axis=-1, keepdims=True) + EPS) * g


def _softplus(x):
    return jnp.maximum(x, 0.0) + jnp.log(1.0 + jnp.exp(-jnp.abs(x)))


def _dot(a, b):
    return jnp.dot(a, b, preferred_element_type=F32)


def _dot_nt(a, b, precision=None):
    return lax.dot_general(a, b, (((1,), (1,)), ((), ())), precision=precision,
                           preferred_element_type=F32)


def _dot_tn(a, b):
    return lax.dot_general(a, b, (((0,), (0,)), ((), ())), preferred_element_type=F32)


def _pack_bf16_pairs(x):
    n = x.shape[1] // 2
    lo = lax.bitcast_convert_type(x[:, :n].astype(BF16).astype(F32), U32) >> 16
    hi = lax.bitcast_convert_type(x[:, n:].astype(BF16).astype(F32), U32) & jnp.uint32(0xFFFF0000)
    return hi | lo


def _unpack_bf16_pairs(xu):
    lo = lax.bitcast_convert_type(xu << 16, F32).astype(BF16)
    hi = lax.bitcast_convert_type(xu & jnp.uint32(0xFFFF0000), F32).astype(BF16)
    return lo, hi


def _inproj_kernel(x_ref, g_ref, w_ref, wt_ref, proj_ref, tail_ref, h_scr, *, time_major):
    j = pl.program_id(1)

    @pl.when(j == 0)
    def _():
        g = g_ref[...]
        if time_major:
            d = g.shape[-1]
            rows = x_ref.shape[0]
            for t in range(x_ref.shape[1] // d):
                h_scr[t * rows:(t + 1) * rows, :] = _rms(x_ref[:, t * d:(t + 1) * d], g).astype(BF16)
        else:
            h_scr[...] = _rms(x_ref[...], g).astype(BF16)
        tail_ref[...] = _dot(h_scr[...], wt_ref[...])

    proj_ref[...] = _dot(h_scr[...], w_ref[...])


def _inproj(x2d, g_mix, w_main, w_tail, n_main, *, time_major, col_tile=1024):
    d = g_mix.shape[-1]
    if time_major:
        n_tok = x2d.shape[0] * (x2d.shape[1] // d)
        tm = n_tok
        x_spec = pl.BlockSpec(x2d.shape, lambda i, j: (0, 0))
    else:
        n_tok = x2d.shape[0]
        tm = min(INPROJ_TILE, n_tok)
        x_spec = pl.BlockSpec((tm, d), lambda i, j: (i, 0))
    grid = (n_tok // tm, n_main // col_tile)
    return pl.pallas_call(
        partial(_inproj_kernel, time_major=time_major),
        grid=grid,
        in_specs=[
            x_spec,
            pl.BlockSpec((1, d), lambda i, j: (0, 0)),
            pl.BlockSpec((d, col_tile), lambda i, j: (0, j)),
            pl.BlockSpec((d, LANES), lambda i, j: (0, 0)),
        ],
        out_specs=[
            pl.BlockSpec((tm, col_tile), lambda i, j: (i, j)),
            pl.BlockSpec((tm, LANES), lambda i, j: (i, 0)),
        ],
        out_shape=[
            jax.ShapeDtypeStruct((n_tok, n_main), F32),
            jax.ShapeDtypeStruct((n_tok, LANES), F32),
        ],
        scratch_shapes=[pltpu.VMEM((tm, d), BF16)],
        compiler_params=_cparams(("parallel", "arbitrary")),
        name="inproj_tm" if time_major else "inproj",
    )(x2d, g_mix, w_main, w_tail)


def _group_ln(v, g):
    mu = jnp.mean(v, axis=-1, keepdims=True)
    dlt = v - mu
    var = jnp.mean(dlt * dlt, axis=-1, keepdims=True)
    return dlt * lax.rsqrt(var + EPS) * g


def _mixa_prompt_kernel(au_ref, av_ref, ws_ref, bt_ref, gv_ref, out_ref):
    n_chunks = au_ref.shape[0] // A_CHUNK
    for c in range(n_chunks):
        rows = slice(c * A_CHUNK, (c + 1) * A_CHUNK)
        for h in range(A_GROUPS):
            cols = slice(h * A_GROUP_DIM, (h + 1) * A_GROUP_DIM)
            u = jax.nn.gelu(au_ref[rows, cols])
            v = _group_ln(jax.nn.gelu(av_ref[rows, cols]), gv_ref[:, cols])
            mixed = _dot(ws_ref[h], v.astype(BF16)) + bt_ref[:, h:h + 1]
            out_ref[rows, cols] = (u * mixed).astype(BF16)


def _mixa_prompt(proj, ws_tril, b_t, g_v, n_tok):
    a_width = A_GROUPS * A_GROUP_DIM
    tm = TOKEN_TILE
    return pl.pallas_call(
        _mixa_prompt_kernel,
        grid=(n_tok // tm,),
        in_specs=[
            pl.BlockSpec((tm, a_width), lambda i: (i, 0)),
            pl.BlockSpec((tm, a_width), lambda i: (i, 1)),
            pl.BlockSpec((A_GROUPS, A_CHUNK, A_CHUNK), lambda i: (0, 0, 0)),
            pl.BlockSpec((A_CHUNK, A_GROUPS), lambda i: (0, 0)),
            pl.BlockSpec((1, a_width), lambda i: (0, 0)),
        ],
        out_specs=pl.BlockSpec((tm, a_width), lambda i: (i, 0)),
        out_shape=jax.ShapeDtypeStruct((n_tok, a_width), BF16),
        compiler_params=_cparams(("parallel",)),
        name="mixa_prompt",
    )(proj, proj, ws_tril, b_t, g_v)


def _mixa_sample_kernel(ws_ref, bs_ref, au_ref, av_ref, gv_ref, out_ref, va_ref, *, n_seq, n_pos):
    a_width = A_GROUPS * A_GROUP_DIM
    vs = []
    for t in range(n_pos):
        rows = slice(t * n_seq, (t + 1) * n_seq)
        per_group = []
        for h in range(A_GROUPS):
            cols = slice(h * A_GROUP_DIM, (h + 1) * A_GROUP_DIM)
            per_group.append(_group_ln(jax.nn.gelu(av_ref[rows, cols]), gv_ref[:, cols]))
        vs.append(per_group)
        for h in range(A_GROUPS):
            va_ref[:, t * a_width + h * A_GROUP_DIM:t * a_width + (h + 1) * A_GROUP_DIM] = per_group[h]
    for t in range(n_pos):
        rows = slice(t * n_seq, (t + 1) * n_seq)
        for h in range(A_GROUPS):
            cols = slice(h * A_GROUP_DIM, (h + 1) * A_GROUP_DIM)
            mixed = jnp.zeros((n_seq, A_GROUP_DIM), F32) + bs_ref[h * n_pos + t]
            for s in range(t + 1):
                mixed = mixed + ws_ref[(h * n_pos + t) * n_pos + s] * vs[s][h]
            out_ref[rows, cols] = (jax.nn.gelu(au_ref[rows, cols]) * mixed).astype(BF16)


def _mixa_sample(proj, w_s, b_s, g_v, n_seq, n_pos):
    a_width = A_GROUPS * A_GROUP_DIM
    n_tok = n_seq * n_pos
    return pl.pallas_call(
        partial(_mixa_sample_kernel, n_seq=n_seq, n_pos=n_pos),
        grid_spec=pltpu.PrefetchScalarGridSpec(
            num_scalar_prefetch=2,
            grid=(1,),
            in_specs=[
                pl.BlockSpec((n_tok, a_width), lambda i, *_: (0, 0)),
                pl.BlockSpec((n_tok, a_width), lambda i, *_: (0, 1)),
                pl.BlockSpec((1, a_width), lambda i, *_: (0, 0)),
            ],
            out_specs=[
                pl.BlockSpec((n_tok, a_width), lambda i, *_: (0, 0)),
                pl.BlockSpec((n_seq, n_pos * a_width), lambda i, *_: (0, 0)),
            ],
        ),
        out_shape=[
            jax.ShapeDtypeStruct((n_tok, a_width), BF16),
            jax.ShapeDtypeStruct((n_seq, n_pos * a_width), F32),
        ],
        compiler_params=_cparams(("arbitrary",)),
        name="mixa_sample",
    )(w_s[:, :n_pos, :n_pos].reshape(-1), b_s[:, :n_pos].reshape(-1), proj, proj, g_v)


def _gdn_prompt_kernel(q_ref, k_ref, v_ref, z_ref, tail_ref, cw_ref, alog_ref, dtb_ref, gout_ref,
                       out_ref, s_out_ref, conv_out_ref, s_scr, xbuf):
    c = pl.program_id(1)
    n_c = pl.num_programs(1)
    C = q_ref.shape[0]
    hd = B_HEADS * B_DIM
    pad = 8
    lo = pad - (CONV_W - 1)

    @pl.when(c == 0)
    def _():
        s_scr[...] = jnp.zeros_like(s_scr)
        xbuf[0:pad, :] = jnp.zeros((pad, 3 * hd), F32)

    xbuf[pad:pad + C, 0:hd] = q_ref[...]
    xbuf[pad:pad + C, hd:2 * hd] = k_ref[...]
    xbuf[pad:pad + C, 2 * hd:3 * hd] = v_ref[...]
    y = xbuf[lo:lo + C, :] * cw_ref[0:1, :]
    for j in range(1, CONV_W):
        y = y + xbuf[lo + j:lo + j + C, :] * cw_ref[j:j + 1, :]
    y = y * jax.nn.sigmoid(y)
    last = xbuf[pad + C - (CONV_W - 1):pad + C, :]
    xbuf[lo:pad, :] = last

    @pl.when(c == n_c - 1)
    def _():
        conv_out_ref[0] = last

    tail = tail_ref[...]
    g_all = -jnp.exp(alog_ref[...]) * _softplus(tail + dtb_ref[...])
    beta_all = jax.nn.sigmoid(tail)
    ri = lax.broadcasted_iota(I32, (C, C), 0)
    ci = lax.broadcasted_iota(I32, (C, C), 1)
    causal = ci <= ri
    strict = ci < ri
    tril = causal.astype(F32)
    G_all = jnp.dot(tril, g_all, precision=lax.Precision.HIGHEST, preferred_element_type=F32)
    G_rows = jnp.concatenate([G_all, jnp.zeros((LANES - C, LANES), F32)], axis=0).T if C < LANES else G_all.T
    scale = B_DIM ** -0.5
    n_levels = max(C.bit_length() - 2, 0)
    heads = range(B_HEADS)

    qs, ks, Gs, eGs, rhss, qks, Ps = [], [], [], [], [], [], []
    for h in heads:
        q = y[:, h * B_DIM:(h + 1) * B_DIM]
        k = y[:, hd + h * B_DIM:hd + (h + 1) * B_DIM]
        v = y[:, 2 * hd + h * B_DIM:2 * hd + (h + 1) * B_DIM]
        q = q * lax.rsqrt(jnp.sum(q * q, axis=-1, keepdims=True) + EPS) * scale
        k = k * lax.rsqrt(jnp.sum(k * k, axis=-1, keepdims=True) + EPS)
        G = G_all[:, h:h + 1]
        beta = beta_all[:, B_HEADS + h:B_HEADS + h + 1]
        eG = jnp.exp(G)
        decay = jnp.exp(jnp.where(causal, G - G_rows[h:h + 1, 0:C], -jnp.inf))
        qk_kk = _dot_nt(jnp.concatenate([q, k], axis=0), k)
        qks.append(qk_kk[:C] * decay)
        Ps.append(jnp.where(strict, -(beta * qk_kk[C:] * decay), 0.0))
        rhss.append(jnp.concatenate([v * beta, k * (beta * eG)], axis=1))
        qs.append(q)
        ks.append(k)
        Gs.append(G)
        eGs.append(eG)

    Ls = list(Ps)
    if n_levels > 0:
        Ps = [_dot(P, P) for P in Ps]
    for lvl in range(1, n_levels + 1):
        for h in heads:
            if lvl < n_levels:
                R = _dot(jnp.concatenate([Ps[h], Ls[h]], axis=0), Ps[h])
                Ls[h] = Ls[h] + Ps[h] + R[C:]
                Ps[h] = R[:C]
            else:
                Ls[h] = Ls[h] + Ps[h] + _dot(Ls[h], Ps[h])

    uws = [rhss[h] + _dot(Ls[h], rhss[h]) for h in heads]
    Ss = [s_scr[h] for h in heads]
    rs = [_dot(jnp.concatenate([uws[h][:, B_DIM:], qs[h] * eGs[h]], axis=0), Ss[h]) for h in heads]
    v_news = [uws[h][:, :B_DIM] - rs[h][:C] for h in heads]
    os_ = [rs[h][C:] + _dot(qks[h], v_news[h]) for h in heads]
    for h in heads:
        G_last = Gs[h][C - 1:C, :]
        kd = ks[h] * jnp.exp(G_last - Gs[h])
        s_scr[h] = Ss[h] * jnp.exp(G_last) + _dot_tn(kd, v_news[h])
    for h in heads:
        cols = slice(h * B_DIM, (h + 1) * B_DIM)
        o = _rms(os_[h], gout_ref[...])
        z = z_ref[:, cols]
        out_ref[:, cols] = (o * (z * jax.nn.sigmoid(z))).astype(BF16)

    @pl.when(c == n_c - 1)
    def _():
        s_out_ref[0] = s_scr[...]


def _gdn_prompt(proj, tail, conv_w, alog_row, dtb_row, g_out, n_batch, seq):
    C = GDN_CHUNK
    n_c = seq // C
    hd = B_HEADS * B_DIM
    row = lambda b, c: b * n_c + c
    return pl.pallas_call(
        _gdn_prompt_kernel,
        grid=(n_batch, n_c),
        in_specs=[
            pl.BlockSpec((C, hd), lambda b, c: (row(b, c), 2)),
            pl.BlockSpec((C, hd), lambda b, c: (row(b, c), 3)),
            pl.BlockSpec((C, hd), lambda b, c: (row(b, c), 4)),
            pl.BlockSpec((C, hd), lambda b, c: (row(b, c), 5)),
            pl.BlockSpec((C, LANES), lambda b, c: (row(b, c), 0)),
            pl.BlockSpec((CONV_W, 3 * hd), lambda b, c: (0, 0)),
            pl.BlockSpec((1, LANES), lambda b, c: (0, 0)),
            pl.BlockSpec((1, LANES), lambda b, c: (0, 0)),
            pl.BlockSpec((1, B_DIM), lambda b, c: (0, 0)),
        ],
        out_specs=[
            pl.BlockSpec((C, hd), lambda b, c: (row(b, c), 0)),
            pl.BlockSpec((1, B_HEADS, B_DIM, B_DIM), lambda b, c: (b, 0, 0, 0)),
            pl.BlockSpec((1, CONV_W - 1, 3 * hd), lambda b, c: (b, 0, 0)),
        ],
        out_shape=[
            jax.ShapeDtypeStruct((n_batch * seq, hd), BF16),
            jax.ShapeDtypeStruct((n_batch, B_HEADS, B_DIM, B_DIM), F32),
            jax.ShapeDtypeStruct((n_batch, CONV_W - 1, 3 * hd), F32),
        ],
        scratch_shapes=[
            pltpu.VMEM((B_HEADS, B_DIM, B_DIM), F32),
            pltpu.VMEM((8 + C, 3 * hd), F32),
        ],
        compiler_params=_cparams(("parallel", "arbitrary")),
        name="gdn_prompt",
    )(proj, proj, proj, proj, tail, conv_w, alog_row, dtb_row, g_out)


def _gdn_decode_kernel(*refs, n_pos):
    proj_refs = refs[:n_pos]
    tail_refs = refs[n_pos:2 * n_pos]
    (cbuf_ref, s_ref, cw_ref, alog_ref, dtb_ref, gout_ref,
     out_ref, s_out_ref, conv_out_ref, lhs_scr, res_scr, kd_scr, vn_scr, gl_scr) = refs[2 * n_pos:]
    SB = s_ref.shape[0]
    hd = B_HEADS * B_DIM
    qkv0 = 2 * hd
    z0 = qkv0 + 3 * hd
    scale = B_DIM ** -0.5

    xp = [cbuf_ref[:, j * 3 * hd:(j + 1) * 3 * hd] for j in range(CONV_W - 1)]
    xp += [proj_refs[t][:, qkv0:qkv0 + 3 * hd] for t in range(n_pos)]
    for j in range(CONV_W - 1):
        conv_out_ref[:, j * 3 * hd:(j + 1) * 3 * hd] = xp[n_pos + j]
    ys = []
    for t in range(n_pos):
        y = xp[t] * cw_ref[0:1, :]
        for j in range(1, CONV_W):
            y = y + xp[t + j] * cw_ref[j:j + 1, :]
        ys.append(y * jax.nn.sigmoid(y))
    gs, betas = [], []
    for t in range(n_pos):
        tail = tail_refs[t][...]
        gs.append(-jnp.exp(alog_ref[...]) * _softplus(tail + dtb_ref[...]))
        betas.append(jax.nn.sigmoid(tail))

    kd_scr[...] = jnp.zeros_like(kd_scr)
    vn_scr[...] = jnp.zeros_like(vn_scr)
    heads = range(B_HEADS)
    stride = 2 * n_pos

    qs, ks, Gs, us = [], [], [], []
    for h in heads:
        q_h, k_h, v_h, G_h, b_h = [], [], [], [], []
        G = None
        for t in range(n_pos):
            q = ys[t][:, h * B_DIM:(h + 1) * B_DIM]
            k = ys[t][:, hd + h * B_DIM:hd + (h + 1) * B_DIM]
            q_h.append(q * lax.rsqrt(jnp.sum(q * q, axis=-1, keepdims=True) + EPS) * scale)
            k_h.append(k * lax.rsqrt(jnp.sum(k * k, axis=-1, keepdims=True) + EPS))
            v_h.append(ys[t][:, 2 * hd + h * B_DIM:2 * hd + (h + 1) * B_DIM])
            g = gs[t][:, h:h + 1]
            G = g if G is None else G + g
            G_h.append(G)
            b_h.append(betas[t][:, B_HEADS + h:B_HEADS + h + 1])
        u_h, w_h = [], []
        for t in range(n_pos):
            u = v_h[t] * b_h[t]
            w = k_h[t] * (b_h[t] * jnp.exp(G_h[t]))
            for s in range(t):
                a = b_h[t] * jnp.sum(k_h[t] * k_h[s], axis=-1, keepdims=True) * jnp.exp(G_h[t] - G_h[s])
                u = u - a * u_h[s]
                w = w - a * w_h[s]
            u_h.append(u)
            w_h.append(w)
        for t in range(n_pos):
            lhs_scr[h, pl.ds(t, SB, stride=stride), :] = w_h[t]
            lhs_scr[h, pl.ds(n_pos + t, SB, stride=stride), :] = q_h[t] * jnp.exp(G_h[t])
        qs.append(q_h)
        ks.append(k_h)
        Gs.append(G_h)
        us.append(u_h)

    for h in heads:
        for s in range(SB):
            rows = slice(s * stride, (s + 1) * stride)
            res_scr[h, rows, :] = _dot(lhs_scr[h, rows, :], s_ref[s, h])

    for h in heads:
        cols = slice(h * B_DIM, (h + 1) * B_DIM)
        v_news = [us[h][t] - res_scr[h, pl.ds(t, SB, stride=stride), :] for t in range(n_pos)]
        G_last = Gs[h][n_pos - 1]
        for t in range(n_pos):
            o = res_scr[h, pl.ds(n_pos + t, SB, stride=stride), :]
            for s in range(t + 1):
                qk = jnp.sum(qs[h][t] * ks[h][s], axis=-1, keepdims=True) * jnp.exp(Gs[h][t] - Gs[h][s])
                o = o + qk * v_news[s]
            o = _rms(o, gout_ref[...])
            z = proj_refs[t][:, z0 + h * B_DIM:z0 + (h + 1) * B_DIM]
            out_ref[t, :, cols] = (o * (z * jax.nn.sigmoid(z))).astype(BF16)
            kd_scr[h, pl.ds(t, SB, stride=stride), :] = ks[h][t] * jnp.exp(G_last - Gs[h][t])
            vn_scr[h, pl.ds(t, SB, stride=stride), :] = v_news[t]
        gl_scr[h] = jnp.broadcast_to(jnp.exp(G_last), (SB, B_DIM))

    for h in heads:
        for s in range(SB):
            rows = slice(s * stride, (s + 1) * stride)
            s_out_ref[s, h] = (s_ref[s, h] * gl_scr[h, s:s + 1, 0:1]
                               + _dot_tn(kd_scr[h, rows, :], vn_scr[h, rows, :]))


def _gdn_decode(proj_s, tail_s, cbuf, s0, conv_w, alog_row, dtb_row, g_out, n_seq, n_pos):
    SB = DEC_SEQ_TILE
    hd = B_HEADS * B_DIM
    n_main = proj_s.shape[1]
    per_pos = n_seq // SB
    proj_specs = [pl.BlockSpec((SB, n_main), lambda i, t=t: (t * per_pos + i, 0)) for t in range(n_pos)]
    tail_specs = [pl.BlockSpec((SB, LANES), lambda i, t=t: (t * per_pos + i, 0)) for t in range(n_pos)]
    return pl.pallas_call(
        partial(_gdn_decode_kernel, n_pos=n_pos),
        grid=(per_pos,),
        in_specs=proj_specs + tail_specs + [
            pl.BlockSpec((SB, (CONV_W - 1) * 3 * hd), lambda i: (i, 0)),
            pl.BlockSpec((SB, B_HEADS, B_DIM, B_DIM), lambda i: (i, 0, 0, 0)),
            pl.BlockSpec((CONV_W, 3 * hd), lambda i: (0, 0)),
            pl.BlockSpec((1, LANES), lambda i: (0, 0)),
            pl.BlockSpec((1, LANES), lambda i: (0, 0)),
            pl.BlockSpec((1, B_DIM), lambda i: (0, 0)),
        ],
        out_specs=[
            pl.BlockSpec((n_pos, SB, hd), lambda i: (0, i, 0)),
            pl.BlockSpec((SB, B_HEADS, B_DIM, B_DIM), lambda i: (i, 0, 0, 0)),
            pl.BlockSpec((SB, (CONV_W - 1) * 3 * hd), lambda i: (i, 0)),
        ],
        out_shape=[
            jax.ShapeDtypeStruct((n_pos, n_seq, hd), BF16),
            jax.ShapeDtypeStruct((n_seq, B_HEADS, B_DIM, B_DIM), F32),
            jax.ShapeDtypeStruct((n_seq, (CONV_W - 1) * 3 * hd), F32),
        ],
        scratch_shapes=[
            pltpu.VMEM((B_HEADS, SB * 2 * n_pos, B_DIM), F32),
            pltpu.VMEM((B_HEADS, SB * 2 * n_pos, B_DIM), F32),
            pltpu.VMEM((B_HEADS, SB * 2 * n_pos, B_DIM), F32),
            pltpu.VMEM((B_HEADS, SB * 2 * n_pos, B_DIM), F32),
            pltpu.VMEM((B_HEADS, SB, B_DIM), F32),
        ],
        compiler_params=_cparams(("parallel",)),
        name="gdn_decode",
    )(*([proj_s] * n_pos), *([tail_s] * n_pos), cbuf, s0, conv_w, alog_row, dtb_row, g_out)


def _outproj_kernel(x_ref, oa_ref, ob_ref, w_ref, g_ref, wr_ref, br_ref,
                    x1_ref, h2_ref, lt_ref, *, time_major):
    a_width = oa_ref.shape[1]
    acc = _dot(oa_ref[...], w_ref[0:a_width, :]) + _dot(ob_ref[...], w_ref[a_width:, :])
    if time_major:
        d = g_ref.shape[-1]
        rows = x_ref.shape[0]
        x = jnp.concatenate([x_ref[:, t * d:(t + 1) * d] for t in range(x_ref.shape[1] // d)], axis=0)
    else:
        x = x_ref[...]
    x1 = x + acc
    x1_ref[...] = x1
    h2 = _rms(x1, g_ref[...])
    h2_ref[...] = _pack_bf16_pairs(h2)
    h_hi = h2.astype(BF16)
    h_lo = (h2 - h_hi.astype(F32)).astype(BF16)
    tm = h2.shape[0]
    lg = _dot(jnp.concatenate([h_hi, h_lo], axis=0), wr_ref[...])
    lg = lg[:tm, :LANES] + lg[:tm, LANES:] + lg[tm:, :LANES] + lg[tm:, LANES:]
    lt_ref[...] = lg.T[0:ROUTER_ROWS, :] + br_ref[:, 0:1]


def _outproj(x2d, out_a, out_b, w_out, g_ffn, wr_t, br_col, *, time_major):
    d = g_ffn.shape[-1]
    a_width = out_a.shape[1]
    n_tok = out_a.shape[0]
    tm = TOKEN_TILE
    if time_major:
        x_spec = pl.BlockSpec(x2d.shape, lambda i: (0, 0))
    else:
        x_spec = pl.BlockSpec((tm, d), lambda i: (i, 0))
    return pl.pallas_call(
        partial(_outproj_kernel, time_major=time_major),
        grid=(n_tok // tm,),
        in_specs=[
            x_spec,
            pl.BlockSpec((tm, a_width), lambda i: (i, 0)),
            pl.BlockSpec((tm, out_b.shape[1]), lambda i: (i, 0)),
            pl.BlockSpec(w_out.shape, lambda i: (0, 0)),
            pl.BlockSpec((1, d), lambda i: (0, 0)),
            pl.BlockSpec(wr_t.shape, lambda i: (0, 0)),
            pl.BlockSpec(br_col.shape, lambda i: (0, 0)),
        ],
        out_specs=[
            pl.BlockSpec((tm, d), lambda i: (i, 0)),
            pl.BlockSpec((tm, d // 2), lambda i: (i, 0)),
            pl.BlockSpec((ROUTER_ROWS, tm), lambda i: (0, i)),
        ],
        out_shape=[
            jax.ShapeDtypeStruct((n_tok, d), F32),
            jax.ShapeDtypeStruct((n_tok, d // 2), U32),
            jax.ShapeDtypeStruct((ROUTER_ROWS, n_tok), F32),
        ],
        compiler_params=_cparams(("parallel",)),
        name="outproj_tm" if time_major else "outproj",
    )(x2d, out_a, out_b, w_out, g_ffn, wr_t, br_col)


def _router_kernel(lp_ref, ls_ref, dest_ref, wcol_ref, meta_ref, cnt_scr, base_scr, *, n_prompt_tiles):
    p = pl.program_id(0)
    i = pl.program_id(1)
    tm = lp_ref.shape[1]
    lt = jnp.where(i < n_prompt_tiles, lp_ref[...], ls_ref[...])

    m = lt[0:1, :]
    sel = jnp.zeros((1, tm), I32)
    for r in range(1, N_GROUPS):
        upd = lt[r:r + 1, :] > m
        sel = jnp.where(upd, r, sel)
        m = jnp.where(upd, lt[r:r + 1, :], m)
    den = jnp.zeros((1, tm), F32)
    for r in range(N_GROUPS):
        den = den + jnp.exp(lt[r:r + 1, :] - m)
    g_w = 1.0 / den

    ev = []
    for j in range(EXPERTS_PER_GROUP):
        e = jnp.zeros((1, tm), F32)
        for g in range(N_GROUPS):
            row = N_GROUPS + g * EXPERTS_PER_GROUP + j
            e = jnp.where(sel == g, lt[row:row + 1, :], e)
        ev.append(e)
    v0 = ev[0]
    i0 = jnp.zeros((1, tm), I32)
    for j in range(1, EXPERTS_PER_GROUP):
        upd = ev[j] > v0
        i0 = jnp.where(upd, j, i0)
        v0 = jnp.where(upd, ev[j], v0)
    v1 = jnp.full((1, tm), -jnp.inf, F32)
    i1 = jnp.zeros((1, tm), I32)
    for j in range(EXPERTS_PER_GROUP):
        upd = (ev[j] > v1) & (i0 != j)
        i1 = jnp.where(upd, j, i1)
        v1 = jnp.where(upd, ev[j], v1)
    t = jnp.exp(v1 - v0)
    w0 = g_w / (1.0 + t)
    w1 = g_w * t / (1.0 + t)
    e0 = sel * EXPERTS_PER_GROUP + i0
    e1 = sel * EXPERTS_PER_GROUP + i1

    eio = lax.broadcasted_iota(I32, (N_EXPERTS, tm), 0)
    hit0 = eio == e0
    hit1 = eio == e1
    onehot = (hit0 | hit1).astype(F32)
    tile_cnt = jnp.sum(onehot, axis=1, keepdims=True)

    @pl.when((p == 0) & (i == 0))
    def _():
        cnt_scr[...] = jnp.zeros_like(cnt_scr)

    @pl.when(p == 0)
    def _():
        cnt_scr[...] = cnt_scr[...] + tile_cnt

    @pl.when((p == 1) & (i == 0))
    def _():
        cnt = cnt_scr[...]
        padded = jnp.floor((cnt + (MOE_BLOCK - 1)) * (1.0 / MOE_BLOCK)) * MOE_BLOCK
        ri = lax.broadcasted_iota(I32, (N_EXPERTS, N_EXPERTS), 0)
        ci = lax.broadcasted_iota(I32, (N_EXPERTS, N_EXPERTS), 1)
        lower = (ci < ri).astype(F32)
        pad_start = jnp.dot(lower, jnp.broadcast_to(padded, (N_EXPERTS, LANES)),
                            precision=lax.Precision.HIGHEST, preferred_element_type=F32)[:, 0:1]
        base_scr[...] = pad_start
        pad_end = pad_start + padded
        n_active = jnp.broadcast_to(pad_end[N_EXPERTS - 1:N_EXPERTS, :] * (1.0 / MOE_BLOCK), (1, LANES))
        diag = (lax.broadcasted_iota(I32, (N_EXPERTS, LANES), 0)
                == lax.broadcasted_iota(I32, (N_EXPERTS, LANES), 1))
        blk_start = jnp.sum(jnp.where(diag, pad_start * (1.0 / MOE_BLOCK), 0.0), axis=0, keepdims=True)
        blk_count = jnp.sum(jnp.where(diag, padded * (1.0 / MOE_BLOCK), 0.0), axis=0, keepdims=True)
        meta_ref[...] = jnp.concatenate(
            [n_active, blk_start, blk_count, jnp.zeros((5, LANES), F32)], axis=0).astype(I32)

    @pl.when(p == 1)
    def _():
        ui = lax.broadcasted_iota(I32, (tm, tm), 0)
        uj = lax.broadcasted_iota(I32, (tm, tm), 1)
        upper = (ui < uj).astype(BF16)
        excl = _dot(onehot.astype(BF16), upper)
        pos = base_scr[...] + excl
        d0 = jnp.sum(jnp.where(hit0, pos, 0.0), axis=0, keepdims=True)
        d1 = jnp.sum(jnp.where(hit1, pos, 0.0), axis=0, keepdims=True)
        dest_ref[...] = jnp.concatenate([d0, d1, jnp.zeros((6, tm), F32)], axis=0).astype(I32)
        wmat = jnp.concatenate([w0, w1, jnp.zeros((LANES - 2, tm), F32)], axis=0)
        wcol_ref[...] = wmat.T
        base_scr[...] = base_scr[...] + tile_cnt


def _router(lt_p, lt_s):
    tm = TOKEN_TILE
    n_p = lt_p.shape[1] // tm
    n_s = lt_s.shape[1] // tm
    n_tok = lt_p.shape[1] + lt_s.shape[1]
    return pl.pallas_call(
        partial(_router_kernel, n_prompt_tiles=n_p),
        grid=(2, n_p + n_s),
        in_specs=[
            pl.BlockSpec((ROUTER_ROWS, tm), lambda p, i: (0, jnp.minimum(i, n_p - 1))),
            pl.BlockSpec((ROUTER_ROWS, tm), lambda p, i: (0, jnp.maximum(i - n_p, 0))),
        ],
        out_specs=[
            pl.BlockSpec((8, tm), lambda p, i: (0, i * p)),
            pl.BlockSpec((tm, LANES), lambda p, i: (i * p, 0)),
            pl.BlockSpec((8, LANES), lambda p, i: (0, 0)),
        ],
        out_shape=[
            jax.ShapeDtypeStruct((8, n_tok), I32),
            jax.ShapeDtypeStruct((n_tok, LANES), F32),
            jax.ShapeDtypeStruct((8, LANES), I32),
        ],
        scratch_shapes=[pltpu.VMEM((N_EXPERTS, 1), F32), pltpu.VMEM((N_EXPERTS, 1), F32)],
        compiler_params=_cparams(("arbitrary", "arbitrary")),
        name="router",
    )(lt_p, lt_s)


def _dispatch_kernel(d0_ref, d1_ref, bs_ref, nb_ref, na_ref, hp_ref, hs_ref, xb_ref, sem, zbuf, zsem,
                     *, n_prompt_tiles, n_blocks):
    i = pl.program_id(0)
    tm = hp_ref.shape[0]

    @pl.when(i == 0)
    def _():
        zbuf[...] = jnp.zeros_like(zbuf)

        def zero_copy(blk):
            rows = pl.ds(pl.multiple_of(blk * MOE_BLOCK, MOE_BLOCK), MOE_BLOCK)
            return pltpu.make_async_copy(zbuf, xb_ref.at[rows], zsem)

        def last_block(e):
            return bs_ref[e] + jnp.maximum(nb_ref[e], 1) - 1

        def start_e(e, carry):
            @pl.when(nb_ref[e] > 0)
            def _():
                zero_copy(last_block(e)).start()
            return carry

        def wait_e(e, carry):
            @pl.when(nb_ref[e] > 0)
            def _():
                zero_copy(last_block(e)).wait()
            return carry

        def start_b(b, carry):
            zero_copy(b).start()
            return carry

        def wait_b(b, carry):
            zero_copy(b).wait()
            return carry

        lax.fori_loop(0, N_EXPERTS, start_e, 0)
        lax.fori_loop(na_ref[0], n_blocks, start_b, 0)
        lax.fori_loop(0, N_EXPERTS, wait_e, 0)
        lax.fori_loop(na_ref[0], n_blocks, wait_b, 0)

    def scatter(h_ref):
        def copy(r, d_ref):
            return pltpu.make_async_copy(h_ref.at[pl.ds(r, 1)], xb_ref.at[pl.ds(d_ref[i * tm + r], 1)], sem)

        def issue(r, carry):
            copy(r, d0_ref).start(priority=0)
            copy(r, d1_ref).start(priority=1)
            return carry

        def drain(r, carry):
            copy(r, d0_ref).wait()
            copy(r, d1_ref).wait()
            return carry

        lax.fori_loop(0, tm, issue, 0, unroll=ROW_DMA_UNROLL)
        lax.fori_loop(0, tm, drain, 0, unroll=ROW_DMA_UNROLL)

    @pl.when(i < n_prompt_tiles)
    def _():
        scatter(hp_ref)

    @pl.when(i >= n_prompt_tiles)
    def _():
        scatter(hs_ref)


def _dispatch(d0, d1, block_start, block_count, n_active, h2_p, h2_s, n_blocks):
    tm = TOKEN_TILE
    d = h2_p.shape[1]
    n_p = h2_p.shape[0] // tm
    n_s = h2_s.shape[0] // tm
    return pl.pallas_call(
        partial(_dispatch_kernel, n_prompt_tiles=n_p, n_blocks=n_blocks),
        grid_spec=pltpu.PrefetchScalarGridSpec(
            num_scalar_prefetch=5,
            grid=(n_p + n_s,),
            in_specs=[
                pl.BlockSpec((tm, d), lambda i, *_: (jnp.minimum(i, n_p - 1), 0)),
                pl.BlockSpec((tm, d), lambda i, *_: (jnp.maximum(i - n_p, 0), 0)),
            ],
            out_specs=pl.BlockSpec(memory_space=pl.ANY),
            scratch_shapes=[
                pltpu.SemaphoreType.DMA(()),
                pltpu.VMEM((MOE_BLOCK, d), U32),
                pltpu.SemaphoreType.DMA(()),
            ],
        ),
        out_shape=jax.ShapeDtypeStruct((n_blocks * MOE_BLOCK, d), U32),
        compiler_params=_cparams(("arbitrary",)),
        name="dispatch",
    )(d0, d1, block_start, block_count, n_active, h2_p, h2_s)


def _ffn_kernel(bs_ref, nb_ref, na_ref, xb_ref, wg_ref, wu_ref, wd_ref, yb_ref,
                xbuf, ybuf, wg_bf, wu_bf, wd_bf, xsem, ysem, *, n_blocks):
    e = pl.program_id(0)
    nb = nb_ref[e]
    b0 = bs_ref[e]
    na = na_ref[0]
    ring = xbuf.shape[0]

    def x_copy(blk):
        rows = pl.ds(pl.multiple_of(blk * MOE_BLOCK, MOE_BLOCK), MOE_BLOCK)
        slot = lax.rem(blk, ring)
        return pltpu.make_async_copy(xb_ref.at[rows], xbuf.at[slot], xsem.at[slot])

    def y_copy(blk):
        rows = pl.ds(pl.multiple_of(blk * MOE_BLOCK, MOE_BLOCK), MOE_BLOCK)
        slot = lax.rem(blk, ring)
        return pltpu.make_async_copy(ybuf.at[slot], yb_ref.at[rows], ysem.at[slot])

    @pl.when(e == 0)
    def _():
        for ahead in range(ring - 1):
            @pl.when(ahead < na)
            def _():
                x_copy(ahead).start(priority=1)

    @pl.when(nb > 0)
    def _():
        wg_bf[...] = wg_ref[0].astype(BF16)
        wu_bf[...] = wu_ref[0].astype(BF16)
        wd_bf[...] = wd_ref[0].astype(BF16)

    def block(i, carry):
        blk = b0 + i
        slot = lax.rem(blk, ring)
        x_copy(blk).wait()

        @pl.when(blk + ring - 1 < na)
        def _():
            x_copy(blk + ring - 1).start(priority=1)

        x_lo, x_hi = _unpack_bf16_pairs(xbuf[slot])
        half = x_lo.shape[1]
        g = _dot(x_lo, wg_bf[0:half, :]) + _dot(x_hi, wg_bf[half:, :])
        u = _dot(x_lo, wu_bf[0:half, :]) + _dot(x_hi, wu_bf[half:, :])
        a = (g * jax.nn.sigmoid(g) * u).astype(BF16)
        y = _dot(a, wd_bf[...])

        @pl.when(blk >= ring)
        def _():
            y_copy(blk - ring).wait()

        ybuf[slot] = y
        y_copy(blk).start()
        return carry

    lax.fori_loop(0, nb, block, 0)

    @pl.when(e == pl.num_programs(0) - 1)
    def _():
        for back in range(1, ring + 1):
            @pl.when(na - back >= 0)
            def _():
                y_copy(na - back).wait()

        ybuf[0] = jnp.zeros(ybuf.shape[1:], F32)

        def zero_block(b, carry):
            rows = pl.ds(pl.multiple_of(b * MOE_BLOCK, MOE_BLOCK), MOE_BLOCK)
            cp = pltpu.make_async_copy(ybuf.at[0], yb_ref.at[rows], ysem.at[0])
            cp.start()
            cp.wait()
            return carry

        lax.fori_loop(na_ref[0], n_blocks, zero_block, 0)


def _ffn(block_start, block_count, n_active, xb, w_gate, w_up, w_down):
    n_rows, d_packed = xb.shape
    n_exp, d, d_e = w_gate.shape
    n_blocks = n_rows // MOE_BLOCK
    return pl.pallas_call(
        partial(_ffn_kernel, n_blocks=n_blocks),
        grid_spec=pltpu.PrefetchScalarGridSpec(
            num_scalar_prefetch=3,
            grid=(n_exp,),
            in_specs=[
                pl.BlockSpec(memory_space=pl.ANY),
                pl.BlockSpec((1, d, d_e), lambda e, *_: (e, 0, 0)),
                pl.BlockSpec((1, d, d_e), lambda e, *_: (e, 0, 0)),
                pl.BlockSpec((1, d_e, d), lambda e, *_: (e, 0, 0)),
            ],
            out_specs=pl.BlockSpec(memory_space=pl.ANY),
            scratch_shapes=[
                pltpu.VMEM((FFN_RING, MOE_BLOCK, d_packed), U32),
                pltpu.VMEM((FFN_RING, MOE_BLOCK, d), F32),
                pltpu.VMEM((d, d_e), BF16),
                pltpu.VMEM((d, d_e), BF16),
                pltpu.VMEM((d_e, d), BF16),
                pltpu.SemaphoreType.DMA((FFN_RING,)),
                pltpu.SemaphoreType.DMA((FFN_RING,)),
            ],
        ),
        out_shape=jax.ShapeDtypeStruct((n_rows, d), F32),
        compiler_params=_cparams(("arbitrary",)),
        name="moe_ffn",
    )(block_start, block_count, n_active, xb, w_gate, w_up, w_down)


def _ple_kernel(d0_ref, d1_ref, x1_ref, wcol_ref, p_ref, yb_ref, gple_ref, wpg_ref, wp_ref, gfin_ref,
                y_ref, g_scr, sem, *, time_major, n_steps):
    i = pl.program_id(0)
    tm = x1_ref.shape[0]
    slot = lax.rem(i, 2) if n_steps > 1 else 0

    def copy(tile, r, k, to_slot):
        d_ref = d0_ref if k == 0 else d1_ref
        return pltpu.make_async_copy(yb_ref.at[pl.ds(d_ref[tile * tm + r], 1)],
                                     g_scr.at[to_slot, k, pl.ds(r, 1)], sem.at[to_slot])

    def issue(tile, to_slot):
        def body(r, carry):
            copy(tile, r, 0, to_slot).start(priority=0)
            copy(tile, r, 1, to_slot).start(priority=1)
            return carry
        lax.fori_loop(0, tm, body, 0, unroll=ROW_DMA_UNROLL)

    def drain(tile, from_slot):
        def body(r, carry):
            copy(tile, r, 0, from_slot).wait()
            copy(tile, r, 1, from_slot).wait()
            return carry
        lax.fori_loop(0, tm, body, 0, unroll=ROW_DMA_UNROLL)

    @pl.when(i == 0)
    def _():
        issue(0, 0)

    if n_steps > 1:
        @pl.when(i + 1 < n_steps)
        def _():
            issue(i + 1, 1 - slot)

    drain(i, slot)

    x2 = x1_ref[...] + wcol_ref[:, 0:1] * g_scr[slot, 0] + wcol_ref[:, 1:2] * g_scr[slot, 1]
    hn = _rms(x2, gple_ref[...]).astype(BF16)
    gate = jax.nn.sigmoid(_dot(hn, wpg_ref[...]))
    if time_major:
        pd = wp_ref.shape[0]
        pp = jnp.concatenate([p_ref[:, t * pd:(t + 1) * pd] for t in range(p_ref.shape[1] // pd)], axis=0)
    else:
        pp = p_ref[...]
    x3 = x2 + _dot(pp.astype(BF16), wp_ref[...]) * gate
    y = _rms(x3, gfin_ref[...])
    if time_major:
        d = y.shape[1]
        rows = y_ref.shape[0]
        for t in range(y_ref.shape[1] // d):
            y_ref[:, t * d:(t + 1) * d] = y[t * rows:(t + 1) * rows, :]
    else:
        y_ref[...] = y


def _ple(d0, d1, x1, wcol, p2d, yb, g_ple, w_pg, w_p, g_final, *, time_major):
    n_tok, d = x1.shape
    tm = n_tok if time_major else PLE_TILE
    n_steps = n_tok // tm
    pd = w_p.shape[0]
    if time_major:
        n_seq = p2d.shape[0]
        p_spec = pl.BlockSpec(p2d.shape, lambda i, *_: (0, 0))
        y_spec = pl.BlockSpec((n_seq, (n_tok // n_seq) * d), lambda i, *_: (0, 0))
        y_shape = jax.ShapeDtypeStruct((n_seq, (n_tok // n_seq) * d), F32)
    else:
        p_spec = pl.BlockSpec((tm, pd), lambda i, *_: (i, 0))
        y_spec = pl.BlockSpec((tm, d), lambda i, *_: (i, 0))
        y_shape = jax.ShapeDtypeStruct((n_tok, d), F32)
    return pl.pallas_call(
        partial(_ple_kernel, time_major=time_major, n_steps=n_steps),
        grid_spec=pltpu.PrefetchScalarGridSpec(
            num_scalar_prefetch=2,
            grid=(n_steps,),
            in_specs=[
                pl.BlockSpec((tm, d), lambda i, *_: (i, 0)),
                pl.BlockSpec((tm, LANES), lambda i, *_: (i, 0)),
                p_spec,
                pl.BlockSpec(memory_space=pl.ANY),
                pl.BlockSpec((1, d), lambda i, *_: (0, 0)),
                pl.BlockSpec(w_pg.shape, lambda i, *_: (0, 0)),
                pl.BlockSpec(w_p.shape, lambda i, *_: (0, 0)),
                pl.BlockSpec((1, d), lambda i, *_: (0, 0)),
            ],
            out_specs=y_spec,
            scratch_shapes=[
                pltpu.VMEM((min(n_steps, 2), 2, tm, d), F32),
                pltpu.SemaphoreType.DMA((min(n_steps, 2),)),
            ],
        ),
        out_shape=y_shape,
        compiler_params=_cparams(("arbitrary",)),
        name="ple_tm" if time_major else "ple",
    )(d0, d1, x1, wcol, p2d, yb, g_ple, w_pg, w_p, g_final)


def _pad_lanes(v, offset=0):
    row = jnp.zeros((1, LANES), F32)
    return row.at[0, offset:offset + v.shape[0]].set(v.astype(F32))


def kernel(x_prompt, x_sample, state_conv, state_delta, p_prompt, p_sample, g_mix, w_in, w_s, b_s, g_v,
           conv_w, a_log, dt_bias, g_out, w_out, g_ffn, w_group, b_group, w_router, b_router, w_gate, w_up,
           w_down, g_ple, w_ple_gate, w_ple, g_final):
    n_batch, seq, d = x_prompt.shape
    n_seq, n_pos, _ = x_sample.shape
    depth = g_mix.shape[0]
    assert depth == 1
    a_width = A_GROUPS * A_GROUP_DIM
    hd = B_HEADS * B_DIM
    n_main = 2 * a_width + 4 * hd
    n_p = n_batch * seq
    n_s = n_seq * n_pos

    xp = x_prompt.reshape(n_p, d)
    xs = x_sample.reshape(n_seq, n_pos * d)
    l = 0

    w_main = w_in[l][:, :n_main].astype(BF16)
    w_tail = jnp.pad(w_in[l][:, n_main:], ((0, 0), (0, LANES - 2 * B_HEADS))).astype(BF16)
    gm = g_mix[l].reshape(1, d)
    causal = jnp.tril(jnp.ones((A_CHUNK, A_CHUNK), dtype=bool))
    ws_tril = jnp.where(causal, w_s[l], 0).astype(BF16)
    b_t = b_s[l].T
    gv = g_v[l].reshape(1, a_width)
    alog_row = _pad_lanes(a_log[l])
    dtb_row = _pad_lanes(dt_bias[l])
    gout = g_out[l].reshape(1, B_DIM)
    w_o = w_out[l].astype(BF16)
    gf = g_ffn[l].reshape(1, d)
    wr = jnp.pad(jnp.concatenate([w_group[l], w_router[l]], axis=1),
                 ((0, 0), (0, LANES - N_GROUPS - N_EXPERTS)))
    wr_hi = wr.astype(BF16)
    wr_t = jnp.concatenate([wr_hi, (wr - wr_hi.astype(F32)).astype(BF16)], axis=1)
    br_col = jnp.broadcast_to(
        jnp.pad(jnp.concatenate([b_group[l], b_router[l]]), (0, ROUTER_ROWS - N_GROUPS - N_EXPERTS))[:, None],
        (ROUTER_ROWS, LANES))
    gp = g_ple[l].reshape(1, d)
    w_pg = w_ple_gate[l].astype(BF16)
    w_p = w_ple[l].astype(BF16)
    gfin = g_final.reshape(1, d)

    proj_p, tail_p = _inproj(xp, gm, w_main, w_tail, n_main, time_major=False)
    proj_s, tail_s = _inproj(xs, gm, w_main, w_tail, n_main, time_major=True)
    oa_p = _mixa_prompt(proj_p, ws_tril, b_t, gv, n_p)
    oa_s, va_s = _mixa_sample(proj_s, w_s[l], b_s[l], gv, n_seq, n_pos)
    ob_p, sd_p, cb_p = _gdn_prompt(proj_p, tail_p, conv_w[l], alog_row, dtb_row, gout, n_batch, seq)
    cbuf = state_conv[l].reshape(n_seq, (CONV_W - 1) * 3 * hd)
    ob_s, sd_s, cb_s = _gdn_decode(proj_s, tail_s, cbuf, state_delta[l], conv_w[l], alog_row, dtb_row,
                                   gout, n_seq, n_pos)
    ob_s = ob_s.reshape(n_s, hd)

    x1_p, h2_p, lt_p = _outproj(xp, oa_p, ob_p, w_o, gf, wr_t, br_col, time_major=False)
    x1_s, h2_s, lt_s = _outproj(xs, oa_s, ob_s, w_o, gf, wr_t, br_col, time_major=True)
    dest, wcol, meta = _router(lt_p, lt_s)
    d0, d1 = dest[0], dest[1]
    n_active = meta[0, 0:1]
    block_start = meta[1, :N_EXPERTS]
    block_count = meta[2, :N_EXPERTS]

    n_tok = n_p + n_s
    n_blocks = (n_tok * 2) // MOE_BLOCK + N_EXPERTS
    xb = _dispatch(d0, d1, block_start, block_count, n_active, h2_p, h2_s, n_blocks)
    yb = _ffn(block_start, block_count, n_active, xb, w_gate[l], w_up[l], w_down[l])

    pp = p_prompt[l].reshape(n_p, -1)
    ps = p_sample[l].reshape(n_seq, -1)
    y_p = _ple(d0[:n_p], d1[:n_p], x1_p, wcol[:n_p], pp, yb, gp, w_pg, w_p, gfin, time_major=False)
    y_s = _ple(d0[n_p:], d1[n_p:], x1_s, wcol[n_p:], ps, yb, gp, w_pg, w_p, gfin, time_major=True)

    return (
        y_p.reshape(n_batch, seq, d),
        y_s.reshape(n_seq, n_pos, d),
        cb_p[None],
        sd_p[None],
        cb_s.reshape(n_seq, CONV_W - 1, 3 * hd)[None],
        sd_s[None],
        va_s.reshape(n_seq, n_pos, a_width)[None],
    )
```

```python
from functools import partial

import jax
import jax.numpy as jnp
from jax import lax
from jax.experimental import pallas as pl
from jax.experimental.pallas import tpu as pltpu

F32 = jnp.float32
BF16 = jnp.bfloat16
I32 = jnp.int32
U32 = jnp.uint32

EPS = 1e-6
LANES = 128
A_GROUPS = 8
A_GROUP_DIM = 128
A_CHUNK = 128
B_HEADS = 8
B_DIM = 128
CONV_W = 4
N_GROUPS = 4
EXPERTS_PER_GROUP = 8
N_EXPERTS = N_GROUPS * EXPERTS_PER_GROUP
ROUTER_ROWS = 40
TOKEN_TILE = 512
INPROJ_TILE = 1024
PLE_TILE = 256
GDN_CHUNK = 64
MOE_BLOCK = 128
FFN_RING = 4
DEC_SEQ_TILE = 16
ROW_DMA_UNROLL = 8
VMEM_LIMIT = 56 * 1024 * 1024


def _cparams(sem):
    return pltpu.CompilerParams(dimension_semantics=sem, vmem_limit_bytes=VMEM_LIMIT)


def _rms(x, g):
    return x * lax.rsqrt(jnp.mean(x * x, axis=-1, keepdims=True) + EPS) * g


def _softplus(x):
    return jnp.maximum(x, 0.0) + jnp.log(1.0 + jnp.exp(-jnp.abs(x)))


def _dot(a, b):
    return jnp.dot(a, b, preferred_element_type=F32)


def _dot_nt(a, b, precision=None):
    return lax.dot_general(a, b, (((1,), (1,)), ((), ())), precision=precision,
                           preferred_element_type=F32)


def _dot_tn(a, b):
    return lax.dot_general(a, b, (((0,), (0,)), ((), ())), preferred_element_type=F32)


def _pack_bf16_pairs(x):
    n = x.shape[1] // 2
    lo = lax.bitcast_convert_type(x[:, :n].astype(BF16).astype(F32), U32) >> 16
    hi = lax.bitcast_convert_type(x[:, n:].astype(BF16).astype(F32), U32) & jnp.uint32(0xFFFF0000)
    return hi | lo


def _unpack_bf16_pairs(xu):
    lo = lax.bitcast_convert_type(xu << 16, F32).astype(BF16)
    hi = lax.bitcast_convert_type(xu & jnp.uint32(0xFFFF0000), F32).astype(BF16)
    return lo, hi


def _inproj_kernel(x_ref, g_ref, w_ref, wt_ref, proj_ref, tail_ref, h_scr, *, time_major):
    j = pl.program_id(1)

    @pl.when(j == 0)
    def _():
        g = g_ref[...]
        if time_major:
            d = g.shape[-1]
            rows = x_ref.shape[0]
            for t in range(x_ref.shape[1] // d):
                h_scr[t * rows:(t + 1) * rows, :] = _rms(x_ref[:, t * d:(t + 1) * d], g).astype(BF16)
        else:
            h_scr[...] = _rms(x_ref[...], g).astype(BF16)
        tail_ref[...] = _dot_nt(h_scr[...], wt_ref[...])

    proj_ref[...] = _dot_nt(h_scr[...], w_ref[...])


def _inproj(x2d, g_mix, w_main, w_tail, n_main, *, time_major, col_tile=1024):
    d = g_mix.shape[-1]
    if time_major:
        n_tok = x2d.shape[0] * (x2d.shape[1] // d)
        tm = n_tok
        x_spec = pl.BlockSpec(x2d.shape, lambda i, j: (0, 0))
    else:
        n_tok = x2d.shape[0]
        tm = min(INPROJ_TILE, n_tok)
        x_spec = pl.BlockSpec((tm, d), lambda i, j: (i, 0))
    grid = (n_tok // tm, n_main // col_tile)
    return pl.pallas_call(
        partial(_inproj_kernel, time_major=time_major),
        grid=grid,
        in_specs=[
            x_spec,
            pl.BlockSpec((1, d), lambda i, j: (0, 0)),
            pl.BlockSpec((col_tile, d), lambda i, j: (j, 0)),
            pl.BlockSpec((LANES, d), lambda i, j: (0, 0)),
        ],
        out_specs=[
            pl.BlockSpec((tm, col_tile), lambda i, j: (i, j)),
            pl.BlockSpec((tm, LANES), lambda i, j: (i, 0)),
        ],
        out_shape=[
            jax.ShapeDtypeStruct((n_tok, n_main), F32),
            jax.ShapeDtypeStruct((n_tok, LANES), F32),
        ],
        scratch_shapes=[pltpu.VMEM((tm, d), BF16)],
        compiler_params=_cparams(("parallel", "arbitrary")),
        name="inproj_tm" if time_major else "inproj",
    )(x2d, g_mix, w_main, w_tail)


def _group_ln(v, g):
    mu = jnp.mean(v, axis=-1, keepdims=True)
    dlt = v - mu
    var = jnp.mean(dlt * dlt, axis=-1, keepdims=True)
    return dlt * lax.rsqrt(var + EPS) * g


def _inproj_mixa_kernel(x_ref, g_ref, w_ref, wt_ref, ws_ref, bt_ref, gv_ref,
                        proj_ref, tail_ref, oa_ref, h_scr, u_scr, v_scr, *, n_later):
    j = pl.program_id(1)
    tm = h_scr.shape[0]

    @pl.when(j == 0)
    def _():
        h_scr[...] = _rms(x_ref[...], g_ref[...]).astype(BF16)
        tail_ref[...] = _dot_nt(h_scr[...], wt_ref[...])
        u_scr[...] = _dot_nt(h_scr[...], w_ref[...])

    @pl.when(j == 1)
    def _():
        v_scr[...] = _dot_nt(h_scr[...], w_ref[...])

    @pl.when(j >= 2)
    def _():
        proj_ref[...] = _dot_nt(h_scr[...], w_ref[...])
        share = tm // n_later
        r0 = (j - 2) * share
        for c in range(share // A_CHUNK):
            rows = pl.ds(pl.multiple_of(r0 + c * A_CHUNK, A_CHUNK), A_CHUNK)
            for h in range(A_GROUPS):
                cols = slice(h * A_GROUP_DIM, (h + 1) * A_GROUP_DIM)
                u = jax.nn.gelu(u_scr[rows, cols])
                v = _group_ln(jax.nn.gelu(v_scr[rows, cols]), gv_ref[:, cols])
                mixed = _dot(ws_ref[h], v.astype(BF16)) + bt_ref[:, h:h + 1]
                oa_ref[rows, cols] = (u * mixed).astype(BF16)


def _inproj_mixa(x2d, g_mix, w_main, w_tail, ws_tril, b_t, g_v, n_main):
    n_tok, d = x2d.shape
    a_width = A_GROUPS * A_GROUP_DIM
    tm = min(INPROJ_TILE, n_tok)
    n_col = n_main // a_width
    return pl.pallas_call(
        partial(_inproj_mixa_kernel, n_later=n_col - 2),
        grid=(n_tok // tm, n_col),
        in_specs=[
            pl.BlockSpec((tm, d), lambda i, j: (i, 0)),
            pl.BlockSpec((1, d), lambda i, j: (0, 0)),
            pl.BlockSpec((a_width, d), lambda i, j: (j, 0)),
            pl.BlockSpec((LANES, d), lambda i, j: (0, 0)),
            pl.BlockSpec((A_GROUPS, A_CHUNK, A_CHUNK), lambda i, j: (0, 0, 0)),
            pl.BlockSpec((A_CHUNK, A_GROUPS), lambda i, j: (0, 0)),
            pl.BlockSpec((1, a_width), lambda i, j: (0, 0)),
        ],
        out_specs=[
            pl.BlockSpec((tm, a_width), lambda i, j: (i, jnp.maximum(j - 2, 0))),
            pl.BlockSpec((tm, LANES), lambda i, j: (i, 0)),
            pl.BlockSpec((tm, a_width), lambda i, j: (i, 0)),
        ],
        out_shape=[
            jax.ShapeDtypeStruct((n_tok, n_main - 2 * a_width), F32),
            jax.ShapeDtypeStruct((n_tok, LANES), F32),
            jax.ShapeDtypeStruct((n_tok, a_width), BF16),
        ],
        scratch_shapes=[pltpu.VMEM((tm, d), BF16), pltpu.VMEM((tm, a_width), F32),
                        pltpu.VMEM((tm, a_width), F32)],
        compiler_params=_cparams(("parallel", "arbitrary")),
        name="inproj_mixa",
    )(x2d, g_mix, w_main, w_tail, ws_tril, b_t, g_v)


def _mixa_sample_kernel(ws_ref, bs_ref, au_ref, av_ref, gv_ref, out_ref, va_ref, *, n_seq, n_pos):
    a_width = A_GROUPS * A_GROUP_DIM
    vs = []
    for t in range(n_pos):
        rows = slice(t * n_seq, (t + 1) * n_seq)
        per_group = []
        for h in range(A_GROUPS):
            cols = slice(h * A_GROUP_DIM, (h + 1) * A_GROUP_DIM)
            per_group.append(_group_ln(jax.nn.gelu(av_ref[rows, cols]), gv_ref[:, cols]))
        vs.append(per_group)
        for h in range(A_GROUPS):
            va_ref[:, t * a_width + h * A_GROUP_DIM:t * a_width + (h + 1) * A_GROUP_DIM] = per_group[h]
    for t in range(n_pos):
        rows = slice(t * n_seq, (t + 1) * n_seq)
        for h in range(A_GROUPS):
            cols = slice(h * A_GROUP_DIM, (h + 1) * A_GROUP_DIM)
            mixed = jnp.zeros((n_seq, A_GROUP_DIM), F32) + bs_ref[h * n_pos + t]
            for s in range(t + 1):
                mixed = mixed + ws_ref[(h * n_pos + t) * n_pos + s] * vs[s][h]
            out_ref[rows, cols] = (jax.nn.gelu(au_ref[rows, cols]) * mixed).astype(BF16)


def _mixa_sample(proj, w_s, b_s, g_v, n_seq, n_pos):
    a_width = A_GROUPS * A_GROUP_DIM
    n_tok = n_seq * n_pos
    return pl.pallas_call(
        partial(_mixa_sample_kernel, n_seq=n_seq, n_pos=n_pos),
        grid_spec=pltpu.PrefetchScalarGridSpec(
            num_scalar_prefetch=2,
            grid=(1,),
            in_specs=[
                pl.BlockSpec((n_tok, a_width), lambda i, *_: (0, 0)),
                pl.BlockSpec((n_tok, a_width), lambda i, *_: (0, 1)),
                pl.BlockSpec((1, a_width), lambda i, *_: (0, 0)),
            ],
            out_specs=[
                pl.BlockSpec((n_tok, a_width), lambda i, *_: (0, 0)),
                pl.BlockSpec((n_seq, n_pos * a_width), lambda i, *_: (0, 0)),
            ],
        ),
        out_shape=[
            jax.ShapeDtypeStruct((n_tok, a_width), BF16),
            jax.ShapeDtypeStruct((n_seq, n_pos * a_width), F32),
        ],
        compiler_params=_cparams(("arbitrary",)),
        name="mixa_sample",
    )(w_s[:, :n_pos, :n_pos].reshape(-1), b_s[:, :n_pos].reshape(-1), proj, proj, g_v)


def _gdn_prompt_kernel(q_ref, k_ref, v_ref, z_ref, tail_ref, cw_ref, alog_ref, dtb_ref, gout_ref,
                       out_ref, s_out_ref, conv_out_ref, s_scr, xbuf):
    c = pl.program_id(1)
    n_c = pl.num_programs(1)
    C = q_ref.shape[0]
    hd = B_HEADS * B_DIM
    pad = 8
    lo = pad - (CONV_W - 1)

    @pl.when(c == 0)
    def _():
        s_scr[...] = jnp.zeros_like(s_scr)
        xbuf[0:pad, :] = jnp.zeros((pad, 3 * hd), F32)

    xbuf[pad:pad + C, 0:hd] = q_ref[...]
    xbuf[pad:pad + C, hd:2 * hd] = k_ref[...]
    xbuf[pad:pad + C, 2 * hd:3 * hd] = v_ref[...]
    y = xbuf[lo:lo + C, :] * cw_ref[0:1, :]
    for j in range(1, CONV_W):
        y = y + xbuf[lo + j:lo + j + C, :] * cw_ref[j:j + 1, :]
    y = y * jax.nn.sigmoid(y)
    last = xbuf[pad + C - (CONV_W - 1):pad + C, :]
    xbuf[lo:pad, :] = last

    @pl.when(c == n_c - 1)
    def _():
        conv_out_ref[0] = last

    tail = tail_ref[...]
    g_all = -jnp.exp(alog_ref[...]) * _softplus(tail + dtb_ref[...])
    beta_all = jax.nn.sigmoid(tail)
    ri = lax.broadcasted_iota(I32, (C, C), 0)
    ci = lax.broadcasted_iota(I32, (C, C), 1)
    causal = ci <= ri
    strict = ci < ri
    tril = causal.astype(F32)
    G_all = jnp.dot(tril, g_all, precision=lax.Precision.HIGHEST, preferred_element_type=F32)
    G_rows = jnp.concatenate([G_all, jnp.zeros((LANES - C, LANES), F32)], axis=0).T if C < LANES else G_all.T
    scale = B_DIM ** -0.5
    n_levels = max(C.bit_length() - 2, 0)
    heads = range(B_HEADS)

    qs, ks, Gs, eGs, rhss, qks, Ps = [], [], [], [], [], [], []
    for h in heads:
        q = y[:, h * B_DIM:(h + 1) * B_DIM]
        k = y[:, hd + h * B_DIM:hd + (h + 1) * B_DIM]
        v = y[:, 2 * hd + h * B_DIM:2 * hd + (h + 1) * B_DIM]
        q = q * lax.rsqrt(jnp.sum(q * q, axis=-1, keepdims=True) + EPS) * scale
        k = k * lax.rsqrt(jnp.sum(k * k, axis=-1, keepdims=True) + EPS)
        G = G_all[:, h:h + 1]
        beta = beta_all[:, B_HEADS + h:B_HEADS + h + 1]
        eG = jnp.exp(G)
        decay = jnp.exp(jnp.where(causal, G - G_rows[h:h + 1, 0:C], -jnp.inf))
        qk_kk = _dot_nt(jnp.concatenate([q, k], axis=0), k)
        qks.append(qk_kk[:C] * decay)
        Ps.append(jnp.where(strict, -(beta * qk_kk[C:] * decay), 0.0))
        rhss.append(jnp.concatenate([v * beta, k * (beta * eG)], axis=1))
        qs.append(q)
        ks.append(k)
        Gs.append(G)
        eGs.append(eG)

    Ls = list(Ps)
    if n_levels > 0:
        Ps = [_dot(P, P) for P in Ps]
    for lvl in range(1, n_levels + 1):
        for h in heads:
            if lvl < n_levels:
                R = _dot(jnp.concatenate([Ps[h], Ls[h]], axis=0), Ps[h])
                Ls[h] = Ls[h] + Ps[h] + R[C:]
                Ps[h] = R[:C]
            else:
                Ls[h] = Ls[h] + Ps[h] + _dot(Ls[h], Ps[h])

    uws = [rhss[h] + _dot(Ls[h], rhss[h]) for h in heads]
    Ss = [s_scr[h] for h in heads]
    rs = [_dot(jnp.concatenate([uws[h][:, B_DIM:], qs[h] * eGs[h]], axis=0), Ss[h]) for h in heads]
    v_news = [uws[h][:, :B_DIM] - rs[h][:C] for h in heads]
    os_ = [rs[h][C:] + _dot(qks[h], v_news[h]) for h in heads]
    for h in heads:
        G_last = Gs[h][C - 1:C, :]
        kd = ks[h] * jnp.exp(G_last - Gs[h])
        s_scr[h] = Ss[h] * jnp.exp(G_last) + _dot_tn(kd, v_news[h])
    for h in heads:
        cols = slice(h * B_DIM, (h + 1) * B_DIM)
        o = _rms(os_[h], gout_ref[...])
        z = z_ref[:, cols]
        out_ref[:, cols] = (o * (z * jax.nn.sigmoid(z))).astype(BF16)

    @pl.when(c == n_c - 1)
    def _():
        s_out_ref[0] = s_scr[...]


def _gdn_prompt(proj, tail, conv_w, alog_row, dtb_row, g_out, n_batch, seq):
    C = GDN_CHUNK
    n_c = seq // C
    hd = B_HEADS * B_DIM
    row = lambda b, c: b * n_c + c
    return pl.pallas_call(
        _gdn_prompt_kernel,
        grid=(n_batch, n_c),
        in_specs=[
            pl.BlockSpec((C, hd), lambda b, c: (row(b, c), 0)),
            pl.BlockSpec((C, hd), lambda b, c: (row(b, c), 1)),
            pl.BlockSpec((C, hd), lambda b, c: (row(b, c), 2)),
            pl.BlockSpec((C, hd), lambda b, c: (row(b, c), 3)),
            pl.BlockSpec((C, LANES), lambda b, c: (row(b, c), 0)),
            pl.BlockSpec((CONV_W, 3 * hd), lambda b, c: (0, 0)),
            pl.BlockSpec((1, LANES), lambda b, c: (0, 0)),
            pl.BlockSpec((1, LANES), lambda b, c: (0, 0)),
            pl.BlockSpec((1, B_DIM), lambda b, c: (0, 0)),
        ],
        out_specs=[
            pl.BlockSpec((C, hd), lambda b, c: (row(b, c), 0)),
            pl.BlockSpec((1, B_HEADS, B_DIM, B_DIM), lambda b, c: (b, 0, 0, 0)),
            pl.BlockSpec((1, CONV_W - 1, 3 * hd), lambda b, c: (b, 0, 0)),
        ],
        out_shape=[
            jax.ShapeDtypeStruct((n_batch * seq, hd), BF16),
            jax.ShapeDtypeStruct((n_batch, B_HEADS, B_DIM, B_DIM), F32),
            jax.ShapeDtypeStruct((n_batch, CONV_W - 1, 3 * hd), F32),
        ],
        scratch_shapes=[
            pltpu.VMEM((B_HEADS, B_DIM, B_DIM), F32),
            pltpu.VMEM((8 + C, 3 * hd), F32),
        ],
        compiler_params=_cparams(("parallel", "arbitrary")),
        name="gdn_prompt",
    )(proj, proj, proj, proj, tail, conv_w, alog_row, dtb_row, g_out)


def _gdn_decode_kernel(*refs, n_pos):
    proj_refs = refs[:n_pos]
    tail_refs = refs[n_pos:2 * n_pos]
    (cbuf_ref, s_ref, cw_ref, alog_ref, dtb_ref, gout_ref,
     out_ref, s_out_ref, conv_out_ref, lhs_scr, res_scr, kd_scr, vn_scr, gl_scr) = refs[2 * n_pos:]
    SB = s_ref.shape[0]
    hd = B_HEADS * B_DIM
    qkv0 = 2 * hd
    z0 = qkv0 + 3 * hd
    scale = B_DIM ** -0.5

    xp = [cbuf_ref[:, j * 3 * hd:(j + 1) * 3 * hd] for j in range(CONV_W - 1)]
    xp += [proj_refs[t][:, qkv0:qkv0 + 3 * hd] for t in range(n_pos)]
    for j in range(CONV_W - 1):
        conv_out_ref[:, j * 3 * hd:(j + 1) * 3 * hd] = xp[n_pos + j]
    ys = []
    for t in range(n_pos):
        y = xp[t] * cw_ref[0:1, :]
        for j in range(1, CONV_W):
            y = y + xp[t + j] * cw_ref[j:j + 1, :]
        ys.append(y * jax.nn.sigmoid(y))
    gs, betas = [], []
    for t in range(n_pos):
        tail = tail_refs[t][...]
        gs.append(-jnp.exp(alog_ref[...]) * _softplus(tail + dtb_ref[...]))
        betas.append(jax.nn.sigmoid(tail))

    kd_scr[...] = jnp.zeros_like(kd_scr)
    vn_scr[...] = jnp.zeros_like(vn_scr)
    heads = range(B_HEADS)
    stride = 2 * n_pos

    qs, ks, Gs, us = [], [], [], []
    for h in heads:
        q_h, k_h, v_h, G_h, b_h = [], [], [], [], []
        G = None
        for t in range(n_pos):
            q = ys[t][:, h * B_DIM:(h + 1) * B_DIM]
            k = ys[t][:, hd + h * B_DIM:hd + (h + 1) * B_DIM]
            q_h.append(q * lax.rsqrt(jnp.sum(q * q, axis=-1, keepdims=True) + EPS) * scale)
            k_h.append(k * lax.rsqrt(jnp.sum(k * k, axis=-1, keepdims=True) + EPS))
            v_h.append(ys[t][:, 2 * hd + h * B_DIM:2 * hd + (h + 1) * B_DIM])
            g = gs[t][:, h:h + 1]
            G = g if G is None else G + g
            G_h.append(G)
            b_h.append(betas[t][:, B_HEADS + h:B_HEADS + h + 1])
        u_h, w_h = [], []
        for t in range(n_pos):
            u = v_h[t] * b_h[t]
            w = k_h[t] * (b_h[t] * jnp.exp(G_h[t]))
            for s in range(t):
                a = b_h[t] * jnp.sum(k_h[t] * k_h[s], axis=-1, keepdims=True) * jnp.exp(G_h[t] - G_h[s])
                u = u - a * u_h[s]
                w = w - a * w_h[s]
            u_h.append(u)
            w_h.append(w)
        for t in range(n_pos):
            lhs_scr[h, pl.ds(t, SB, stride=stride), :] = w_h[t]
            lhs_scr[h, pl.ds(n_pos + t, SB, stride=stride), :] = q_h[t] * jnp.exp(G_h[t])
        qs.append(q_h)
        ks.append(k_h)
        Gs.append(G_h)
        us.append(u_h)

    for h in heads:
        for s in range(SB):
            rows = slice(s * stride, (s + 1) * stride)
            res_scr[h, rows, :] = _dot(lhs_scr[h, rows, :], s_ref[s, h])

    for h in heads:
        cols = slice(h * B_DIM, (h + 1) * B_DIM)
        v_news = [us[h][t] - res_scr[h, pl.ds(t, SB, stride=stride), :] for t in range(n_pos)]
        G_last = Gs[h][n_pos - 1]
        for t in range(n_pos):
            o = res_scr[h, pl.ds(n_pos + t, SB, stride=stride), :]
            for s in range(t + 1):
                qk = jnp.sum(qs[h][t] * ks[h][s], axis=-1, keepdims=True) * jnp.exp(Gs[h][t] - Gs[h][s])
                o = o + qk * v_news[s]
            o = _rms(o, gout_ref[...])
            z = proj_refs[t][:, z0 + h * B_DIM:z0 + (h + 1) * B_DIM]
            out_ref[t, :, cols] = (o * (z * jax.nn.sigmoid(z))).astype(BF16)
            kd_scr[h, pl.ds(t, SB, stride=stride), :] = ks[h][t] * jnp.exp(G_last - Gs[h][t])
            vn_scr[h, pl.ds(t, SB, stride=stride), :] = v_news[t]
        gl_scr[h] = jnp.broadcast_to(jnp.exp(G_last), (SB, B_DIM))

    for h in heads:
        for s in range(SB):
            rows = slice(s * stride, (s + 1) * stride)
            s_out_ref[s, h] = (s_ref[s, h] * gl_scr[h, s:s + 1, 0:1]
                               + _dot_tn(kd_scr[h, rows, :], vn_scr[h, rows, :]))


def _gdn_decode(proj_s, tail_s, cbuf, s0, conv_w, alog_row, dtb_row, g_out, n_seq, n_pos):
    SB = DEC_SEQ_TILE
    hd = B_HEADS * B_DIM
    n_main = proj_s.shape[1]
    per_pos = n_seq // SB
    proj_specs = [pl.BlockSpec((SB, n_main), lambda i, t=t: (t * per_pos + i, 0)) for t in range(n_pos)]
    tail_specs = [pl.BlockSpec((SB, LANES), lambda i, t=t: (t * per_pos + i, 0)) for t in range(n_pos)]
    return pl.pallas_call(
        partial(_gdn_decode_kernel, n_pos=n_pos),
        grid=(per_pos,),
        in_specs=proj_specs + tail_specs + [
            pl.BlockSpec((SB, (CONV_W - 1) * 3 * hd), lambda i: (i, 0)),
            pl.BlockSpec((SB, B_HEADS, B_DIM, B_DIM), lambda i: (i, 0, 0, 0)),
            pl.BlockSpec((CONV_W, 3 * hd), lambda i: (0, 0)),
            pl.BlockSpec((1, LANES), lambda i: (0, 0)),
            pl.BlockSpec((1, LANES), lambda i: (0, 0)),
            pl.BlockSpec((1, B_DIM), lambda i: (0, 0)),
        ],
        out_specs=[
            pl.BlockSpec((n_pos, SB, hd), lambda i: (0, i, 0)),
            pl.BlockSpec((SB, B_HEADS, B_DIM, B_DIM), lambda i: (i, 0, 0, 0)),
            pl.BlockSpec((SB, (CONV_W - 1) * 3 * hd), lambda i: (i, 0)),
        ],
        out_shape=[
            jax.ShapeDtypeStruct((n_pos, n_seq, hd), BF16),
            jax.ShapeDtypeStruct((n_seq, B_HEADS, B_DIM, B_DIM), F32),
            jax.ShapeDtypeStruct((n_seq, (CONV_W - 1) * 3 * hd), F32),
        ],
        scratch_shapes=[
            pltpu.VMEM((B_HEADS, SB * 2 * n_pos, B_DIM), F32),
            pltpu.VMEM((B_HEADS, SB * 2 * n_pos, B_DIM), F32),
            pltpu.VMEM((B_HEADS, SB * 2 * n_pos, B_DIM), F32),
            pltpu.VMEM((B_HEADS, SB * 2 * n_pos, B_DIM), F32),
            pltpu.VMEM((B_HEADS, SB, B_DIM), F32),
        ],
        compiler_params=_cparams(("parallel",)),
        name="gdn_decode",
    )(*([proj_s] * n_pos), *([tail_s] * n_pos), cbuf, s0, conv_w, alog_row, dtb_row, g_out)


def _outproj_kernel(x_ref, oa_ref, ob_ref, w_ref, g_ref, wr_ref, br_ref,
                    x1_ref, h2_ref, lt_ref, *, time_major):
    a_width = oa_ref.shape[1]
    acc = _dot(oa_ref[...], w_ref[0:a_width, :]) + _dot(ob_ref[...], w_ref[a_width:, :])
    if time_major:
        d = g_ref.shape[-1]
        rows = x_ref.shape[0]
        x = jnp.concatenate([x_ref[:, t * d:(t + 1) * d] for t in range(x_ref.shape[1] // d)], axis=0)
    else:
        x = x_ref[...]
    x1 = x + acc
    x1_ref[...] = x1
    h2 = _rms(x1, g_ref[...])
    h2_ref[...] = _pack_bf16_pairs(h2)
    h_hi = h2.astype(BF16)
    h_lo = (h2 - h_hi.astype(F32)).astype(BF16)
    tm = h2.shape[0]
    lg = _dot(jnp.concatenate([h_hi, h_lo], axis=0), wr_ref[...])
    lg = lg[:tm, :LANES] + lg[:tm, LANES:] + lg[tm:, :LANES] + lg[tm:, LANES:]
    lt_ref[...] = lg.T[0:ROUTER_ROWS, :] + br_ref[:, 0:1]


def _outproj(x2d, out_a, out_b, w_out, g_ffn, wr_t, br_col, *, time_major):
    d = g_ffn.shape[-1]
    a_width = out_a.shape[1]
    n_tok = out_a.shape[0]
    tm = TOKEN_TILE
    if time_major:
        x_spec = pl.BlockSpec(x2d.shape, lambda i: (0, 0))
    else:
        x_spec = pl.BlockSpec((tm, d), lambda i: (i, 0))
    return pl.pallas_call(
        partial(_outproj_kernel, time_major=time_major),
        grid=(n_tok // tm,),
        in_specs=[
            x_spec,
            pl.BlockSpec((tm, a_width), lambda i: (i, 0)),
            pl.BlockSpec((tm, out_b.shape[1]), lambda i: (i, 0)),
            pl.BlockSpec(w_out.shape, lambda i: (0, 0)),
            pl.BlockSpec((1, d), lambda i: (0, 0)),
            pl.BlockSpec(wr_t.shape, lambda i: (0, 0)),
            pl.BlockSpec(br_col.shape, lambda i: (0, 0)),
        ],
        out_specs=[
            pl.BlockSpec((tm, d), lambda i: (i, 0)),
            pl.BlockSpec((tm, d // 2), lambda i: (i, 0)),
            pl.BlockSpec((ROUTER_ROWS, tm), lambda i: (0, i)),
        ],
        out_shape=[
            jax.ShapeDtypeStruct((n_tok, d), F32),
            jax.ShapeDtypeStruct((n_tok, d // 2), U32),
            jax.ShapeDtypeStruct((ROUTER_ROWS, n_tok), F32),
        ],
        compiler_params=_cparams(("parallel",)),
        name="outproj_tm" if time_major else "outproj",
    )(x2d, out_a, out_b, w_out, g_ffn, wr_t, br_col)


def _router_kernel(lp_ref, ls_ref, dest_ref, wcol_ref, meta_ref, cnt_scr, base_scr, *, n_prompt_tiles):
    p = pl.program_id(0)
    i = pl.program_id(1)
    tm = lp_ref.shape[1]
    lt = jnp.where(i < n_prompt_tiles, lp_ref[...], ls_ref[...])

    m = lt[0:1, :]
    sel = jnp.zeros((1, tm), I32)
    for r in range(1, N_GROUPS):
        upd = lt[r:r + 1, :] > m
        sel = jnp.where(upd, r, sel)
        m = jnp.where(upd, lt[r:r + 1, :], m)
    den = jnp.zeros((1, tm), F32)
    for r in range(N_GROUPS):
        den = den + jnp.exp(lt[r:r + 1, :] - m)
    g_w = 1.0 / den

    ev = []
    for j in range(EXPERTS_PER_GROUP):
        e = jnp.zeros((1, tm), F32)
        for g in range(N_GROUPS):
            row = N_GROUPS + g * EXPERTS_PER_GROUP + j
            e = jnp.where(sel == g, lt[row:row + 1, :], e)
        ev.append(e)
    v0 = ev[0]
    i0 = jnp.zeros((1, tm), I32)
    for j in range(1, EXPERTS_PER_GROUP):
        upd = ev[j] > v0
        i0 = jnp.where(upd, j, i0)
        v0 = jnp.where(upd, ev[j], v0)
    v1 = jnp.full((1, tm), -jnp.inf, F32)
    i1 = jnp.zeros((1, tm), I32)
    for j in range(EXPERTS_PER_GROUP):
        upd = (ev[j] > v1) & (i0 != j)
        i1 = jnp.where(upd, j, i1)
        v1 = jnp.where(upd, ev[j], v1)
    t = jnp.exp(v1 - v0)
    w0 = g_w / (1.0 + t)
    w1 = g_w * t / (1.0 + t)
    e0 = sel * EXPERTS_PER_GROUP + i0
    e1 = sel * EXPERTS_PER_GROUP + i1

    eio = lax.broadcasted_iota(I32, (N_EXPERTS, tm), 0)
    hit0 = eio == e0
    hit1 = eio == e1
    onehot = (hit0 | hit1).astype(F32)
    tile_cnt = jnp.sum(onehot, axis=1, keepdims=True)

    @pl.when((p == 0) & (i == 0))
    def _():
        cnt_scr[...] = jnp.zeros_like(cnt_scr)

    @pl.when(p == 0)
    def _():
        cnt_scr[...] = cnt_scr[...] + tile_cnt

    @pl.when((p == 1) & (i == 0))
    def _():
        cnt = cnt_scr[...]
        padded = jnp.floor((cnt + (MOE_BLOCK - 1)) * (1.0 / MOE_BLOCK)) * MOE_BLOCK
        ri = lax.broadcasted_iota(I32, (N_EXPERTS, N_EXPERTS), 0)
        ci = lax.broadcasted_iota(I32, (N_EXPERTS, N_EXPERTS), 1)
        lower = (ci < ri).astype(F32)
        pad_start = jnp.dot(lower, jnp.broadcast_to(padded, (N_EXPERTS, LANES)),
                            precision=lax.Precision.HIGHEST, preferred_element_type=F32)[:, 0:1]
        base_scr[...] = pad_start
        pad_end = pad_start + padded
        n_active = jnp.broadcast_to(pad_end[N_EXPERTS - 1:N_EXPERTS, :] * (1.0 / MOE_BLOCK), (1, LANES))
        diag = (lax.broadcasted_iota(I32, (N_EXPERTS, LANES), 0)
                == lax.broadcasted_iota(I32, (N_EXPERTS, LANES), 1))
        blk_start = jnp.sum(jnp.where(diag, pad_start * (1.0 / MOE_BLOCK), 0.0), axis=0, keepdims=True)
        blk_count = jnp.sum(jnp.where(diag, padded * (1.0 / MOE_BLOCK), 0.0), axis=0, keepdims=True)
        meta_ref[...] = jnp.concatenate(
            [n_active, blk_start, blk_count, jnp.zeros((5, LANES), F32)], axis=0).astype(I32)

    @pl.when(p == 1)
    def _():
        ui = lax.broadcasted_iota(I32, (tm, tm), 0)
        uj = lax.broadcasted_iota(I32, (tm, tm), 1)
        upper = (ui < uj).astype(BF16)
        excl = _dot(onehot.astype(BF16), upper)
        pos = base_scr[...] + excl
        d0 = jnp.sum(jnp.where(hit0, pos, 0.0), axis=0, keepdims=True)
        d1 = jnp.sum(jnp.where(hit1, pos, 0.0), axis=0, keepdims=True)
        dest_ref[...] = jnp.concatenate([d0, d1, jnp.zeros((6, tm), F32)], axis=0).astype(I32)
        wmat = jnp.concatenate([w0, w1, jnp.zeros((LANES - 2, tm), F32)], axis=0)
        wcol_ref[...] = wmat.T
        base_scr[...] = base_scr[...] + tile_cnt


def _router(lt_p, lt_s):
    tm = TOKEN_TILE
    n_p = lt_p.shape[1] // tm
    n_s = lt_s.shape[1] // tm
    n_tok = lt_p.shape[1] + lt_s.shape[1]
    return pl.pallas_call(
        partial(_router_kernel, n_prompt_tiles=n_p),
        grid=(2, n_p + n_s),
        in_specs=[
            pl.BlockSpec((ROUTER_ROWS, tm), lambda p, i: (0, jnp.minimum(i, n_p - 1))),
            pl.BlockSpec((ROUTER_ROWS, tm), lambda p, i: (0, jnp.maximum(i - n_p, 0))),
        ],
        out_specs=[
            pl.BlockSpec((8, tm), lambda p, i: (0, i * p)),
            pl.BlockSpec((tm, LANES), lambda p, i: (i * p, 0)),
            pl.BlockSpec((8, LANES), lambda p, i: (0, 0)),
        ],
        out_shape=[
            jax.ShapeDtypeStruct((8, n_tok), I32),
            jax.ShapeDtypeStruct((n_tok, LANES), F32),
            jax.ShapeDtypeStruct((8, LANES), I32),
        ],
        scratch_shapes=[pltpu.VMEM((N_EXPERTS, 1), F32), pltpu.VMEM((N_EXPERTS, 1), F32)],
        compiler_params=_cparams(("arbitrary", "arbitrary")),
        name="router",
    )(lt_p, lt_s)


def _dispatch_kernel(d0_ref, d1_ref, bs_ref, nb_ref, na_ref, hp_ref, hs_ref, xb_ref, sem, zbuf, zsem,
                     *, n_prompt_tiles, n_blocks):
    i = pl.program_id(0)
    tm = hp_ref.shape[0]

    @pl.when(i == 0)
    def _():
        zbuf[...] = jnp.zeros_like(zbuf)

        def zero_copy(blk):
            rows = pl.ds(pl.multiple_of(blk * MOE_BLOCK, MOE_BLOCK), MOE_BLOCK)
            return pltpu.make_async_copy(zbuf, xb_ref.at[rows], zsem)

        def last_block(e):
            return bs_ref[e] + jnp.maximum(nb_ref[e], 1) - 1

        def start_e(e, carry):
            @pl.when(nb_ref[e] > 0)
            def _():
                zero_copy(last_block(e)).start()
            return carry

        def wait_e(e, carry):
            @pl.when(nb_ref[e] > 0)
            def _():
                zero_copy(last_block(e)).wait()
            return carry

        def start_b(b, carry):
            zero_copy(b).start()
            return carry

        def wait_b(b, carry):
            zero_copy(b).wait()
            return carry

        lax.fori_loop(0, N_EXPERTS, start_e, 0)
        lax.fori_loop(na_ref[0], n_blocks, start_b, 0)
        lax.fori_loop(0, N_EXPERTS, wait_e, 0)
        lax.fori_loop(na_ref[0], n_blocks, wait_b, 0)

    def scatter(h_ref):
        def copy(r, d_ref):
            return pltpu.make_async_copy(h_ref.at[pl.ds(r, 1)], xb_ref.at[pl.ds(d_ref[i * tm + r], 1)], sem)

        def issue(r, carry):
            copy(r, d0_ref).start(priority=0)
            copy(r, d1_ref).start(priority=1)
            return carry

        def drain(r, carry):
            copy(r, d0_ref).wait()
            copy(r, d1_ref).wait()
            return carry

        lax.fori_loop(0, tm, issue, 0, unroll=ROW_DMA_UNROLL)
        lax.fori_loop(0, tm, drain, 0, unroll=ROW_DMA_UNROLL)

    @pl.when(i < n_prompt_tiles)
    def _():
        scatter(hp_ref)

    @pl.when(i >= n_prompt_tiles)
    def _():
        scatter(hs_ref)


def _dispatch(d0, d1, block_start, block_count, n_active, h2_p, h2_s, n_blocks):
    tm = TOKEN_TILE
    d = h2_p.shape[1]
    n_p = h2_p.shape[0] // tm
    n_s = h2_s.shape[0] // tm
    return pl.pallas_call(
        partial(_dispatch_kernel, n_prompt_tiles=n_p, n_blocks=n_blocks),
        grid_spec=pltpu.PrefetchScalarGridSpec(
            num_scalar_prefetch=5,
            grid=(n_p + n_s,),
            in_specs=[
                pl.BlockSpec((tm, d), lambda i, *_: (jnp.minimum(i, n_p - 1), 0)),
                pl.BlockSpec((tm, d), lambda i, *_: (jnp.maximum(i - n_p, 0), 0)),
            ],
            out_specs=pl.BlockSpec(memory_space=pl.ANY),
            scratch_shapes=[
                pltpu.SemaphoreType.DMA(()),
                pltpu.VMEM((MOE_BLOCK, d), U32),
                pltpu.SemaphoreType.DMA(()),
            ],
        ),
        out_shape=jax.ShapeDtypeStruct((n_blocks * MOE_BLOCK, d), U32),
        compiler_params=_cparams(("arbitrary",)),
        name="dispatch",
    )(d0, d1, block_start, block_count, n_active, h2_p, h2_s)


def _ffn_kernel(bs_ref, nb_ref, na_ref, xb_ref, wg_ref, wu_ref, wd_ref, yb_ref,
                xbuf, ybuf, wg_bf, wu_bf, wd_bf, xsem, ysem, *, n_blocks):
    e = pl.program_id(0)
    nb = nb_ref[e]
    b0 = bs_ref[e]
    na = na_ref[0]
    ring = xbuf.shape[0]

    def x_copy(blk):
        rows = pl.ds(pl.multiple_of(blk * MOE_BLOCK, MOE_BLOCK), MOE_BLOCK)
        slot = lax.rem(blk, ring)
        return pltpu.make_async_copy(xb_ref.at[rows], xbuf.at[slot], xsem.at[slot])

    def y_copy(blk):
        rows = pl.ds(pl.multiple_of(blk * MOE_BLOCK, MOE_BLOCK), MOE_BLOCK)
        slot = lax.rem(blk, ring)
        return pltpu.make_async_copy(ybuf.at[slot], yb_ref.at[rows], ysem.at[slot])

    @pl.when(e == 0)
    def _():
        for ahead in range(ring - 1):
            @pl.when(ahead < na)
            def _():
                x_copy(ahead).start(priority=1)

    @pl.when(nb > 0)
    def _():
        wg_bf[...] = wg_ref[0].astype(BF16)
        wu_bf[...] = wu_ref[0].astype(BF16)
        wd_bf[...] = wd_ref[0].astype(BF16)

    def block(i, carry):
        blk = b0 + i
        slot = lax.rem(blk, ring)
        x_copy(blk).wait()

        @pl.when(blk + ring - 1 < na)
        def _():
            x_copy(blk + ring - 1).start(priority=1)

        x_lo, x_hi = _unpack_bf16_pairs(xbuf[slot])
        half = x_lo.shape[1]
        g = _dot(x_lo, wg_bf[0:half, :]) + _dot(x_hi, wg_bf[half:, :])
        u = _dot(x_lo, wu_bf[0:half, :]) + _dot(x_hi, wu_bf[half:, :])
        a = (g * jax.nn.sigmoid(g) * u).astype(BF16)
        y = _dot(a, wd_bf[...])

        @pl.when(blk >= ring)
        def _():
            y_copy(blk - ring).wait()

        ybuf[slot] = y
        y_copy(blk).start(priority=1)
        return carry

    lax.fori_loop(0, nb, block, 0)

    @pl.when(e == pl.num_programs(0) - 1)
    def _():
        for back in range(1, ring + 1):
            @pl.when(na - back >= 0)
            def _():
                y_copy(na - back).wait()

        ybuf[0] = jnp.zeros(ybuf.shape[1:], F32)

        def zero_block(b, carry):
            rows = pl.ds(pl.multiple_of(b * MOE_BLOCK, MOE_BLOCK), MOE_BLOCK)
            cp = pltpu.make_async_copy(ybuf.at[0], yb_ref.at[rows], ysem.at[0])
            cp.start()
            cp.wait()
            return carry

        lax.fori_loop(na_ref[0], n_blocks, zero_block, 0)


def _ffn(block_start, block_count, n_active, xb, w_gate, w_up, w_down):
    n_rows, d_packed = xb.shape
    n_exp, d, d_e = w_gate.shape
    n_blocks = n_rows // MOE_BLOCK
    return pl.pallas_call(
        partial(_ffn_kernel, n_blocks=n_blocks),
        grid_spec=pltpu.PrefetchScalarGridSpec(
            num_scalar_prefetch=3,
            grid=(n_exp,),
            in_specs=[
                pl.BlockSpec(memory_space=pl.ANY),
                pl.BlockSpec((1, d, d_e), lambda e, *_: (e, 0, 0)),
                pl.BlockSpec((1, d, d_e), lambda e, *_: (e, 0, 0)),
                pl.BlockSpec((1, d_e, d), lambda e, *_: (e, 0, 0)),
            ],
            out_specs=pl.BlockSpec(memory_space=pl.ANY),
            scratch_shapes=[
                pltpu.VMEM((FFN_RING, MOE_BLOCK, d_packed), U32),
                pltpu.VMEM((FFN_RING, MOE_BLOCK, d), F32),
                pltpu.VMEM((d, d_e), BF16),
                pltpu.VMEM((d, d_e), BF16),
                pltpu.VMEM((d_e, d), BF16),
                pltpu.SemaphoreType.DMA((FFN_RING,)),
                pltpu.SemaphoreType.DMA((FFN_RING,)),
            ],
        ),
        out_shape=jax.ShapeDtypeStruct((n_rows, d), F32),
        compiler_params=_cparams(("arbitrary",)),
        name="moe_ffn",
    )(block_start, block_count, n_active, xb, w_gate, w_up, w_down)


def _ple_kernel(d0_ref, d1_ref, x1_ref, wcol_ref, p_ref, yb_ref, gple_ref, wpg_ref, wp_ref, gfin_ref,
                y_ref, g_scr, sem, *, time_major, n_steps):
    i = pl.program_id(0)
    tm = x1_ref.shape[0]
    slot = lax.rem(i, 2) if n_steps > 1 else 0

    def copy(tile, r, k, to_slot):
        d_ref = d0_ref if k == 0 else d1_ref
        return pltpu.make_async_copy(yb_ref.at[pl.ds(d_ref[tile * tm + r], 1)],
                                     g_scr.at[to_slot, k, pl.ds(r, 1)], sem.at[to_slot])

    def issue(tile, to_slot):
        def body(r, carry):
            copy(tile, r, 0, to_slot).start(priority=0)
            copy(tile, r, 1, to_slot).start(priority=1)
            return carry
        lax.fori_loop(0, tm, body, 0, unroll=ROW_DMA_UNROLL)

    def drain(tile, from_slot):
        def body(r, carry):
            copy(tile, r, 0, from_slot).wait()
            copy(tile, r, 1, from_slot).wait()
            return carry
        lax.fori_loop(0, tm, body, 0, unroll=ROW_DMA_UNROLL)

    @pl.when(i == 0)
    def _():
        issue(0, 0)

    drain(i, slot)

    x2 = x1_ref[...] + wcol_ref[:, 0:1] * g_scr[slot, 0] + wcol_ref[:, 1:2] * g_scr[slot, 1]

    if n_steps > 1:
        nxt = jnp.minimum(i + 1, n_steps - 1)
        for r in range(tm):
            copy(nxt, r, 0, 1 - slot).start(priority=0)
            copy(nxt, r, 1, 1 - slot).start(priority=1)
    hn = _rms(x2, gple_ref[...]).astype(BF16)
    gate = jax.nn.sigmoid(_dot(hn, wpg_ref[...]))
    if time_major:
        pd = wp_ref.shape[0]
        pp = jnp.concatenate([p_ref[:, t * pd:(t + 1) * pd] for t in range(p_ref.shape[1] // pd)], axis=0)
    else:
        pp = p_ref[...]
    x3 = x2 + _dot(pp.astype(BF16), wp_ref[...]) * gate
    y = _rms(x3, gfin_ref[...])
    if time_major:
        d = y.shape[1]
        rows = y_ref.shape[0]
        for t in range(y_ref.shape[1] // d):
            y_ref[:, t * d:(t + 1) * d] = y[t * rows:(t + 1) * rows, :]
    else:
        y_ref[...] = y

    if n_steps > 1:
        @pl.when(i == n_steps - 1)
        def _():
            drain(n_steps - 1, 1 - slot)


def _ple(d0, d1, x1, wcol, p2d, yb, g_ple, w_pg, w_p, g_final, *, time_major):
    n_tok, d = x1.shape
    tm = n_tok if time_major else PLE_TILE
    n_steps = n_tok // tm
    pd = w_p.shape[0]
    if time_major:
        n_seq = p2d.shape[0]
        p_spec = pl.BlockSpec(p2d.shape, lambda i, *_: (0, 0))
        y_spec = pl.BlockSpec((n_seq, (n_tok // n_seq) * d), lambda i, *_: (0, 0))
        y_shape = jax.ShapeDtypeStruct((n_seq, (n_tok // n_seq) * d), F32)
    else:
        p_spec = pl.BlockSpec((tm, pd), lambda i, *_: (i, 0))
        y_spec = pl.BlockSpec((tm, d), lambda i, *_: (i, 0))
        y_shape = jax.ShapeDtypeStruct((n_tok, d), F32)
    return pl.pallas_call(
        partial(_ple_kernel, time_major=time_major, n_steps=n_steps),
        grid_spec=pltpu.PrefetchScalarGridSpec(
            num_scalar_prefetch=2,
            grid=(n_steps,),
            in_specs=[
                pl.BlockSpec((tm, d), lambda i, *_: (i, 0)),
                pl.BlockSpec((tm, LANES), lambda i, *_: (i, 0)),
                p_spec,
                pl.BlockSpec(memory_space=pl.ANY),
                pl.BlockSpec((1, d), lambda i, *_: (0, 0)),
                pl.BlockSpec(w_pg.shape, lambda i, *_: (0, 0)),
                pl.BlockSpec(w_p.shape, lambda i, *_: (0, 0)),
                pl.BlockSpec((1, d), lambda i, *_: (0, 0)),
            ],
            out_specs=y_spec,
            scratch_shapes=[
                pltpu.VMEM((min(n_steps, 2), 2, tm, d), F32),
                pltpu.SemaphoreType.DMA((min(n_steps, 2),)),
            ],
        ),
        out_shape=y_shape,
        compiler_params=_cparams(("arbitrary",)),
        name="ple_tm" if time_major else "ple",
    )(d0, d1, x1, wcol, p2d, yb, g_ple, w_pg, w_p, g_final)


def _pad_lanes(v, offset=0):
    row = jnp.zeros((1, LANES), F32)
    return row.at[0, offset:offset + v.shape[0]].set(v.astype(F32))


def kernel(x_prompt, x_sample, state_conv, state_delta, p_prompt, p_sample, g_mix, w_in, w_s, b_s, g_v,
           conv_w, a_log, dt_bias, g_out, w_out, g_ffn, w_group, b_group, w_router, b_router, w_gate, w_up,
           w_down, g_ple, w_ple_gate, w_ple, g_final):
    n_batch, seq, d = x_prompt.shape
    n_seq, n_pos, _ = x_sample.shape
    depth = g_mix.shape[0]
    assert depth == 1
    a_width = A_GROUPS * A_GROUP_DIM
    hd = B_HEADS * B_DIM
    n_main = 2 * a_width + 4 * hd
    n_p = n_batch * seq
    n_s = n_seq * n_pos

    xp = x_prompt.reshape(n_p, d)
    xs = x_sample.reshape(n_seq, n_pos * d)
    l = 0

    w_in_t = jnp.swapaxes(w_in[l], 0, 1)
    w_main = w_in_t[:n_main].astype(BF16)
    w_tail = jnp.pad(w_in_t[n_main:], ((0, LANES - 2 * B_HEADS), (0, 0))).astype(BF16)
    gm = g_mix[l].reshape(1, d)
    causal = jnp.tril(jnp.ones((A_CHUNK, A_CHUNK), dtype=bool))
    ws_tril = jnp.where(causal, w_s[l], 0).astype(BF16)
    b_t = b_s[l].T
    gv = g_v[l].reshape(1, a_width)
    alog_row = _pad_lanes(a_log[l])
    dtb_row = _pad_lanes(dt_bias[l])
    gout = g_out[l].reshape(1, B_DIM)
    w_o = w_out[l].astype(BF16)
    gf = g_ffn[l].reshape(1, d)
    wr = jnp.pad(jnp.concatenate([w_group[l], w_router[l]], axis=1),
                 ((0, 0), (0, LANES - N_GROUPS - N_EXPERTS)))
    wr_hi = wr.astype(BF16)
    wr_t = jnp.concatenate([wr_hi, (wr - wr_hi.astype(F32)).astype(BF16)], axis=1)
    br_col = jnp.broadcast_to(
        jnp.pad(jnp.concatenate([b_group[l], b_router[l]]), (0, ROUTER_ROWS - N_GROUPS - N_EXPERTS))[:, None],
        (ROUTER_ROWS, LANES))
    gp = g_ple[l].reshape(1, d)
    w_pg = w_ple_gate[l].astype(BF16)
    w_p = w_ple[l].astype(BF16)
    gfin = g_final.reshape(1, d)

    proj_p, tail_p, oa_p = _inproj_mixa(xp, gm, w_main, w_tail, ws_tril, b_t, gv, n_main)
    proj_s, tail_s = _inproj(xs, gm, w_main, w_tail, n_main, time_major=True)
    oa_s, va_s = _mixa_sample(proj_s, w_s[l], b_s[l], gv, n_seq, n_pos)
    ob_p, sd_p, cb_p = _gdn_prompt(proj_p, tail_p, conv_w[l], alog_row, dtb_row, gout, n_batch, seq)
    cbuf = state_conv[l].reshape(n_seq, (CONV_W - 1) * 3 * hd)
    ob_s, sd_s, cb_s = _gdn_decode(proj_s, tail_s, cbuf, state_delta[l], conv_w[l], alog_row, dtb_row,
                                   gout, n_seq, n_pos)
    ob_s = ob_s.reshape(n_s, hd)

    x1_p, h2_p, lt_p = _outproj(xp, oa_p, ob_p, w_o, gf, wr_t, br_col, time_major=False)
    x1_s, h2_s, lt_s = _outproj(xs, oa_s, ob_s, w_o, gf, wr_t, br_col, time_major=True)
    dest, wcol, meta = _router(lt_p, lt_s)
    d0, d1 = dest[0], dest[1]
    n_active = meta[0, 0:1]
    block_start = meta[1, :N_EXPERTS]
    block_count = meta[2, :N_EXPERTS]

    n_tok = n_p + n_s
    n_blocks = (n_tok * 2) // MOE_BLOCK + N_EXPERTS
    xb = _dispatch(d0, d1, block_start, block_count, n_active, h2_p, h2_s, n_blocks)
    yb = _ffn(block_start, block_count, n_active, xb, w_gate[l], w_up[l], w_down[l])

    pp = p_prompt[l].reshape(n_p, -1)
    ps = p_sample[l].reshape(n_seq, -1)
    y_p = _ple(d0[:n_p], d1[:n_p], x1_p, wcol[:n_p], pp, yb, gp, w_pg, w_p, gfin, time_major=False)
    y_s = _ple(d0[n_p:], d1[n_p:], x1_s, wcol[n_p:], ps, yb, gp, w_pg, w_p, gfin, time_major=True)

    return (
        y_p.reshape(n_batch, seq, d),
        y_s.reshape(n_seq, n_pos, d),
        cb_p[None],
        sd_p[None],
        cb_s.reshape(n_seq, CONV_W - 1, 3 * hd)[None],
        sd_s[None],
        va_s.reshape(n_seq, n_pos, a_width)[None],
    )
```

```python
from functools import partial

import jax
import jax.numpy as jnp
from jax import lax
from jax.experimental import pallas as pl
from jax.experimental.pallas import tpu as pltpu

F32 = jnp.float32
BF16 = jnp.bfloat16
I32 = jnp.int32
U32 = jnp.uint32

EPS = 1e-6
LANES = 128
A_GROUPS = 8
A_GROUP_DIM = 128
A_CHUNK = 128
B_HEADS = 8
B_DIM = 128
CONV_W = 4
N_GROUPS = 4
EXPERTS_PER_GROUP = 8
N_EXPERTS = N_GROUPS * EXPERTS_PER_GROUP
ROUTER_ROWS = 40
TOKEN_TILE = 512
INPROJ_TILE = 1024
PLE_TILE = 256
GDN_CHUNK = 128
MOE_BLOCK = 128
FFN_RING = 4
DEC_SEQ_TILE = 16
ROW_DMA_UNROLL = 8
VMEM_LIMIT = 56 * 1024 * 1024


def _cparams(sem):
    return pltpu.CompilerParams(dimension_semantics=sem, vmem_limit_bytes=VMEM_LIMIT)


def _rms(x, g):
    return x * lax.rsqrt(jnp.mean(x * x, axis=-1, keepdims=True) + EPS) * g


def _softplus(x):
    return jnp.maximum(x, 0.0) + jnp.log(1.0 + jnp.exp(-jnp.abs(x)))


def _dot(a, b):
    return jnp.dot(a, b, preferred_element_type=F32)


def _dot_nt(a, b, precision=None):
    return lax.dot_general(a, b, (((1,), (1,)), ((), ())), precision=precision,
                           preferred_element_type=F32)


def _dot_tn(a, b):
    return lax.dot_general(a, b, (((0,), (0,)), ((), ())), preferred_element_type=F32)


def _pack_bf16_pairs(x):
    n = x.shape[1] // 2
    lo = lax.bitcast_convert_type(x[:, :n].astype(BF16).astype(F32), U32) >> 16
    hi = lax.bitcast_convert_type(x[:, n:].astype(BF16).astype(F32), U32) & jnp.uint32(0xFFFF0000)
    return hi | lo


def _unpack_pairs_f32(xu):
    lo = lax.bitcast_convert_type(xu << 16, F32)
    hi = lax.bitcast_convert_type(xu & jnp.uint32(0xFFFF0000), F32)
    return lo, hi


def _unpack_bf16_pairs(xu):
    lo, hi = _unpack_pairs_f32(xu)
    return lo.astype(BF16), hi.astype(BF16)


def _inproj_kernel(x_ref, g_ref, w_ref, wt_ref, proj_ref, tail_ref, h_scr, *, time_major):
    j = pl.program_id(1)

    @pl.when(j == 0)
    def _():
        g = g_ref[...]
        if time_major:
            d = g.shape[-1]
            rows = x_ref.shape[0]
            for t in range(x_ref.shape[1] // d):
                h_scr[t * rows:(t + 1) * rows, :] = _rms(x_ref[:, t * d:(t + 1) * d], g).astype(BF16)
        else:
            h_scr[...] = _rms(x_ref[...], g).astype(BF16)
        tail_ref[...] = _dot_nt(h_scr[...], wt_ref[...])

    proj_ref[...] = _dot_nt(h_scr[...], w_ref[...])


def _inproj(x2d, g_mix, w_main, w_tail, n_main, *, time_major, col_tile=1024):
    d = g_mix.shape[-1]
    if time_major:
        n_tok = x2d.shape[0] * (x2d.shape[1] // d)
        tm = n_tok
        x_spec = pl.BlockSpec(x2d.shape, lambda i, j: (0, 0))
    else:
        n_tok = x2d.shape[0]
        tm = min(INPROJ_TILE, n_tok)
        x_spec = pl.BlockSpec((tm, d), lambda i, j: (i, 0))
    grid = (n_tok // tm, n_main // col_tile)
    return pl.pallas_call(
        partial(_inproj_kernel, time_major=time_major),
        grid=grid,
        in_specs=[
            x_spec,
            pl.BlockSpec((1, d), lambda i, j: (0, 0)),
            pl.BlockSpec((col_tile, d), lambda i, j: (j, 0)),
            pl.BlockSpec((LANES, d), lambda i, j: (0, 0)),
        ],
        out_specs=[
            pl.BlockSpec((tm, col_tile), lambda i, j: (i, j)),
            pl.BlockSpec((tm, LANES), lambda i, j: (i, 0)),
        ],
        out_shape=[
            jax.ShapeDtypeStruct((n_tok, n_main), F32),
            jax.ShapeDtypeStruct((n_tok, LANES), F32),
        ],
        scratch_shapes=[pltpu.VMEM((tm, d), BF16)],
        compiler_params=_cparams(("parallel", "arbitrary")),
        name="inproj_tm" if time_major else "inproj",
    )(x2d, g_mix, w_main, w_tail)


def _group_ln(v, g):
    mu = jnp.mean(v, axis=-1, keepdims=True)
    dlt = v - mu
    var = jnp.mean(dlt * dlt, axis=-1, keepdims=True)
    return dlt * lax.rsqrt(var + EPS) * g


def _inproj_mixa_kernel(x_ref, g_ref, w_ref, wt_ref, ws_ref, bt_ref, gv_ref,
                        proj_ref, tail_ref, oa_ref, h_scr, u_scr, v_scr, *, n_later):
    j = pl.program_id(1)
    tm = h_scr.shape[0]

    @pl.when(j == 0)
    def _():
        h_scr[...] = _rms(x_ref[...], g_ref[...]).astype(BF16)
        tail_ref[...] = _dot_nt(h_scr[...], wt_ref[...])
        u_scr[...] = _dot_nt(h_scr[...], w_ref[...])

    @pl.when(j == 1)
    def _():
        v_scr[...] = _dot_nt(h_scr[...], w_ref[...])

    @pl.when(j >= 2)
    def _():
        proj_ref[...] = _dot_nt(h_scr[...], w_ref[...])
        share = tm // n_later
        r0 = (j - 2) * share
        for c in range(share // A_CHUNK):
            rows = pl.ds(pl.multiple_of(r0 + c * A_CHUNK, A_CHUNK), A_CHUNK)
            for h in range(A_GROUPS):
                cols = slice(h * A_GROUP_DIM, (h + 1) * A_GROUP_DIM)
                u = jax.nn.gelu(u_scr[rows, cols])
                v = _group_ln(jax.nn.gelu(v_scr[rows, cols]), gv_ref[:, cols])
                mixed = _dot(ws_ref[h], v.astype(BF16)) + bt_ref[:, h:h + 1]
                oa_ref[rows, cols] = (u * mixed).astype(BF16)


def _inproj_mixa(x2d, g_mix, w_main, w_tail, ws_tril, b_t, g_v, n_main):
    n_tok, d = x2d.shape
    a_width = A_GROUPS * A_GROUP_DIM
    tm = min(INPROJ_TILE, n_tok)
    n_col = n_main // a_width
    return pl.pallas_call(
        partial(_inproj_mixa_kernel, n_later=n_col - 2),
        grid=(n_tok // tm, n_col),
        in_specs=[
            pl.BlockSpec((tm, d), lambda i, j: (i, 0)),
            pl.BlockSpec((1, d), lambda i, j: (0, 0)),
            pl.BlockSpec((a_width, d), lambda i, j: (j, 0)),
            pl.BlockSpec((LANES, d), lambda i, j: (0, 0)),
            pl.BlockSpec((A_GROUPS, A_CHUNK, A_CHUNK), lambda i, j: (0, 0, 0)),
            pl.BlockSpec((A_CHUNK, A_GROUPS), lambda i, j: (0, 0)),
            pl.BlockSpec((1, a_width), lambda i, j: (0, 0)),
        ],
        out_specs=[
            pl.BlockSpec((tm, a_width), lambda i, j: (i, jnp.maximum(j - 2, 0))),
            pl.BlockSpec((tm, LANES), lambda i, j: (i, 0)),
            pl.BlockSpec((tm, a_width), lambda i, j: (i, 0)),
        ],
        out_shape=[
            jax.ShapeDtypeStruct((n_tok, n_main - 2 * a_width), F32),
            jax.ShapeDtypeStruct((n_tok, LANES), F32),
            jax.ShapeDtypeStruct((n_tok, a_width), BF16),
        ],
        scratch_shapes=[pltpu.VMEM((tm, d), BF16), pltpu.VMEM((tm, a_width), F32),
                        pltpu.VMEM((tm, a_width), F32)],
        compiler_params=_cparams(("parallel", "arbitrary")),
        name="inproj_mixa",
    )(x2d, g_mix, w_main, w_tail, ws_tril, b_t, g_v)


def _mixa_sample_kernel(ws_ref, bs_ref, au_ref, av_ref, gv_ref, out_ref, va_ref, *, n_seq, n_pos):
    a_width = A_GROUPS * A_GROUP_DIM
    vs = []
    for t in range(n_pos):
        rows = slice(t * n_seq, (t + 1) * n_seq)
        per_group = []
        for h in range(A_GROUPS):
            cols = slice(h * A_GROUP_DIM, (h + 1) * A_GROUP_DIM)
            per_group.append(_group_ln(jax.nn.gelu(av_ref[rows, cols]), gv_ref[:, cols]))
        vs.append(per_group)
        for h in range(A_GROUPS):
            va_ref[:, t * a_width + h * A_GROUP_DIM:t * a_width + (h + 1) * A_GROUP_DIM] = per_group[h]
    for t in range(n_pos):
        rows = slice(t * n_seq, (t + 1) * n_seq)
        for h in range(A_GROUPS):
            cols = slice(h * A_GROUP_DIM, (h + 1) * A_GROUP_DIM)
            mixed = jnp.zeros((n_seq, A_GROUP_DIM), F32) + bs_ref[h * n_pos + t]
            for s in range(t + 1):
                mixed = mixed + ws_ref[(h * n_pos + t) * n_pos + s] * vs[s][h]
            out_ref[rows, cols] = (jax.nn.gelu(au_ref[rows, cols]) * mixed).astype(BF16)


def _mixa_sample(proj, w_s, b_s, g_v, n_seq, n_pos):
    a_width = A_GROUPS * A_GROUP_DIM
    n_tok = n_seq * n_pos
    return pl.pallas_call(
        partial(_mixa_sample_kernel, n_seq=n_seq, n_pos=n_pos),
        grid_spec=pltpu.PrefetchScalarGridSpec(
            num_scalar_prefetch=2,
            grid=(1,),
            in_specs=[
                pl.BlockSpec((n_tok, a_width), lambda i, *_: (0, 0)),
                pl.BlockSpec((n_tok, a_width), lambda i, *_: (0, 1)),
                pl.BlockSpec((1, a_width), lambda i, *_: (0, 0)),
            ],
            out_specs=[
                pl.BlockSpec((n_tok, a_width), lambda i, *_: (0, 0)),
                pl.BlockSpec((n_seq, n_pos * a_width), lambda i, *_: (0, 0)),
            ],
        ),
        out_shape=[
            jax.ShapeDtypeStruct((n_tok, a_width), BF16),
            jax.ShapeDtypeStruct((n_seq, n_pos * a_width), F32),
        ],
        compiler_params=_cparams(("arbitrary",)),
        name="mixa_sample",
    )(w_s[:, :n_pos, :n_pos].reshape(-1), b_s[:, :n_pos].reshape(-1), proj, proj, g_v)


def _gdn_prompt_kernel(q_ref, k_ref, v_ref, z_ref, tail_ref, cw_ref, alog_ref, dtb_ref, gout_ref,
                       out_ref, s_out_ref, conv_out_ref, s_scr, xbuf):
    c = pl.program_id(1)
    n_c = pl.num_programs(1)
    C = q_ref.shape[0]
    hd = B_HEADS * B_DIM
    pad = 8
    lo = pad - (CONV_W - 1)

    @pl.when(c == 0)
    def _():
        s_scr[...] = jnp.zeros_like(s_scr)
        xbuf[0:pad, :] = jnp.zeros((pad, 3 * hd), F32)

    xbuf[pad:pad + C, 0:hd] = q_ref[...]
    xbuf[pad:pad + C, hd:2 * hd] = k_ref[...]
    xbuf[pad:pad + C, 2 * hd:3 * hd] = v_ref[...]
    y = xbuf[lo:lo + C, :] * cw_ref[0:1, :]
    for j in range(1, CONV_W):
        y = y + xbuf[lo + j:lo + j + C, :] * cw_ref[j:j + 1, :]
    y = y * jax.nn.sigmoid(y)
    last = xbuf[pad + C - (CONV_W - 1):pad + C, :]
    xbuf[lo:pad, :] = last

    @pl.when(c == n_c - 1)
    def _():
        conv_out_ref[0] = last

    tail = tail_ref[...]
    g_all = -jnp.exp(alog_ref[...]) * _softplus(tail + dtb_ref[...])
    beta_all = jax.nn.sigmoid(tail)
    ri = lax.broadcasted_iota(I32, (C, C), 0)
    ci = lax.broadcasted_iota(I32, (C, C), 1)
    causal = ci <= ri
    strict = ci < ri
    tril = causal.astype(F32)
    G_all = jnp.dot(tril, g_all, precision=lax.Precision.HIGHEST, preferred_element_type=F32)
    G_rows = jnp.concatenate([G_all, jnp.zeros((LANES - C, LANES), F32)], axis=0).T if C < LANES else G_all.T
    scale = B_DIM ** -0.5
    n_levels = max(C.bit_length() - 2, 0)
    heads = range(B_HEADS)

    qs, ks, Gs, eGs, rhss, qks, Ps = [], [], [], [], [], [], []
    for h in heads:
        q = y[:, h * B_DIM:(h + 1) * B_DIM]
        k = y[:, hd + h * B_DIM:hd + (h + 1) * B_DIM]
        v = y[:, 2 * hd + h * B_DIM:2 * hd + (h + 1) * B_DIM]
        q = q * lax.rsqrt(jnp.sum(q * q, axis=-1, keepdims=True) + EPS) * scale
        k = k * lax.rsqrt(jnp.sum(k * k, axis=-1, keepdims=True) + EPS)
        G = G_all[:, h:h + 1]
        beta = beta_all[:, B_HEADS + h:B_HEADS + h + 1]
        eG = jnp.exp(G)
        decay = jnp.exp(jnp.where(causal, G - G_rows[h:h + 1, 0:C], -jnp.inf))
        qk_kk = _dot_nt(jnp.concatenate([q, k], axis=0), k)
        qks.append(qk_kk[:C] * decay)
        Ps.append(jnp.where(strict, -(beta * qk_kk[C:] * decay), 0.0))
        rhss.append(jnp.concatenate([v * beta, k * (beta * eG)], axis=1))
        qs.append(q)
        ks.append(k)
        Gs.append(G)
        eGs.append(eG)

    Ls = list(Ps)
    if n_levels > 0:
        Ps = [_dot(P, P) for P in Ps]
    for lvl in range(1, n_levels + 1):
        for h in heads:
            if lvl < n_levels:
                R = _dot(jnp.concatenate([Ps[h], Ls[h]], axis=0), Ps[h])
                Ls[h] = Ls[h] + Ps[h] + R[C:]
                Ps[h] = R[:C]
            else:
                Ls[h] = Ls[h] + Ps[h] + _dot(Ls[h], Ps[h])

    uws = [rhss[h] + _dot(Ls[h], rhss[h]) for h in heads]
    Ss = [s_scr[h] for h in heads]
    rs = [_dot(jnp.concatenate([uws[h][:, B_DIM:], qs[h] * eGs[h]], axis=0), Ss[h]) for h in heads]
    v_news = [uws[h][:, :B_DIM] - rs[h][:C] for h in heads]
    os_ = [rs[h][C:] + _dot(qks[h], v_news[h]) for h in heads]
    for h in heads:
        G_last = Gs[h][C - 1:C, :]
        kd = ks[h] * jnp.exp(G_last - Gs[h])
        s_scr[h] = Ss[h] * jnp.exp(G_last) + _dot_tn(kd, v_news[h])
    for h in heads:
        cols = slice(h * B_DIM, (h + 1) * B_DIM)
        o = _rms(os_[h], gout_ref[...])
        z = z_ref[:, cols]
        out_ref[:, cols] = (o * (z * jax.nn.sigmoid(z))).astype(BF16)

    @pl.when(c == n_c - 1)
    def _():
        s_out_ref[0] = s_scr[...]


def _gdn_prompt(proj, tail, conv_w, alog_row, dtb_row, g_out, n_batch, seq):
    C = GDN_CHUNK
    n_c = seq // C
    hd = B_HEADS * B_DIM
    row = lambda b, c: b * n_c + c
    return pl.pallas_call(
        _gdn_prompt_kernel,
        grid=(n_batch, n_c),
        in_specs=[
            pl.BlockSpec((C, hd), lambda b, c: (row(b, c), 0)),
            pl.BlockSpec((C, hd), lambda b, c: (row(b, c), 1)),
            pl.BlockSpec((C, hd), lambda b, c: (row(b, c), 2)),
            pl.BlockSpec((C, hd), lambda b, c: (row(b, c), 3)),
            pl.BlockSpec((C, LANES), lambda b, c: (row(b, c), 0)),
            pl.BlockSpec((CONV_W, 3 * hd), lambda b, c: (0, 0)),
            pl.BlockSpec((1, LANES), lambda b, c: (0, 0)),
            pl.BlockSpec((1, LANES), lambda b, c: (0, 0)),
            pl.BlockSpec((1, B_DIM), lambda b, c: (0, 0)),
        ],
        out_specs=[
            pl.BlockSpec((C, hd), lambda b, c: (row(b, c), 0)),
            pl.BlockSpec((1, B_HEADS, B_DIM, B_DIM), lambda b, c: (b, 0, 0, 0)),
            pl.BlockSpec((1, CONV_W - 1, 3 * hd), lambda b, c: (b, 0, 0)),
        ],
        out_shape=[
            jax.ShapeDtypeStruct((n_batch * seq, hd), BF16),
            jax.ShapeDtypeStruct((n_batch, B_HEADS, B_DIM, B_DIM), F32),
            jax.ShapeDtypeStruct((n_batch, CONV_W - 1, 3 * hd), F32),
        ],
        scratch_shapes=[
            pltpu.VMEM((B_HEADS, B_DIM, B_DIM), F32),
            pltpu.VMEM((8 + C, 3 * hd), F32),
        ],
        compiler_params=_cparams(("parallel", "arbitrary")),
        name="gdn_prompt",
    )(proj, proj, proj, proj, tail, conv_w, alog_row, dtb_row, g_out)


def _gdn_decode_kernel(*refs, n_pos):
    proj_refs = refs[:n_pos]
    tail_refs = refs[n_pos:2 * n_pos]
    (cbuf_ref, s_ref, cw_ref, alog_ref, dtb_ref, gout_ref,
     out_ref, s_out_ref, conv_out_ref, lhs_scr, res_scr, kd_scr, vn_scr, gl_scr) = refs[2 * n_pos:]
    SB = s_ref.shape[0]
    hd = B_HEADS * B_DIM
    qkv0 = 2 * hd
    z0 = qkv0 + 3 * hd
    scale = B_DIM ** -0.5

    xp = [cbuf_ref[:, j * 3 * hd:(j + 1) * 3 * hd] for j in range(CONV_W - 1)]
    xp += [proj_refs[t][:, qkv0:qkv0 + 3 * hd] for t in range(n_pos)]
    for j in range(CONV_W - 1):
        conv_out_ref[:, j * 3 * hd:(j + 1) * 3 * hd] = xp[n_pos + j]
    ys = []
    for t in range(n_pos):
        y = xp[t] * cw_ref[0:1, :]
        for j in range(1, CONV_W):
            y = y + xp[t + j] * cw_ref[j:j + 1, :]
        ys.append(y * jax.nn.sigmoid(y))
    gs, betas = [], []
    for t in range(n_pos):
        tail = tail_refs[t][...]
        gs.append(-jnp.exp(alog_ref[...]) * _softplus(tail + dtb_ref[...]))
        betas.append(jax.nn.sigmoid(tail))

    kd_scr[...] = jnp.zeros_like(kd_scr)
    vn_scr[...] = jnp.zeros_like(vn_scr)
    heads = range(B_HEADS)
    stride = 2 * n_pos

    qs, ks, Gs, us = [], [], [], []
    for h in heads:
        q_h, k_h, v_h, G_h, b_h = [], [], [], [], []
        G = None
        for t in range(n_pos):
            q = ys[t][:, h * B_DIM:(h + 1) * B_DIM]
            k = ys[t][:, hd + h * B_DIM:hd + (h + 1) * B_DIM]
            q_h.append(q * lax.rsqrt(jnp.sum(q * q, axis=-1, keepdims=True) + EPS) * scale)
            k_h.append(k * lax.rsqrt(jnp.sum(k * k, axis=-1, keepdims=True) + EPS))
            v_h.append(ys[t][:, 2 * hd + h * B_DIM:2 * hd + (h + 1) * B_DIM])
            g = gs[t][:, h:h + 1]
            G = g if G is None else G + g
            G_h.append(G)
            b_h.append(betas[t][:, B_HEADS + h:B_HEADS + h + 1])
        u_h, w_h = [], []
        for t in range(n_pos):
            u = v_h[t] * b_h[t]
            w = k_h[t] * (b_h[t] * jnp.exp(G_h[t]))
            for s in range(t):
                a = b_h[t] * jnp.sum(k_h[t] * k_h[s], axis=-1, keepdims=True) * jnp.exp(G_h[t] - G_h[s])
                u = u - a * u_h[s]
                w = w - a * w_h[s]
            u_h.append(u)
            w_h.append(w)
        for t in range(n_pos):
            lhs_scr[h, pl.ds(t, SB, stride=stride), :] = w_h[t]
            lhs_scr[h, pl.ds(n_pos + t, SB, stride=stride), :] = q_h[t] * jnp.exp(G_h[t])
        qs.append(q_h)
        ks.append(k_h)
        Gs.append(G_h)
        us.append(u_h)

    for h in heads:
        for s in range(SB):
            rows = slice(s * stride, (s + 1) * stride)
            res_scr[h, rows, :] = _dot(lhs_scr[h, rows, :], s_ref[s, h])

    for h in heads:
        cols = slice(h * B_DIM, (h + 1) * B_DIM)
        v_news = [us[h][t] - res_scr[h, pl.ds(t, SB, stride=stride), :] for t in range(n_pos)]
        G_last = Gs[h][n_pos - 1]
        for t in range(n_pos):
            o = res_scr[h, pl.ds(n_pos + t, SB, stride=stride), :]
            for s in range(t + 1):
                qk = jnp.sum(qs[h][t] * ks[h][s], axis=-1, keepdims=True) * jnp.exp(Gs[h][t] - Gs[h][s])
                o = o + qk * v_news[s]
            o = _rms(o, gout_ref[...])
            z = proj_refs[t][:, z0 + h * B_DIM:z0 + (h + 1) * B_DIM]
            out_ref[t, :, cols] = (o * (z * jax.nn.sigmoid(z))).astype(BF16)
            kd_scr[h, pl.ds(t, SB, stride=stride), :] = ks[h][t] * jnp.exp(G_last - Gs[h][t])
            vn_scr[h, pl.ds(t, SB, stride=stride), :] = v_news[t]
        gl_scr[h] = jnp.broadcast_to(jnp.exp(G_last), (SB, B_DIM))

    for h in heads:
        for s in range(SB):
            rows = slice(s * stride, (s + 1) * stride)
            s_out_ref[s, h] = (s_ref[s, h] * gl_scr[h, s:s + 1, 0:1]
                               + _dot_tn(kd_scr[h, rows, :], vn_scr[h, rows, :]))


def _gdn_decode(proj_s, tail_s, cbuf, s0, conv_w, alog_row, dtb_row, g_out, n_seq, n_pos):
    SB = DEC_SEQ_TILE
    hd = B_HEADS * B_DIM
    n_main = proj_s.shape[1]
    per_pos = n_seq // SB
    proj_specs = [pl.BlockSpec((SB, n_main), lambda i, t=t: (t * per_pos + i, 0)) for t in range(n_pos)]
    tail_specs = [pl.BlockSpec((SB, LANES), lambda i, t=t: (t * per_pos + i, 0)) for t in range(n_pos)]
    return pl.pallas_call(
        partial(_gdn_decode_kernel, n_pos=n_pos),
        grid=(per_pos,),
        in_specs=proj_specs + tail_specs + [
            pl.BlockSpec((SB, (CONV_W - 1) * 3 * hd), lambda i: (i, 0)),
            pl.BlockSpec((SB, B_HEADS, B_DIM, B_DIM), lambda i: (i, 0, 0, 0)),
            pl.BlockSpec((CONV_W, 3 * hd), lambda i: (0, 0)),
            pl.BlockSpec((1, LANES), lambda i: (0, 0)),
            pl.BlockSpec((1, LANES), lambda i: (0, 0)),
            pl.BlockSpec((1, B_DIM), lambda i: (0, 0)),
        ],
        out_specs=[
            pl.BlockSpec((n_pos, SB, hd), lambda i: (0, i, 0)),
            pl.BlockSpec((SB, B_HEADS, B_DIM, B_DIM), lambda i: (i, 0, 0, 0)),
            pl.BlockSpec((SB, (CONV_W - 1) * 3 * hd), lambda i: (i, 0)),
        ],
        out_shape=[
            jax.ShapeDtypeStruct((n_pos, n_seq, hd), BF16),
            jax.ShapeDtypeStruct((n_seq, B_HEADS, B_DIM, B_DIM), F32),
            jax.ShapeDtypeStruct((n_seq, (CONV_W - 1) * 3 * hd), F32),
        ],
        scratch_shapes=[
            pltpu.VMEM((B_HEADS, SB * 2 * n_pos, B_DIM), F32),
            pltpu.VMEM((B_HEADS, SB * 2 * n_pos, B_DIM), F32),
            pltpu.VMEM((B_HEADS, SB * 2 * n_pos, B_DIM), F32),
            pltpu.VMEM((B_HEADS, SB * 2 * n_pos, B_DIM), F32),
            pltpu.VMEM((B_HEADS, SB, B_DIM), F32),
        ],
        compiler_params=_cparams(("parallel",)),
        name="gdn_decode",
    )(*([proj_s] * n_pos), *([tail_s] * n_pos), cbuf, s0, conv_w, alog_row, dtb_row, g_out)


def _outproj_kernel(x_ref, oa_ref, ob_ref, w_ref, g_ref, wr_ref, br_ref,
                    x1_ref, h2_ref, lt_ref, *, time_major):
    a_width = oa_ref.shape[1]
    acc = _dot(oa_ref[...], w_ref[0:a_width, :]) + _dot(ob_ref[...], w_ref[a_width:, :])
    if time_major:
        d = g_ref.shape[-1]
        rows = x_ref.shape[0]
        x = jnp.concatenate([x_ref[:, t * d:(t + 1) * d] for t in range(x_ref.shape[1] // d)], axis=0)
    else:
        x = x_ref[...]
    x1 = x + acc
    x1_ref[...] = x1
    h2 = _rms(x1, g_ref[...])
    h2_ref[...] = _pack_bf16_pairs(h2)
    h_hi = h2.astype(BF16)
    h_lo = (h2 - h_hi.astype(F32)).astype(BF16)
    tm = h2.shape[0]
    lg = _dot(jnp.concatenate([h_hi, h_lo], axis=0), wr_ref[...])
    lg = lg[:tm, :LANES] + lg[:tm, LANES:] + lg[tm:, :LANES] + lg[tm:, LANES:]
    lt_ref[...] = lg.T[0:ROUTER_ROWS, :] + br_ref[:, 0:1]


def _outproj(x2d, out_a, out_b, w_out, g_ffn, wr_t, br_col, *, time_major):
    d = g_ffn.shape[-1]
    a_width = out_a.shape[1]
    n_tok = out_a.shape[0]
    tm = TOKEN_TILE
    if time_major:
        x_spec = pl.BlockSpec(x2d.shape, lambda i: (0, 0))
    else:
        x_spec = pl.BlockSpec((tm, d), lambda i: (i, 0))
    return pl.pallas_call(
        partial(_outproj_kernel, time_major=time_major),
        grid=(n_tok // tm,),
        in_specs=[
            x_spec,
            pl.BlockSpec((tm, a_width), lambda i: (i, 0)),
            pl.BlockSpec((tm, out_b.shape[1]), lambda i: (i, 0)),
            pl.BlockSpec(w_out.shape, lambda i: (0, 0)),
            pl.BlockSpec((1, d), lambda i: (0, 0)),
            pl.BlockSpec(wr_t.shape, lambda i: (0, 0)),
            pl.BlockSpec(br_col.shape, lambda i: (0, 0)),
        ],
        out_specs=[
            pl.BlockSpec((tm, d), lambda i: (i, 0)),
            pl.BlockSpec((tm, d // 2), lambda i: (i, 0)),
            pl.BlockSpec((ROUTER_ROWS, tm), lambda i: (0, i)),
        ],
        out_shape=[
            jax.ShapeDtypeStruct((n_tok, d), F32),
            jax.ShapeDtypeStruct((n_tok, d // 2), U32),
            jax.ShapeDtypeStruct((ROUTER_ROWS, n_tok), F32),
        ],
        compiler_params=_cparams(("parallel",)),
        name="outproj_tm" if time_major else "outproj",
    )(x2d, out_a, out_b, w_out, g_ffn, wr_t, br_col)


def _router_kernel(lp_ref, ls_ref, dest_ref, wcol_ref, meta_ref, cnt_scr, base_scr, *, n_prompt_tiles):
    p = pl.program_id(0)
    i = pl.program_id(1)
    tm = lp_ref.shape[1]
    lt = jnp.where(i < n_prompt_tiles, lp_ref[...], ls_ref[...])

    m = lt[0:1, :]
    sel = jnp.zeros((1, tm), I32)
    for r in range(1, N_GROUPS):
        upd = lt[r:r + 1, :] > m
        sel = jnp.where(upd, r, sel)
        m = jnp.where(upd, lt[r:r + 1, :], m)
    den = jnp.zeros((1, tm), F32)
    for r in range(N_GROUPS):
        den = den + jnp.exp(lt[r:r + 1, :] - m)
    g_w = 1.0 / den

    ev = []
    for j in range(EXPERTS_PER_GROUP):
        e = jnp.zeros((1, tm), F32)
        for g in range(N_GROUPS):
            row = N_GROUPS + g * EXPERTS_PER_GROUP + j
            e = jnp.where(sel == g, lt[row:row + 1, :], e)
        ev.append(e)
    v0 = ev[0]
    i0 = jnp.zeros((1, tm), I32)
    for j in range(1, EXPERTS_PER_GROUP):
        upd = ev[j] > v0
        i0 = jnp.where(upd, j, i0)
        v0 = jnp.where(upd, ev[j], v0)
    v1 = jnp.full((1, tm), -jnp.inf, F32)
    i1 = jnp.zeros((1, tm), I32)
    for j in range(EXPERTS_PER_GROUP):
        upd = (ev[j] > v1) & (i0 != j)
        i1 = jnp.where(upd, j, i1)
        v1 = jnp.where(upd, ev[j], v1)
    t = jnp.exp(v1 - v0)
    w0 = g_w / (1.0 + t)
    w1 = g_w * t / (1.0 + t)
    e0 = sel * EXPERTS_PER_GROUP + i0
    e1 = sel * EXPERTS_PER_GROUP + i1

    eio = lax.broadcasted_iota(I32, (N_EXPERTS, tm), 0)
    hit0 = eio == e0
    hit1 = eio == e1
    onehot = (hit0 | hit1).astype(F32)
    tile_cnt = jnp.sum(onehot, axis=1, keepdims=True)

    @pl.when((p == 0) & (i == 0))
    def _():
        cnt_scr[...] = jnp.zeros_like(cnt_scr)

    @pl.when(p == 0)
    def _():
        cnt_scr[...] = cnt_scr[...] + tile_cnt

    @pl.when((p == 1) & (i == 0))
    def _():
        cnt = cnt_scr[...]
        padded = jnp.floor((cnt + (MOE_BLOCK - 1)) * (1.0 / MOE_BLOCK)) * MOE_BLOCK
        ri = lax.broadcasted_iota(I32, (N_EXPERTS, N_EXPERTS), 0)
        ci = lax.broadcasted_iota(I32, (N_EXPERTS, N_EXPERTS), 1)
        lower = (ci < ri).astype(F32)
        pad_start = jnp.dot(lower, jnp.broadcast_to(padded, (N_EXPERTS, LANES)),
                            precision=lax.Precision.HIGHEST, preferred_element_type=F32)[:, 0:1]
        base_scr[...] = pad_start
        pad_end = pad_start + padded
        n_active = jnp.broadcast_to(pad_end[N_EXPERTS - 1:N_EXPERTS, :] * (1.0 / MOE_BLOCK), (1, LANES))
        diag = (lax.broadcasted_iota(I32, (N_EXPERTS, LANES), 0)
                == lax.broadcasted_iota(I32, (N_EXPERTS, LANES), 1))
        blk_start = jnp.sum(jnp.where(diag, pad_start * (1.0 / MOE_BLOCK), 0.0), axis=0, keepdims=True)
        blk_count = jnp.sum(jnp.where(diag, padded * (1.0 / MOE_BLOCK), 0.0), axis=0, keepdims=True)
        meta_ref[...] = jnp.concatenate(
            [n_active, blk_start, blk_count, jnp.zeros((5, LANES), F32)], axis=0).astype(I32)

    @pl.when(p == 1)
    def _():
        ui = lax.broadcasted_iota(I32, (tm, tm), 0)
        uj = lax.broadcasted_iota(I32, (tm, tm), 1)
        upper = (ui < uj).astype(BF16)
        excl = _dot(onehot.astype(BF16), upper)
        pos = base_scr[...] + excl
        d0 = jnp.sum(jnp.where(hit0, pos, 0.0), axis=0, keepdims=True)
        d1 = jnp.sum(jnp.where(hit1, pos, 0.0), axis=0, keepdims=True)
        dest_ref[...] = jnp.concatenate([d0, d1, jnp.zeros((6, tm), F32)], axis=0).astype(I32)
        wmat = jnp.concatenate([w0, w1, jnp.zeros((LANES - 2, tm), F32)], axis=0)
        wcol_ref[...] = wmat.T
        base_scr[...] = base_scr[...] + tile_cnt


def _router(lt_p, lt_s):
    tm = TOKEN_TILE
    n_p = lt_p.shape[1] // tm
    n_s = lt_s.shape[1] // tm
    n_tok = lt_p.shape[1] + lt_s.shape[1]
    return pl.pallas_call(
        partial(_router_kernel, n_prompt_tiles=n_p),
        grid=(2, n_p + n_s),
        in_specs=[
            pl.BlockSpec((ROUTER_ROWS, tm), lambda p, i: (0, jnp.minimum(i, n_p - 1))),
            pl.BlockSpec((ROUTER_ROWS, tm), lambda p, i: (0, jnp.maximum(i - n_p, 0))),
        ],
        out_specs=[
            pl.BlockSpec((8, tm), lambda p, i: (0, i * p)),
            pl.BlockSpec((tm, LANES), lambda p, i: (i * p, 0)),
            pl.BlockSpec((8, LANES), lambda p, i: (0, 0)),
        ],
        out_shape=[
            jax.ShapeDtypeStruct((8, n_tok), I32),
            jax.ShapeDtypeStruct((n_tok, LANES), F32),
            jax.ShapeDtypeStruct((8, LANES), I32),
        ],
        scratch_shapes=[pltpu.VMEM((N_EXPERTS, 1), F32), pltpu.VMEM((N_EXPERTS, 1), F32)],
        compiler_params=_cparams(("arbitrary", "arbitrary")),
        name="router",
    )(lt_p, lt_s)


def _dispatch_kernel(d0_ref, d1_ref, bs_ref, nb_ref, na_ref, hp_ref, hs_ref, xb_ref, sem, zbuf, zsem,
                     *, n_prompt_tiles, n_blocks):
    i = pl.program_id(0)
    tm = hp_ref.shape[0]

    @pl.when(i == 0)
    def _():
        zbuf[...] = jnp.zeros_like(zbuf)

        def zero_copy(blk):
            rows = pl.ds(pl.multiple_of(blk * MOE_BLOCK, MOE_BLOCK), MOE_BLOCK)
            return pltpu.make_async_copy(zbuf, xb_ref.at[rows], zsem)

        def last_block(e):
            return bs_ref[e] + jnp.maximum(nb_ref[e], 1) - 1

        def start_e(e, carry):
            @pl.when(nb_ref[e] > 0)
            def _():
                zero_copy(last_block(e)).start()
            return carry

        def wait_e(e, carry):
            @pl.when(nb_ref[e] > 0)
            def _():
                zero_copy(last_block(e)).wait()
            return carry

        def start_b(b, carry):
            zero_copy(b).start()
            return carry

        def wait_b(b, carry):
            zero_copy(b).wait()
            return carry

        lax.fori_loop(0, N_EXPERTS, start_e, 0)
        lax.fori_loop(na_ref[0], n_blocks, start_b, 0)
        lax.fori_loop(0, N_EXPERTS, wait_e, 0)
        lax.fori_loop(na_ref[0], n_blocks, wait_b, 0)

    def scatter(h_ref):
        def copy(r, d_ref):
            return pltpu.make_async_copy(h_ref.at[pl.ds(r, 1)], xb_ref.at[pl.ds(d_ref[i * tm + r], 1)], sem)

        def issue(r, carry):
            copy(r, d0_ref).start(priority=0)
            copy(r, d1_ref).start(priority=1)
            return carry

        def drain(r, carry):
            copy(r, d0_ref).wait()
            copy(r, d1_ref).wait()
            return carry

        lax.fori_loop(0, tm, issue, 0, unroll=ROW_DMA_UNROLL)
        lax.fori_loop(0, tm, drain, 0, unroll=ROW_DMA_UNROLL)

    @pl.when(i < n_prompt_tiles)
    def _():
        scatter(hp_ref)

    @pl.when(i >= n_prompt_tiles)
    def _():
        scatter(hs_ref)


def _dispatch(d0, d1, block_start, block_count, n_active, h2_p, h2_s, n_blocks):
    tm = TOKEN_TILE
    d = h2_p.shape[1]
    n_p = h2_p.shape[0] // tm
    n_s = h2_s.shape[0] // tm
    return pl.pallas_call(
        partial(_dispatch_kernel, n_prompt_tiles=n_p, n_blocks=n_blocks),
        grid_spec=pltpu.PrefetchScalarGridSpec(
            num_scalar_prefetch=5,
            grid=(n_p + n_s,),
            in_specs=[
                pl.BlockSpec((tm, d), lambda i, *_: (jnp.minimum(i, n_p - 1), 0)),
                pl.BlockSpec((tm, d), lambda i, *_: (jnp.maximum(i - n_p, 0), 0)),
            ],
            out_specs=pl.BlockSpec(memory_space=pl.ANY),
            scratch_shapes=[
                pltpu.SemaphoreType.DMA(()),
                pltpu.VMEM((MOE_BLOCK, d), U32),
                pltpu.SemaphoreType.DMA(()),
            ],
        ),
        out_shape=jax.ShapeDtypeStruct((n_blocks * MOE_BLOCK, d), U32),
        compiler_params=_cparams(("arbitrary",)),
        name="dispatch",
    )(d0, d1, block_start, block_count, n_active, h2_p, h2_s)


def _ffn_kernel(bs_ref, nb_ref, na_ref, xb_ref, wg_ref, wu_ref, wd_ref, yb_ref,
                xbuf, ybuf, wg_bf, wu_bf, wd_bf, xsem, ysem, *, n_blocks):
    e = pl.program_id(0)
    nb = nb_ref[e]
    b0 = bs_ref[e]
    na = na_ref[0]
    ring = xbuf.shape[0]

    def x_copy(blk):
        rows = pl.ds(pl.multiple_of(blk * MOE_BLOCK, MOE_BLOCK), MOE_BLOCK)
        slot = lax.rem(blk, ring)
        return pltpu.make_async_copy(xb_ref.at[rows], xbuf.at[slot], xsem.at[slot])

    def y_copy(blk):
        rows = pl.ds(pl.multiple_of(blk * MOE_BLOCK, MOE_BLOCK), MOE_BLOCK)
        slot = lax.rem(blk, ring)
        return pltpu.make_async_copy(ybuf.at[slot], yb_ref.at[rows], ysem.at[slot])

    @pl.when(e == 0)
    def _():
        for ahead in range(ring - 1):
            @pl.when(ahead < na)
            def _():
                x_copy(ahead).start(priority=1)

    @pl.when(nb > 0)
    def _():
        wg_bf[...] = wg_ref[0].astype(BF16)
        wu_bf[...] = wu_ref[0].astype(BF16)
        wd_bf[...] = wd_ref[0].astype(BF16)

    def block(i, carry):
        blk = b0 + i
        slot = lax.rem(blk, ring)
        x_copy(blk).wait()

        @pl.when(blk + ring - 1 < na)
        def _():
            x_copy(blk + ring - 1).start(priority=1)

        x_lo, x_hi = _unpack_bf16_pairs(xbuf[slot])
        half = x_lo.shape[1]
        g = _dot(x_lo, wg_bf[0:half, :]) + _dot(x_hi, wg_bf[half:, :])
        u = _dot(x_lo, wu_bf[0:half, :]) + _dot(x_hi, wu_bf[half:, :])
        a = (g * jax.nn.sigmoid(g) * u).astype(BF16)
        y = _dot(a, wd_bf[...])

        @pl.when(blk >= ring)
        def _():
            y_copy(blk - ring).wait()

        ybuf[slot] = _pack_bf16_pairs(y)
        y_copy(blk).start(priority=1)
        return carry

    lax.fori_loop(0, nb, block, 0)

    @pl.when(e == pl.num_programs(0) - 1)
    def _():
        for back in range(1, ring + 1):
            @pl.when(na - back >= 0)
            def _():
                y_copy(na - back).wait()

        ybuf[0] = jnp.zeros(ybuf.shape[1:], U32)

        def zero_block(b, carry):
            rows = pl.ds(pl.multiple_of(b * MOE_BLOCK, MOE_BLOCK), MOE_BLOCK)
            cp = pltpu.make_async_copy(ybuf.at[0], yb_ref.at[rows], ysem.at[0])
            cp.start()
            cp.wait()
            return carry

        lax.fori_loop(na_ref[0], n_blocks, zero_block, 0)


def _ffn(block_start, block_count, n_active, xb, w_gate, w_up, w_down):
    n_rows, d_packed = xb.shape
    n_exp, d, d_e = w_gate.shape
    n_blocks = n_rows // MOE_BLOCK
    return pl.pallas_call(
        partial(_ffn_kernel, n_blocks=n_blocks),
        grid_spec=pltpu.PrefetchScalarGridSpec(
            num_scalar_prefetch=3,
            grid=(n_exp,),
            in_specs=[
                pl.BlockSpec(memory_space=pl.ANY),
                pl.BlockSpec((1, d, d_e), lambda e, *_: (e, 0, 0)),
                pl.BlockSpec((1, d, d_e), lambda e, *_: (e, 0, 0)),
                pl.BlockSpec((1, d_e, d), lambda e, *_: (e, 0, 0)),
            ],
            out_specs=pl.BlockSpec(memory_space=pl.ANY),
            scratch_shapes=[
                pltpu.VMEM((FFN_RING, MOE_BLOCK, d_packed), U32),
                pltpu.VMEM((FFN_RING, MOE_BLOCK, d_packed), U32),
                pltpu.VMEM((d, d_e), BF16),
                pltpu.VMEM((d, d_e), BF16),
                pltpu.VMEM((d_e, d), BF16),
                pltpu.SemaphoreType.DMA((FFN_RING,)),
                pltpu.SemaphoreType.DMA((FFN_RING,)),
            ],
        ),
        out_shape=jax.ShapeDtypeStruct((n_rows, d_packed), U32),
        compiler_params=_cparams(("arbitrary",)),
        name="moe_ffn",
    )(block_start, block_count, n_active, xb, w_gate, w_up, w_down)


def _ple_kernel(d0_ref, d1_ref, x1_ref, wcol_ref, p_ref, yb_ref, gple_ref, wpg_ref, wp_ref, gfin_ref,
                y_ref, g_scr, sem, *, time_major, n_steps):
    i = pl.program_id(0)
    tm = x1_ref.shape[0]
    slot = lax.rem(i, 2) if n_steps > 1 else 0

    def copy(tile, r, k, to_slot):
        d_ref = d0_ref if k == 0 else d1_ref
        return pltpu.make_async_copy(yb_ref.at[pl.ds(d_ref[tile * tm + r], 1)],
                                     g_scr.at[to_slot, k, pl.ds(r, 1)], sem.at[to_slot])

    def issue(tile, to_slot):
        def body(r, carry):
            copy(tile, r, 0, to_slot).start(priority=0)
            copy(tile, r, 1, to_slot).start(priority=1)
            return carry
        lax.fori_loop(0, tm, body, 0, unroll=ROW_DMA_UNROLL)

    def drain(tile, from_slot):
        def body(r, carry):
            copy(tile, r, 0, from_slot).wait()
            copy(tile, r, 1, from_slot).wait()
            return carry
        lax.fori_loop(0, tm, body, 0, unroll=ROW_DMA_UNROLL)

    @pl.when(i == 0)
    def _():
        issue(0, 0)

    drain(i, slot)

    y0_lo, y0_hi = _unpack_pairs_f32(g_scr[slot, 0])
    y1_lo, y1_hi = _unpack_pairs_f32(g_scr[slot, 1])
    w0 = wcol_ref[:, 0:1]
    w1 = wcol_ref[:, 1:2]
    moe = jnp.concatenate([w0 * y0_lo + w1 * y1_lo, w0 * y0_hi + w1 * y1_hi], axis=1)
    x2 = x1_ref[...] + moe

    if n_steps > 1:
        nxt = jnp.minimum(i + 1, n_steps - 1)
        for r in range(tm):
            copy(nxt, r, 0, 1 - slot).start(priority=0)
            copy(nxt, r, 1, 1 - slot).start(priority=1)
    hn = _rms(x2, gple_ref[...]).astype(BF16)
    gate = jax.nn.sigmoid(_dot(hn, wpg_ref[...]))
    if time_major:
        pd = wp_ref.shape[0]
        pp = jnp.concatenate([p_ref[:, t * pd:(t + 1) * pd] for t in range(p_ref.shape[1] // pd)], axis=0)
    else:
        pp = p_ref[...]
    x3 = x2 + _dot(pp.astype(BF16), wp_ref[...]) * gate
    y = _rms(x3, gfin_ref[...])
    if time_major:
        d = y.shape[1]
        rows = y_ref.shape[0]
        for t in range(y_ref.shape[1] // d):
            y_ref[:, t * d:(t + 1) * d] = y[t * rows:(t + 1) * rows, :]
    else:
        y_ref[...] = y

    if n_steps > 1:
        @pl.when(i == n_steps - 1)
        def _():
            drain(n_steps - 1, 1 - slot)


def _ple(d0, d1, x1, wcol, p2d, yb, g_ple, w_pg, w_p, g_final, *, time_major):
    n_tok, d = x1.shape
    tm = n_tok if time_major else PLE_TILE
    n_steps = n_tok // tm
    pd = w_p.shape[0]
    if time_major:
        n_seq = p2d.shape[0]
        p_spec = pl.BlockSpec(p2d.shape, lambda i, *_: (0, 0))
        y_spec = pl.BlockSpec((n_seq, (n_tok // n_seq) * d), lambda i, *_: (0, 0))
        y_shape = jax.ShapeDtypeStruct((n_seq, (n_tok // n_seq) * d), F32)
    else:
        p_spec = pl.BlockSpec((tm, pd), lambda i, *_: (i, 0))
        y_spec = pl.BlockSpec((tm, d), lambda i, *_: (i, 0))
        y_shape = jax.ShapeDtypeStruct((n_tok, d), F32)
    return pl.pallas_call(
        partial(_ple_kernel, time_major=time_major, n_steps=n_steps),
        grid_spec=pltpu.PrefetchScalarGridSpec(
            num_scalar_prefetch=2,
            grid=(n_steps,),
            in_specs=[
                pl.BlockSpec((tm, d), lambda i, *_: (i, 0)),
                pl.BlockSpec((tm, LANES), lambda i, *_: (i, 0)),
                p_spec,
                pl.BlockSpec(memory_space=pl.ANY),
                pl.BlockSpec((1, d), lambda i, *_: (0, 0)),
                pl.BlockSpec(w_pg.shape, lambda i, *_: (0, 0)),
                pl.BlockSpec(w_p.shape, lambda i, *_: (0, 0)),
                pl.BlockSpec((1, d), lambda i, *_: (0, 0)),
            ],
            out_specs=y_spec,
            scratch_shapes=[
                pltpu.VMEM((min(n_steps, 2), 2, tm, yb.shape[1]), yb.dtype),
                pltpu.SemaphoreType.DMA((min(n_steps, 2),)),
            ],
        ),
        out_shape=y_shape,
        compiler_params=_cparams(("arbitrary",)),
        name="ple_tm" if time_major else "ple",
    )(d0, d1, x1, wcol, p2d, yb, g_ple, w_pg, w_p, g_final)


def _pad_lanes(v, offset=0):
    row = jnp.zeros((1, LANES), F32)
    return row.at[0, offset:offset + v.shape[0]].set(v.astype(F32))


def kernel(x_prompt, x_sample, state_conv, state_delta, p_prompt, p_sample, g_mix, w_in, w_s, b_s, g_v,
           conv_w, a_log, dt_bias, g_out, w_out, g_ffn, w_group, b_group, w_router, b_router, w_gate, w_up,
           w_down, g_ple, w_ple_gate, w_ple, g_final):
    n_batch, seq, d = x_prompt.shape
    n_seq, n_pos, _ = x_sample.shape
    depth = g_mix.shape[0]
    assert depth == 1
    a_width = A_GROUPS * A_GROUP_DIM
    hd = B_HEADS * B_DIM
    n_main = 2 * a_width + 4 * hd
    n_p = n_batch * seq
    n_s = n_seq * n_pos

    xp = x_prompt.reshape(n_p, d)
    xs = x_sample.reshape(n_seq, n_pos * d)
    l = 0

    w_main = jnp.swapaxes(w_in[l], 0, 1).astype(BF16)
    w_tail = jnp.pad(w_main[n_main:], ((0, LANES - 2 * B_HEADS), (0, 0)))
    gm = g_mix[l].reshape(1, d)
    causal = jnp.tril(jnp.ones((A_CHUNK, A_CHUNK), dtype=bool))
    ws_tril = jnp.where(causal, w_s[l], 0).astype(BF16)
    b_t = b_s[l].T
    gv = g_v[l].reshape(1, a_width)
    alog_row = _pad_lanes(a_log[l])
    dtb_row = _pad_lanes(dt_bias[l])
    gout = g_out[l].reshape(1, B_DIM)
    w_o = w_out[l].astype(BF16)
    gf = g_ffn[l].reshape(1, d)
    wr = jnp.pad(jnp.concatenate([w_group[l], w_router[l]], axis=1),
                 ((0, 0), (0, LANES - N_GROUPS - N_EXPERTS)))
    wr_hi = wr.astype(BF16)
    wr_t = jnp.concatenate([wr_hi, (wr - wr_hi.astype(F32)).astype(BF16)], axis=1)
    br_col = jnp.broadcast_to(
        jnp.pad(jnp.concatenate([b_group[l], b_router[l]]), (0, ROUTER_ROWS - N_GROUPS - N_EXPERTS))[:, None],
        (ROUTER_ROWS, LANES))
    gp = g_ple[l].reshape(1, d)
    w_pg = w_ple_gate[l].astype(BF16)
    w_p = w_ple[l].astype(BF16)
    gfin = g_final.reshape(1, d)

    proj_p, tail_p, oa_p = _inproj_mixa(xp, gm, w_main, w_tail, ws_tril, b_t, gv, n_main)
    proj_s, tail_s = _inproj(xs, gm, w_main, w_tail, n_main, time_major=True)
    oa_s, va_s = _mixa_sample(proj_s, w_s[l], b_s[l], gv, n_seq, n_pos)
    ob_p, sd_p, cb_p = _gdn_prompt(proj_p, tail_p, conv_w[l], alog_row, dtb_row, gout, n_batch, seq)
    cbuf = state_conv[l].reshape(n_seq, (CONV_W - 1) * 3 * hd)
    ob_s, sd_s, cb_s = _gdn_decode(proj_s, tail_s, cbuf, state_delta[l], conv_w[l], alog_row, dtb_row,
                                   gout, n_seq, n_pos)
    ob_s = ob_s.reshape(n_s, hd)

    x1_p, h2_p, lt_p = _outproj(xp, oa_p, ob_p, w_o, gf, wr_t, br_col, time_major=False)
    x1_s, h2_s, lt_s = _outproj(xs, oa_s, ob_s, w_o, gf, wr_t, br_col, time_major=True)
    dest, wcol, meta = _router(lt_p, lt_s)
    d0, d1 = dest[0], dest[1]
    n_active = meta[0, 0:1]
    block_start = meta[1, :N_EXPERTS]
    block_count = meta[2, :N_EXPERTS]

    n_tok = n_p + n_s
    n_blocks = (n_tok * 2) // MOE_BLOCK + N_EXPERTS
    xb = _dispatch(d0, d1, block_start, block_count, n_active, h2_p, h2_s, n_blocks)
    yb = _ffn(block_start, block_count, n_active, xb, w_gate[l], w_up[l], w_down[l])

    pp = p_prompt[l].reshape(n_p, -1)
    ps = p_sample[l].reshape(n_seq, -1)
    y_p = _ple(d0[:n_p], d1[:n_p], x1_p, wcol[:n_p], pp, yb, gp, w_pg, w_p, gfin, time_major=False)
    y_s = _ple(d0[n_p:], d1[n_p:], x1_s, wcol[n_p:], ps, yb, gp, w_pg, w_p, gfin, time_major=True)

    return (
        y_p.reshape(n_batch, seq, d),
        y_s.reshape(n_seq, n_pos, d),
        cb_p[None],
        sd_p[None],
        cb_s.reshape(n_seq, CONV_W - 1, 3 * hd)[None],
        sd_s[None],
        va_s.reshape(n_seq, n_pos, a_width)[None],
    )
```

```python
from functools import partial

import jax
import jax.numpy as jnp
from jax import lax
from jax.experimental import pallas as pl
from jax.experimental.pallas import tpu as pltpu

F32 = jnp.float32
BF16 = jnp.bfloat16
I32 = jnp.int32
U32 = jnp.uint32

EPS = 1e-6
LANES = 128
A_GROUPS = 8
A_GROUP_DIM = 128
A_CHUNK = 128
B_HEADS = 8
B_DIM = 128
CONV_W = 4
N_GROUPS = 4
EXPERTS_PER_GROUP = 8
N_EXPERTS = N_GROUPS * EXPERTS_PER_GROUP
ROUTER_ROWS = 40
TOKEN_TILE = 512
INPROJ_TILE = 1024
PLE_TILE = 256
GDN_CHUNK = 128
MOE_BLOCK = 128
FFN_RING = 4
DEC_SEQ_TILE = 16
ROW_DMA_UNROLL = 8
VMEM_LIMIT = 56 * 1024 * 1024


def _cparams(sem):
    return pltpu.CompilerParams(dimension_semantics=sem, vmem_limit_bytes=VMEM_LIMIT)


def _rms(x, g):
    return x * lax.rsqrt(jnp.mean(x * x, axis=-1, keepdims=True) + EPS) * g


def _softplus(x):
    return jnp.maximum(x, 0.0) + jnp.log(1.0 + jnp.exp(-jnp.abs(x)))


def _dot(a, b):
    return jnp.dot(a, b, preferred_element_type=F32)


def _dot_nt(a, b, precision=None):
    return lax.dot_general(a, b, (((1,), (1,)), ((), ())), precision=precision,
                           preferred_element_type=F32)


def _dot_tn(a, b):
    return lax.dot_general(a, b, (((0,), (0,)), ((), ())), preferred_element_type=F32)


def _pack_bf16_pairs(x):
    n = x.shape[1] // 2
    lo = lax.bitcast_convert_type(x[:, :n].astype(BF16).astype(F32), U32) >> 16
    hi = lax.bitcast_convert_type(x[:, n:].astype(BF16).astype(F32), U32) & jnp.uint32(0xFFFF0000)
    return hi | lo


def _unpack_pairs_f32(xu):
    lo = lax.bitcast_convert_type(xu << 16, F32)
    hi = lax.bitcast_convert_type(xu & jnp.uint32(0xFFFF0000), F32)
    return lo, hi


def _unpack_bf16_pairs(xu):
    lo, hi = _unpack_pairs_f32(xu)
    return lo.astype(BF16), hi.astype(BF16)


def _inproj_kernel(x_ref, g_ref, w_ref, wt_ref, proj_ref, tail_ref, h_scr, *, time_major):
    j = pl.program_id(1)

    @pl.when(j == 0)
    def _():
        g = g_ref[...]
        if time_major:
            d = g.shape[-1]
            rows = x_ref.shape[0]
            for t in range(x_ref.shape[1] // d):
                h_scr[t * rows:(t + 1) * rows, :] = _rms(x_ref[:, t * d:(t + 1) * d], g).astype(BF16)
        else:
            h_scr[...] = _rms(x_ref[...], g).astype(BF16)
        tail_ref[...] = _dot_nt(h_scr[...], wt_ref[...])

    proj_ref[...] = _dot_nt(h_scr[...], w_ref[...])


def _inproj(x2d, g_mix, w_main, w_tail, n_main, *, time_major, col_tile=1024):
    d = g_mix.shape[-1]
    if time_major:
        n_tok = x2d.shape[0] * (x2d.shape[1] // d)
        tm = n_tok
        x_spec = pl.BlockSpec(x2d.shape, lambda i, j: (0, 0))
    else:
        n_tok = x2d.shape[0]
        tm = min(INPROJ_TILE, n_tok)
        x_spec = pl.BlockSpec((tm, d), lambda i, j: (i, 0))
    grid = (n_tok // tm, n_main // col_tile)
    return pl.pallas_call(
        partial(_inproj_kernel, time_major=time_major),
        grid=grid,
        in_specs=[
            x_spec,
            pl.BlockSpec((1, d), lambda i, j: (0, 0)),
            pl.BlockSpec((col_tile, d), lambda i, j: (j, 0)),
            pl.BlockSpec((LANES, d), lambda i, j: (0, 0)),
        ],
        out_specs=[
            pl.BlockSpec((tm, col_tile), lambda i, j: (i, j)),
            pl.BlockSpec((tm, LANES), lambda i, j: (i, 0)),
        ],
        out_shape=[
            jax.ShapeDtypeStruct((n_tok, n_main), F32),
            jax.ShapeDtypeStruct((n_tok, LANES), F32),
        ],
        scratch_shapes=[pltpu.VMEM((tm, d), BF16)],
        compiler_params=_cparams(("parallel", "arbitrary")),
        name="inproj_tm" if time_major else "inproj",
    )(x2d, g_mix, w_main, w_tail)


def _group_ln(v, g):
    mu = jnp.mean(v, axis=-1, keepdims=True)
    dlt = v - mu
    var = jnp.mean(dlt * dlt, axis=-1, keepdims=True)
    return dlt * lax.rsqrt(var + EPS) * g


def _inproj_mixa_kernel(x_ref, g_ref, w_ref, wt_ref, ws_ref, bt_ref, gv_ref,
                        proj_ref, tail_ref, oa_ref, h_scr, u_scr, v_scr, *, n_later):
    j = pl.program_id(1)
    tm = h_scr.shape[0]

    @pl.when(j == 0)
    def _():
        h_scr[...] = _rms(x_ref[...], g_ref[...]).astype(BF16)
        tail_ref[...] = _dot_nt(h_scr[...], wt_ref[...])
        u_scr[...] = _dot_nt(h_scr[...], w_ref[...])

    @pl.when(j == 1)
    def _():
        v_scr[...] = _dot_nt(h_scr[...], w_ref[...])

    @pl.when(j >= 2)
    def _():
        proj_ref[...] = _dot_nt(h_scr[...], w_ref[...])
        share = tm // n_later
        r0 = (j - 2) * share
        for c in range(share // A_CHUNK):
            rows = pl.ds(pl.multiple_of(r0 + c * A_CHUNK, A_CHUNK), A_CHUNK)
            for h in range(A_GROUPS):
                cols = slice(h * A_GROUP_DIM, (h + 1) * A_GROUP_DIM)
                u = jax.nn.gelu(u_scr[rows, cols])
                v = _group_ln(jax.nn.gelu(v_scr[rows, cols]), gv_ref[:, cols])
                mixed = _dot(ws_ref[h], v.astype(BF16)) + bt_ref[:, h:h + 1]
                oa_ref[rows, cols] = (u * mixed).astype(BF16)


def _inproj_mixa(x2d, g_mix, w_main, w_tail, ws_tril, b_t, g_v, n_main):
    n_tok, d = x2d.shape
    a_width = A_GROUPS * A_GROUP_DIM
    tm = min(INPROJ_TILE, n_tok)
    n_col = n_main // a_width
    return pl.pallas_call(
        partial(_inproj_mixa_kernel, n_later=n_col - 2),
        grid=(n_tok // tm, n_col),
        in_specs=[
            pl.BlockSpec((tm, d), lambda i, j: (i, 0)),
            pl.BlockSpec((1, d), lambda i, j: (0, 0)),
            pl.BlockSpec((a_width, d), lambda i, j: (j, 0)),
            pl.BlockSpec((LANES, d), lambda i, j: (0, 0)),
            pl.BlockSpec((A_GROUPS, A_CHUNK, A_CHUNK), lambda i, j: (0, 0, 0)),
            pl.BlockSpec((A_CHUNK, A_GROUPS), lambda i, j: (0, 0)),
            pl.BlockSpec((1, a_width), lambda i, j: (0, 0)),
        ],
        out_specs=[
            pl.BlockSpec((tm, a_width), lambda i, j: (i, jnp.maximum(j - 2, 0))),
            pl.BlockSpec((tm, LANES), lambda i, j: (i, 0)),
            pl.BlockSpec((tm, a_width), lambda i, j: (i, 0)),
        ],
        out_shape=[
            jax.ShapeDtypeStruct((n_tok, n_main - 2 * a_width), F32),
            jax.ShapeDtypeStruct((n_tok, LANES), F32),
            jax.ShapeDtypeStruct((n_tok, a_width), BF16),
        ],
        scratch_shapes=[pltpu.VMEM((tm, d), BF16), pltpu.VMEM((tm, a_width), F32),
                        pltpu.VMEM((tm, a_width), F32)],
        compiler_params=_cparams(("parallel", "arbitrary")),
        name="inproj_mixa",
    )(x2d, g_mix, w_main, w_tail, ws_tril, b_t, g_v)


def _mixa_sample_kernel(ws_ref, bs_ref, au_ref, av_ref, gv_ref, out_ref, va_ref, *, n_seq, n_pos):
    a_width = A_GROUPS * A_GROUP_DIM
    vs = []
    for t in range(n_pos):
        rows = slice(t * n_seq, (t + 1) * n_seq)
        per_group = []
        for h in range(A_GROUPS):
            cols = slice(h * A_GROUP_DIM, (h + 1) * A_GROUP_DIM)
            per_group.append(_group_ln(jax.nn.gelu(av_ref[rows, cols]), gv_ref[:, cols]))
        vs.append(per_group)
        for h in range(A_GROUPS):
            va_ref[:, t * a_width + h * A_GROUP_DIM:t * a_width + (h + 1) * A_GROUP_DIM] = per_group[h]
    for t in range(n_pos):
        rows = slice(t * n_seq, (t + 1) * n_seq)
        for h in range(A_GROUPS):
            cols = slice(h * A_GROUP_DIM, (h + 1) * A_GROUP_DIM)
            mixed = jnp.zeros((n_seq, A_GROUP_DIM), F32) + bs_ref[h * n_pos + t]
            for s in range(t + 1):
                mixed = mixed + ws_ref[(h * n_pos + t) * n_pos + s] * vs[s][h]
            out_ref[rows, cols] = (jax.nn.gelu(au_ref[rows, cols]) * mixed).astype(BF16)


def _mixa_sample(proj, w_s, b_s, g_v, n_seq, n_pos):
    a_width = A_GROUPS * A_GROUP_DIM
    n_tok = n_seq * n_pos
    return pl.pallas_call(
        partial(_mixa_sample_kernel, n_seq=n_seq, n_pos=n_pos),
        grid_spec=pltpu.PrefetchScalarGridSpec(
            num_scalar_prefetch=2,
            grid=(1,),
            in_specs=[
                pl.BlockSpec((n_tok, a_width), lambda i, *_: (0, 0)),
                pl.BlockSpec((n_tok, a_width), lambda i, *_: (0, 1)),
                pl.BlockSpec((1, a_width), lambda i, *_: (0, 0)),
            ],
            out_specs=[
                pl.BlockSpec((n_tok, a_width), lambda i, *_: (0, 0)),
                pl.BlockSpec((n_seq, n_pos * a_width), lambda i, *_: (0, 0)),
            ],
        ),
        out_shape=[
            jax.ShapeDtypeStruct((n_tok, a_width), BF16),
            jax.ShapeDtypeStruct((n_seq, n_pos * a_width), F32),
        ],
        compiler_params=_cparams(("arbitrary",)),
        name="mixa_sample",
    )(w_s[:, :n_pos, :n_pos].reshape(-1), b_s[:, :n_pos].reshape(-1), proj, proj, g_v)


def _gdn_prompt_kernel(q_ref, k_ref, v_ref, z_ref, tail_ref, cw_ref, alog_ref, dtb_ref, gout_ref,
                       out_ref, s_out_ref, conv_out_ref, s_scr, xbuf):
    c = pl.program_id(1)
    n_c = pl.num_programs(1)
    C = q_ref.shape[0]
    hd = B_HEADS * B_DIM
    pad = 8
    lo = pad - (CONV_W - 1)

    @pl.when(c == 0)
    def _():
        s_scr[...] = jnp.zeros_like(s_scr)
        xbuf[0:pad, :] = jnp.zeros((pad, 3 * hd), F32)

    xbuf[pad:pad + C, 0:hd] = q_ref[...]
    xbuf[pad:pad + C, hd:2 * hd] = k_ref[...]
    xbuf[pad:pad + C, 2 * hd:3 * hd] = v_ref[...]
    xall = xbuf[...]
    y = xall * cw_ref[0:1, :]
    for j in range(1, CONV_W):
        y = pltpu.roll(y, 1, 0) + xall * cw_ref[j:j + 1, :]
    y = y[pad:pad + C, :]
    y = y * jax.nn.sigmoid(y)
    last = xbuf[pad + C - (CONV_W - 1):pad + C, :]
    xbuf[lo:pad, :] = last

    @pl.when(c == n_c - 1)
    def _():
        conv_out_ref[0] = last

    tail = tail_ref[...]
    g_all = -jnp.exp(alog_ref[...]) * _softplus(tail + dtb_ref[...])
    beta_all = jax.nn.sigmoid(tail)
    ri = lax.broadcasted_iota(I32, (C, C), 0)
    ci = lax.broadcasted_iota(I32, (C, C), 1)
    causal = ci <= ri
    strict = ci < ri
    tril = causal.astype(F32)
    G_all = jnp.dot(tril, g_all, precision=lax.Precision.HIGHEST, preferred_element_type=F32)
    G_rows = jnp.concatenate([G_all, jnp.zeros((LANES - C, LANES), F32)], axis=0).T if C < LANES else G_all.T
    scale = B_DIM ** -0.5
    n_levels = max(C.bit_length() - 2, 0)
    heads = range(B_HEADS)

    qs, ks, Gs, eGs, rhss, qks, Ps = [], [], [], [], [], [], []
    for h in heads:
        q = y[:, h * B_DIM:(h + 1) * B_DIM]
        k = y[:, hd + h * B_DIM:hd + (h + 1) * B_DIM]
        v = y[:, 2 * hd + h * B_DIM:2 * hd + (h + 1) * B_DIM]
        q = q * lax.rsqrt(jnp.sum(q * q, axis=-1, keepdims=True) + EPS) * scale
        k = k * lax.rsqrt(jnp.sum(k * k, axis=-1, keepdims=True) + EPS)
        G = G_all[:, h:h + 1]
        beta = beta_all[:, B_HEADS + h:B_HEADS + h + 1]
        eG = jnp.exp(G)
        decay = jnp.exp(jnp.where(causal, G - G_rows[h:h + 1, 0:C], -jnp.inf))
        qk_kk = _dot_nt(jnp.concatenate([q, k], axis=0), k)
        qks.append(qk_kk[:C] * decay)
        Ps.append(jnp.where(strict, -(beta * qk_kk[C:] * decay), 0.0))
        rhss.append(jnp.concatenate([v * beta, k * (beta * eG)], axis=1))
        qs.append(q)
        ks.append(k)
        Gs.append(G)
        eGs.append(eG)

    Ls = list(Ps)
    if n_levels > 0:
        Ps = [_dot(P, P) for P in Ps]
    for lvl in range(1, n_levels + 1):
        for h in heads:
            if lvl < n_levels:
                R = _dot(jnp.concatenate([Ps[h], Ls[h]], axis=0), Ps[h])
                Ls[h] = Ls[h] + Ps[h] + R[C:]
                Ps[h] = R[:C]
            else:
                Ls[h] = Ls[h] + Ps[h] + _dot(Ls[h], Ps[h])

    uws = [rhss[h] + _dot(Ls[h], rhss[h]) for h in heads]
    Ss = [s_scr[h] for h in heads]
    rs = [_dot(jnp.concatenate([uws[h][:, B_DIM:], qs[h] * eGs[h]], axis=0), Ss[h]) for h in heads]
    v_news = [uws[h][:, :B_DIM] - rs[h][:C] for h in heads]
    os_ = [rs[h][C:] + _dot(qks[h], v_news[h]) for h in heads]
    for h in heads:
        G_last = Gs[h][C - 1:C, :]
        kd = ks[h] * jnp.exp(G_last - Gs[h])
        s_scr[h] = Ss[h] * jnp.exp(G_last) + _dot_tn(kd, v_news[h])
    for h in heads:
        cols = slice(h * B_DIM, (h + 1) * B_DIM)
        o = _rms(os_[h], gout_ref[...])
        z = z_ref[:, cols]
        out_ref[:, cols] = (o * (z * jax.nn.sigmoid(z))).astype(BF16)

    @pl.when(c == n_c - 1)
    def _():
        s_out_ref[0] = s_scr[...]


def _gdn_prompt(proj, tail, conv_w, alog_row, dtb_row, g_out, n_batch, seq):
    C = GDN_CHUNK
    n_c = seq // C
    hd = B_HEADS * B_DIM
    row = lambda b, c: b * n_c + c
    return pl.pallas_call(
        _gdn_prompt_kernel,
        grid=(n_batch, n_c),
        in_specs=[
            pl.BlockSpec((C, hd), lambda b, c: (row(b, c), 0)),
            pl.BlockSpec((C, hd), lambda b, c: (row(b, c), 1)),
            pl.BlockSpec((C, hd), lambda b, c: (row(b, c), 2)),
            pl.BlockSpec((C, hd), lambda b, c: (row(b, c), 3)),
            pl.BlockSpec((C, LANES), lambda b, c: (row(b, c), 0)),
            pl.BlockSpec((CONV_W, 3 * hd), lambda b, c: (0, 0)),
            pl.BlockSpec((1, LANES), lambda b, c: (0, 0)),
            pl.BlockSpec((1, LANES), lambda b, c: (0, 0)),
            pl.BlockSpec((1, B_DIM), lambda b, c: (0, 0)),
        ],
        out_specs=[
            pl.BlockSpec((C, hd), lambda b, c: (row(b, c), 0)),
            pl.BlockSpec((1, B_HEADS, B_DIM, B_DIM), lambda b, c: (b, 0, 0, 0)),
            pl.BlockSpec((1, CONV_W - 1, 3 * hd), lambda b, c: (b, 0, 0)),
        ],
        out_shape=[
            jax.ShapeDtypeStruct((n_batch * seq, hd), BF16),
            jax.ShapeDtypeStruct((n_batch, B_HEADS, B_DIM, B_DIM), F32),
            jax.ShapeDtypeStruct((n_batch, CONV_W - 1, 3 * hd), F32),
        ],
        scratch_shapes=[
            pltpu.VMEM((B_HEADS, B_DIM, B_DIM), F32),
            pltpu.VMEM((8 + C, 3 * hd), F32),
        ],
        compiler_params=_cparams(("parallel", "arbitrary")),
        name="gdn_prompt",
    )(proj, proj, proj, proj, tail, conv_w, alog_row, dtb_row, g_out)


def _gdn_decode_kernel(*refs, n_pos):
    proj_refs = refs[:n_pos]
    tail_refs = refs[n_pos:2 * n_pos]
    (cbuf_ref, s_ref, cw_ref, alog_ref, dtb_ref, gout_ref,
     out_ref, s_out_ref, conv_out_ref, lhs_scr, res_scr, kd_scr, vn_scr, gl_scr) = refs[2 * n_pos:]
    SB = s_ref.shape[0]
    hd = B_HEADS * B_DIM
    qkv0 = 2 * hd
    z0 = qkv0 + 3 * hd
    scale = B_DIM ** -0.5

    xp = [cbuf_ref[:, j * 3 * hd:(j + 1) * 3 * hd] for j in range(CONV_W - 1)]
    xp += [proj_refs[t][:, qkv0:qkv0 + 3 * hd] for t in range(n_pos)]
    for j in range(CONV_W - 1):
        conv_out_ref[:, j * 3 * hd:(j + 1) * 3 * hd] = xp[n_pos + j]
    ys = []
    for t in range(n_pos):
        y = xp[t] * cw_ref[0:1, :]
        for j in range(1, CONV_W):
            y = y + xp[t + j] * cw_ref[j:j + 1, :]
        ys.append(y * jax.nn.sigmoid(y))
    gs, betas = [], []
    for t in range(n_pos):
        tail = tail_refs[t][...]
        gs.append(-jnp.exp(alog_ref[...]) * _softplus(tail + dtb_ref[...]))
        betas.append(jax.nn.sigmoid(tail))

    kd_scr[...] = jnp.zeros_like(kd_scr)
    vn_scr[...] = jnp.zeros_like(vn_scr)
    heads = range(B_HEADS)
    stride = 2 * n_pos

    qs, ks, Gs, us = [], [], [], []
    for h in heads:
        q_h, k_h, v_h, G_h, b_h = [], [], [], [], []
        G = None
        for t in range(n_pos):
            q = ys[t][:, h * B_DIM:(h + 1) * B_DIM]
            k = ys[t][:, hd + h * B_DIM:hd + (h + 1) * B_DIM]
            q_h.append(q * lax.rsqrt(jnp.sum(q * q, axis=-1, keepdims=True) + EPS) * scale)
            k_h.append(k * lax.rsqrt(jnp.sum(k * k, axis=-1, keepdims=True) + EPS))
            v_h.append(ys[t][:, 2 * hd + h * B_DIM:2 * hd + (h + 1) * B_DIM])
            g = gs[t][:, h:h + 1]
            G = g if G is None else G + g
            G_h.append(G)
            b_h.append(betas[t][:, B_HEADS + h:B_HEADS + h + 1])
        u_h, w_h = [], []
        for t in range(n_pos):
            u = v_h[t] * b_h[t]
            w = k_h[t] * (b_h[t] * jnp.exp(G_h[t]))
            for s in range(t):
                a = b_h[t] * jnp.sum(k_h[t] * k_h[s], axis=-1, keepdims=True) * jnp.exp(G_h[t] - G_h[s])
                u = u - a * u_h[s]
                w = w - a * w_h[s]
            u_h.append(u)
            w_h.append(w)
        for t in range(n_pos):
            lhs_scr[h, pl.ds(t, SB, stride=stride), :] = w_h[t]
            lhs_scr[h, pl.ds(n_pos + t, SB, stride=stride), :] = q_h[t] * jnp.exp(G_h[t])
        qs.append(q_h)
        ks.append(k_h)
        Gs.append(G_h)
        us.append(u_h)

    for h in heads:
        for s in range(SB):
            rows = slice(s * stride, (s + 1) * stride)
            res_scr[h, rows, :] = _dot(lhs_scr[h, rows, :], s_ref[s, h])

    for h in heads:
        cols = slice(h * B_DIM, (h + 1) * B_DIM)
        v_news = [us[h][t] - res_scr[h, pl.ds(t, SB, stride=stride), :] for t in range(n_pos)]
        G_last = Gs[h][n_pos - 1]
        for t in range(n_pos):
            o = res_scr[h, pl.ds(n_pos + t, SB, stride=stride), :]
            for s in range(t + 1):
                qk = jnp.sum(qs[h][t] * ks[h][s], axis=-1, keepdims=True) * jnp.exp(Gs[h][t] - Gs[h][s])
                o = o + qk * v_news[s]
            o = _rms(o, gout_ref[...])
            z = proj_refs[t][:, z0 + h * B_DIM:z0 + (h + 1) * B_DIM]
            out_ref[t, :, cols] = (o * (z * jax.nn.sigmoid(z))).astype(BF16)
            kd_scr[h, pl.ds(t, SB, stride=stride), :] = ks[h][t] * jnp.exp(G_last - Gs[h][t])
            vn_scr[h, pl.ds(t, SB, stride=stride), :] = v_news[t]
        gl_scr[h] = jnp.broadcast_to(jnp.exp(G_last), (SB, B_DIM))

    for h in heads:
        for s in range(SB):
            rows = slice(s * stride, (s + 1) * stride)
            s_out_ref[s, h] = (s_ref[s, h] * gl_scr[h, s:s + 1, 0:1]
                               + _dot_tn(kd_scr[h, rows, :], vn_scr[h, rows, :]))


def _gdn_decode(proj_s, tail_s, cbuf, s0, conv_w, alog_row, dtb_row, g_out, n_seq, n_pos):
    SB = DEC_SEQ_TILE
    hd = B_HEADS * B_DIM
    n_main = proj_s.shape[1]
    per_pos = n_seq // SB
    proj_specs = [pl.BlockSpec((SB, n_main), lambda i, t=t: (t * per_pos + i, 0)) for t in range(n_pos)]
    tail_specs = [pl.BlockSpec((SB, LANES), lambda i, t=t: (t * per_pos + i, 0)) for t in range(n_pos)]
    return pl.pallas_call(
        partial(_gdn_decode_kernel, n_pos=n_pos),
        grid=(per_pos,),
        in_specs=proj_specs + tail_specs + [
            pl.BlockSpec((SB, (CONV_W - 1) * 3 * hd), lambda i: (i, 0)),
            pl.BlockSpec((SB, B_HEADS, B_DIM, B_DIM), lambda i: (i, 0, 0, 0)),
            pl.BlockSpec((CONV_W, 3 * hd), lambda i: (0, 0)),
            pl.BlockSpec((1, LANES), lambda i: (0, 0)),
            pl.BlockSpec((1, LANES), lambda i: (0, 0)),
            pl.BlockSpec((1, B_DIM), lambda i: (0, 0)),
        ],
        out_specs=[
            pl.BlockSpec((n_pos, SB, hd), lambda i: (0, i, 0)),
            pl.BlockSpec((SB, B_HEADS, B_DIM, B_DIM), lambda i: (i, 0, 0, 0)),
            pl.BlockSpec((SB, (CONV_W - 1) * 3 * hd), lambda i: (i, 0)),
        ],
        out_shape=[
            jax.ShapeDtypeStruct((n_pos, n_seq, hd), BF16),
            jax.ShapeDtypeStruct((n_seq, B_HEADS, B_DIM, B_DIM), F32),
            jax.ShapeDtypeStruct((n_seq, (CONV_W - 1) * 3 * hd), F32),
        ],
        scratch_shapes=[
            pltpu.VMEM((B_HEADS, SB * 2 * n_pos, B_DIM), F32),
            pltpu.VMEM((B_HEADS, SB * 2 * n_pos, B_DIM), F32),
            pltpu.VMEM((B_HEADS, SB * 2 * n_pos, B_DIM), F32),
            pltpu.VMEM((B_HEADS, SB * 2 * n_pos, B_DIM), F32),
            pltpu.VMEM((B_HEADS, SB, B_DIM), F32),
        ],
        compiler_params=_cparams(("parallel",)),
        name="gdn_decode",
    )(*([proj_s] * n_pos), *([tail_s] * n_pos), cbuf, s0, conv_w, alog_row, dtb_row, g_out)


def _outproj_kernel(x_ref, oa_ref, ob_ref, w_ref, g_ref, wr_ref, br_ref,
                    x1_ref, h2_ref, lt_ref, *, time_major):
    a_width = oa_ref.shape[1]
    acc = _dot(oa_ref[...], w_ref[0:a_width, :]) + _dot(ob_ref[...], w_ref[a_width:, :])
    if time_major:
        d = g_ref.shape[-1]
        rows = x_ref.shape[0]
        x = jnp.concatenate([x_ref[:, t * d:(t + 1) * d] for t in range(x_ref.shape[1] // d)], axis=0)
    else:
        x = x_ref[...]
    x1 = x + acc
    x1_ref[...] = x1
    h2 = _rms(x1, g_ref[...])
    h2_ref[...] = _pack_bf16_pairs(h2)
    h_hi = h2.astype(BF16)
    h_lo = (h2 - h_hi.astype(F32)).astype(BF16)
    tm = h2.shape[0]
    lg = _dot(jnp.concatenate([h_hi, h_lo], axis=0), wr_ref[...])
    lg = lg[:tm, :LANES] + lg[:tm, LANES:] + lg[tm:, :LANES] + lg[tm:, LANES:]
    lt_ref[...] = lg.T[0:ROUTER_ROWS, :] + br_ref[:, 0:1]


def _outproj(x2d, out_a, out_b, w_out, g_ffn, wr_t, br_col, *, time_major):
    d = g_ffn.shape[-1]
    a_width = out_a.shape[1]
    n_tok = out_a.shape[0]
    tm = TOKEN_TILE
    if time_major:
        x_spec = pl.BlockSpec(x2d.shape, lambda i: (0, 0))
    else:
        x_spec = pl.BlockSpec((tm, d), lambda i: (i, 0))
    return pl.pallas_call(
        partial(_outproj_kernel, time_major=time_major),
        grid=(n_tok // tm,),
        in_specs=[
            x_spec,
            pl.BlockSpec((tm, a_width), lambda i: (i, 0)),
            pl.BlockSpec((tm, out_b.shape[1]), lambda i: (i, 0)),
            pl.BlockSpec(w_out.shape, lambda i: (0, 0)),
            pl.BlockSpec((1, d), lambda i: (0, 0)),
            pl.BlockSpec(wr_t.shape, lambda i: (0, 0)),
            pl.BlockSpec(br_col.shape, lambda i: (0, 0)),
        ],
        out_specs=[
            pl.BlockSpec((tm, d), lambda i: (i, 0)),
            pl.BlockSpec((tm, d // 2), lambda i: (i, 0)),
            pl.BlockSpec((ROUTER_ROWS, tm), lambda i: (0, i)),
        ],
        out_shape=[
            jax.ShapeDtypeStruct((n_tok, d), F32),
            jax.ShapeDtypeStruct((n_tok, d // 2), U32),
            jax.ShapeDtypeStruct((ROUTER_ROWS, n_tok), F32),
        ],
        compiler_params=_cparams(("parallel",)),
        name="outproj_tm" if time_major else "outproj",
    )(x2d, out_a, out_b, w_out, g_ffn, wr_t, br_col)


def _router_kernel(lp_ref, ls_ref, dest_ref, wcol_ref, meta_ref, cnt_scr, base_scr, *, n_prompt_tiles):
    p = pl.program_id(0)
    i = pl.program_id(1)
    tm = lp_ref.shape[1]
    lt = jnp.where(i < n_prompt_tiles, lp_ref[...], ls_ref[...])

    m = lt[0:1, :]
    sel = jnp.zeros((1, tm), I32)
    for r in range(1, N_GROUPS):
        upd = lt[r:r + 1, :] > m
        sel = jnp.where(upd, r, sel)
        m = jnp.where(upd, lt[r:r + 1, :], m)
    den = jnp.zeros((1, tm), F32)
    for r in range(N_GROUPS):
        den = den + jnp.exp(lt[r:r + 1, :] - m)
    g_w = 1.0 / den

    ev = []
    for j in range(EXPERTS_PER_GROUP):
        e = jnp.zeros((1, tm), F32)
        for g in range(N_GROUPS):
            row = N_GROUPS + g * EXPERTS_PER_GROUP + j
            e = jnp.where(sel == g, lt[row:row + 1, :], e)
        ev.append(e)
    v0 = ev[0]
    i0 = jnp.zeros((1, tm), I32)
    for j in range(1, EXPERTS_PER_GROUP):
        upd = ev[j] > v0
        i0 = jnp.where(upd, j, i0)
        v0 = jnp.where(upd, ev[j], v0)
    v1 = jnp.full((1, tm), -jnp.inf, F32)
    i1 = jnp.zeros((1, tm), I32)
    for j in range(EXPERTS_PER_GROUP):
        upd = (ev[j] > v1) & (i0 != j)
        i1 = jnp.where(upd, j, i1)
        v1 = jnp.where(upd, ev[j], v1)
    t = jnp.exp(v1 - v0)
    w0 = g_w / (1.0 + t)
    w1 = g_w * t / (1.0 + t)
    e0 = sel * EXPERTS_PER_GROUP + i0
    e1 = sel * EXPERTS_PER_GROUP + i1

    eio = lax.broadcasted_iota(I32, (N_EXPERTS, tm), 0)
    hit0 = eio == e0
    hit1 = eio == e1
    onehot = (hit0 | hit1).astype(F32)
    tile_cnt = jnp.sum(onehot, axis=1, keepdims=True)

    @pl.when((p == 0) & (i == 0))
    def _():
        cnt_scr[...] = jnp.zeros_like(cnt_scr)

    @pl.when(p == 0)
    def _():
        cnt_scr[...] = cnt_scr[...] + tile_cnt

    @pl.when((p == 1) & (i == 0))
    def _():
        cnt = cnt_scr[...]
        padded = jnp.floor((cnt + (MOE_BLOCK - 1)) * (1.0 / MOE_BLOCK)) * MOE_BLOCK
        ri = lax.broadcasted_iota(I32, (N_EXPERTS, N_EXPERTS), 0)
        ci = lax.broadcasted_iota(I32, (N_EXPERTS, N_EXPERTS), 1)
        lower = (ci < ri).astype(F32)
        pad_start = jnp.dot(lower, jnp.broadcast_to(padded, (N_EXPERTS, LANES)),
                            precision=lax.Precision.HIGHEST, preferred_element_type=F32)[:, 0:1]
        base_scr[...] = pad_start
        pad_end = pad_start + padded
        n_active = jnp.broadcast_to(pad_end[N_EXPERTS - 1:N_EXPERTS, :] * (1.0 / MOE_BLOCK), (1, LANES))
        diag = (lax.broadcasted_iota(I32, (N_EXPERTS, LANES), 0)
                == lax.broadcasted_iota(I32, (N_EXPERTS, LANES), 1))
        blk_start = jnp.sum(jnp.where(diag, pad_start * (1.0 / MOE_BLOCK), 0.0), axis=0, keepdims=True)
        blk_count = jnp.sum(jnp.where(diag, padded * (1.0 / MOE_BLOCK), 0.0), axis=0, keepdims=True)
        meta_ref[...] = jnp.concatenate(
            [n_active, blk_start, blk_count, jnp.zeros((5, LANES), F32)], axis=0).astype(I32)

    @pl.when(p == 1)
    def _():
        ui = lax.broadcasted_iota(I32, (tm, tm), 0)
        uj = lax.broadcasted_iota(I32, (tm, tm), 1)
        upper = (ui < uj).astype(BF16)
        excl = _dot(onehot.astype(BF16), upper)
        pos = base_scr[...] + excl
        d0 = jnp.sum(jnp.where(hit0, pos, 0.0), axis=0, keepdims=True)
        d1 = jnp.sum(jnp.where(hit1, pos, 0.0), axis=0, keepdims=True)
        dest_ref[...] = jnp.concatenate([d0, d1, jnp.zeros((6, tm), F32)], axis=0).astype(I32)
        wmat = jnp.concatenate([w0, w1, jnp.zeros((LANES - 2, tm), F32)], axis=0)
        wcol_ref[...] = wmat.T
        base_scr[...] = base_scr[...] + tile_cnt


def _router(lt_p, lt_s):
    tm = TOKEN_TILE
    n_p = lt_p.shape[1] // tm
    n_s = lt_s.shape[1] // tm
    n_tok = lt_p.shape[1] + lt_s.shape[1]
    return pl.pallas_call(
        partial(_router_kernel, n_prompt_tiles=n_p),
        grid=(2, n_p + n_s),
        in_specs=[
            pl.BlockSpec((ROUTER_ROWS, tm), lambda p, i: (0, jnp.minimum(i, n_p - 1))),
            pl.BlockSpec((ROUTER_ROWS, tm), lambda p, i: (0, jnp.maximum(i - n_p, 0))),
        ],
        out_specs=[
            pl.BlockSpec((8, tm), lambda p, i: (0, i * p)),
            pl.BlockSpec((tm, LANES), lambda p, i: (i * p, 0)),
            pl.BlockSpec((8, LANES), lambda p, i: (0, 0)),
        ],
        out_shape=[
            jax.ShapeDtypeStruct((8, n_tok), I32),
            jax.ShapeDtypeStruct((n_tok, LANES), F32),
            jax.ShapeDtypeStruct((8, LANES), I32),
        ],
        scratch_shapes=[pltpu.VMEM((N_EXPERTS, 1), F32), pltpu.VMEM((N_EXPERTS, 1), F32)],
        compiler_params=_cparams(("arbitrary", "arbitrary")),
        name="router",
    )(lt_p, lt_s)


def _dispatch_kernel(d0_ref, d1_ref, bs_ref, nb_ref, na_ref, hp_ref, hs_ref, xb_ref, sem, zbuf, zsem,
                     *, n_prompt_tiles, n_blocks):
    i = pl.program_id(0)
    tm = hp_ref.shape[0]

    @pl.when(i == 0)
    def _():
        zbuf[...] = jnp.zeros_like(zbuf)

        def zero_copy(blk):
            rows = pl.ds(pl.multiple_of(blk * MOE_BLOCK, MOE_BLOCK), MOE_BLOCK)
            return pltpu.make_async_copy(zbuf, xb_ref.at[rows], zsem)

        def last_block(e):
            return bs_ref[e] + jnp.maximum(nb_ref[e], 1) - 1

        def start_e(e, carry):
            @pl.when(nb_ref[e] > 0)
            def _():
                zero_copy(last_block(e)).start()
            return carry

        def wait_e(e, carry):
            @pl.when(nb_ref[e] > 0)
            def _():
                zero_copy(last_block(e)).wait()
            return carry

        def start_b(b, carry):
            zero_copy(b).start()
            return carry

        def wait_b(b, carry):
            zero_copy(b).wait()
            return carry

        lax.fori_loop(0, N_EXPERTS, start_e, 0)
        lax.fori_loop(na_ref[0], n_blocks, start_b, 0)
        lax.fori_loop(0, N_EXPERTS, wait_e, 0)
        lax.fori_loop(na_ref[0], n_blocks, wait_b, 0)

    def scatter(h_ref):
        def copy(r, d_ref):
            return pltpu.make_async_copy(h_ref.at[pl.ds(r, 1)], xb_ref.at[pl.ds(d_ref[i * tm + r], 1)], sem)

        def issue(r, carry):
            copy(r, d0_ref).start(priority=0)
            copy(r, d1_ref).start(priority=1)
            return carry

        def drain(r, carry):
            copy(r, d0_ref).wait()
            copy(r, d1_ref).wait()
            return carry

        lax.fori_loop(0, tm, issue, 0, unroll=ROW_DMA_UNROLL)
        lax.fori_loop(0, tm, drain, 0, unroll=ROW_DMA_UNROLL)

    @pl.when(i < n_prompt_tiles)
    def _():
        scatter(hp_ref)

    @pl.when(i >= n_prompt_tiles)
    def _():
        scatter(hs_ref)


def _dispatch(d0, d1, block_start, block_count, n_active, h2_p, h2_s, n_blocks):
    tm = TOKEN_TILE
    d = h2_p.shape[1]
    n_p = h2_p.shape[0] // tm
    n_s = h2_s.shape[0] // tm
    return pl.pallas_call(
        partial(_dispatch_kernel, n_prompt_tiles=n_p, n_blocks=n_blocks),
        grid_spec=pltpu.PrefetchScalarGridSpec(
            num_scalar_prefetch=5,
            grid=(n_p + n_s,),
            in_specs=[
                pl.BlockSpec((tm, d), lambda i, *_: (jnp.minimum(i, n_p - 1), 0)),
                pl.BlockSpec((tm, d), lambda i, *_: (jnp.maximum(i - n_p, 0), 0)),
            ],
            out_specs=pl.BlockSpec(memory_space=pl.ANY),
            scratch_shapes=[
                pltpu.SemaphoreType.DMA(()),
                pltpu.VMEM((MOE_BLOCK, d), U32),
                pltpu.SemaphoreType.DMA(()),
            ],
        ),
        out_shape=jax.ShapeDtypeStruct((n_blocks * MOE_BLOCK, d), U32),
        compiler_params=_cparams(("arbitrary",)),
        name="dispatch",
    )(d0, d1, block_start, block_count, n_active, h2_p, h2_s)


def _ffn_kernel(bs_ref, nb_ref, na_ref, xb_ref, wg_ref, wu_ref, wd_ref, yb_ref,
                xbuf, ybuf, wg_bf, wu_bf, wd_bf, xsem, ysem, *, n_blocks):
    e = pl.program_id(0)
    nb = nb_ref[e]
    b0 = bs_ref[e]
    na = na_ref[0]
    ring = xbuf.shape[0]

    def x_copy(blk):
        rows = pl.ds(pl.multiple_of(blk * MOE_BLOCK, MOE_BLOCK), MOE_BLOCK)
        slot = lax.rem(blk, ring)
        return pltpu.make_async_copy(xb_ref.at[rows], xbuf.at[slot], xsem.at[slot])

    def y_copy(blk):
        rows = pl.ds(pl.multiple_of(blk * MOE_BLOCK, MOE_BLOCK), MOE_BLOCK)
        slot = lax.rem(blk, ring)
        return pltpu.make_async_copy(ybuf.at[slot], yb_ref.at[rows], ysem.at[slot])

    @pl.when(e == 0)
    def _():
        for ahead in range(ring - 1):
            @pl.when(ahead < na)
            def _():
                x_copy(ahead).start(priority=1)

    @pl.when(nb > 0)
    def _():
        wg_bf[...] = wg_ref[0].astype(BF16)
        wu_bf[...] = wu_ref[0].astype(BF16)
        wd_bf[...] = wd_ref[0].astype(BF16)

    def block(i, carry):
        blk = b0 + i
        slot = lax.rem(blk, ring)
        x_copy(blk).wait()

        @pl.when(blk + ring - 1 < na)
        def _():
            x_copy(blk + ring - 1).start(priority=1)

        x_lo, x_hi = _unpack_bf16_pairs(xbuf[slot])
        half = x_lo.shape[1]
        g = _dot(x_lo, wg_bf[0:half, :]) + _dot(x_hi, wg_bf[half:, :])
        u = _dot(x_lo, wu_bf[0:half, :]) + _dot(x_hi, wu_bf[half:, :])
        a = (g * jax.nn.sigmoid(g) * u).astype(BF16)
        y = _dot(a, wd_bf[...])

        @pl.when(blk >= ring)
        def _():
            y_copy(blk - ring).wait()

        ybuf[slot] = _pack_bf16_pairs(y)
        y_copy(blk).start(priority=1)
        return carry

    lax.fori_loop(0, nb, block, 0)

    @pl.when(e == pl.num_programs(0) - 1)
    def _():
        for back in range(1, ring + 1):
            @pl.when(na - back >= 0)
            def _():
                y_copy(na - back).wait()

        ybuf[0] = jnp.zeros(ybuf.shape[1:], U32)

        def zero_block(b, carry):
            rows = pl.ds(pl.multiple_of(b * MOE_BLOCK, MOE_BLOCK), MOE_BLOCK)
            cp = pltpu.make_async_copy(ybuf.at[0], yb_ref.at[rows], ysem.at[0])
            cp.start()
            cp.wait()
            return carry

        lax.fori_loop(na_ref[0], n_blocks, zero_block, 0)


def _ffn(block_start, block_count, n_active, xb, w_gate, w_up, w_down):
    n_rows, d_packed = xb.shape
    n_exp, d, d_e = w_gate.shape
    n_blocks = n_rows // MOE_BLOCK
    return pl.pallas_call(
        partial(_ffn_kernel, n_blocks=n_blocks),
        grid_spec=pltpu.PrefetchScalarGridSpec(
            num_scalar_prefetch=3,
            grid=(n_exp,),
            in_specs=[
                pl.BlockSpec(memory_space=pl.ANY),
                pl.BlockSpec((1, d, d_e), lambda e, *_: (e, 0, 0)),
                pl.BlockSpec((1, d, d_e), lambda e, *_: (e, 0, 0)),
                pl.BlockSpec((1, d_e, d), lambda e, *_: (e, 0, 0)),
            ],
            out_specs=pl.BlockSpec(memory_space=pl.ANY),
            scratch_shapes=[
                pltpu.VMEM((FFN_RING, MOE_BLOCK, d_packed), U32),
                pltpu.VMEM((FFN_RING, MOE_BLOCK, d_packed), U32),
                pltpu.VMEM((d, d_e), BF16),
                pltpu.VMEM((d, d_e), BF16),
                pltpu.VMEM((d_e, d), BF16),
                pltpu.SemaphoreType.DMA((FFN_RING,)),
                pltpu.SemaphoreType.DMA((FFN_RING,)),
            ],
        ),
        out_shape=jax.ShapeDtypeStruct((n_rows, d_packed), U32),
        compiler_params=_cparams(("arbitrary",)),
        name="moe_ffn",
    )(block_start, block_count, n_active, xb, w_gate, w_up, w_down)


def _ple_kernel(d0_ref, d1_ref, x1_ref, wcol_ref, p_ref, yb_ref, gple_ref, wpg_ref, wp_ref, gfin_ref,
                y_ref, g_scr, sem, *, time_major, n_steps):
    i = pl.program_id(0)
    tm = x1_ref.shape[0]
    slot = lax.rem(i, 2) if n_steps > 1 else 0

    def copy(tile, r, k, to_slot):
        d_ref = d0_ref if k == 0 else d1_ref
        return pltpu.make_async_copy(yb_ref.at[pl.ds(d_ref[tile * tm + r], 1)],
                                     g_scr.at[to_slot, k, pl.ds(r, 1)], sem.at[to_slot])

    def issue(tile, to_slot):
        def body(r, carry):
            copy(tile, r, 0, to_slot).start(priority=0)
            copy(tile, r, 1, to_slot).start(priority=1)
            return carry
        lax.fori_loop(0, tm, body, 0, unroll=ROW_DMA_UNROLL)

    def drain(tile, from_slot):
        def body(r, carry):
            copy(tile, r, 0, from_slot).wait()
            copy(tile, r, 1, from_slot).wait()
            return carry
        lax.fori_loop(0, tm, body, 0, unroll=ROW_DMA_UNROLL)

    @pl.when(i == 0)
    def _():
        issue(0, 0)

    drain(i, slot)

    y0_lo, y0_hi = _unpack_pairs_f32(g_scr[slot, 0])
    y1_lo, y1_hi = _unpack_pairs_f32(g_scr[slot, 1])
    w0 = wcol_ref[:, 0:1]
    w1 = wcol_ref[:, 1:2]
    moe = jnp.concatenate([w0 * y0_lo + w1 * y1_lo, w0 * y0_hi + w1 * y1_hi], axis=1)
    x2 = x1_ref[...] + moe

    if n_steps > 1:
        nxt = jnp.minimum(i + 1, n_steps - 1)
        for r in range(tm):
            copy(nxt, r, 0, 1 - slot).start(priority=0)
            copy(nxt, r, 1, 1 - slot).start(priority=1)
    hn = _rms(x2, gple_ref[...]).astype(BF16)
    gate = jax.nn.sigmoid(_dot(hn, wpg_ref[...]))
    if time_major:
        pd = wp_ref.shape[0]
        pp = jnp.concatenate([p_ref[:, t * pd:(t + 1) * pd] for t in range(p_ref.shape[1] // pd)], axis=0)
    else:
        pp = p_ref[...]
    x3 = x2 + _dot(pp.astype(BF16), wp_ref[...]) * gate
    y = _rms(x3, gfin_ref[...])
    if time_major:
        d = y.shape[1]
        rows = y_ref.shape[0]
        for t in range(y_ref.shape[1] // d):
            y_ref[:, t * d:(t + 1) * d] = y[t * rows:(t + 1) * rows, :]
    else:
        y_ref[...] = y

    if n_steps > 1:
        @pl.when(i == n_steps - 1)
        def _():
            drain(n_steps - 1, 1 - slot)


def _ple(d0, d1, x1, wcol, p2d, yb, g_ple, w_pg, w_p, g_final, *, time_major):
    n_tok, d = x1.shape
    tm = n_tok if time_major else PLE_TILE
    n_steps = n_tok // tm
    pd = w_p.shape[0]
    if time_major:
        n_seq = p2d.shape[0]
        p_spec = pl.BlockSpec(p2d.shape, lambda i, *_: (0, 0))
        y_spec = pl.BlockSpec((n_seq, (n_tok // n_seq) * d), lambda i, *_: (0, 0))
        y_shape = jax.ShapeDtypeStruct((n_seq, (n_tok // n_seq) * d), F32)
    else:
        p_spec = pl.BlockSpec((tm, pd), lambda i, *_: (i, 0))
        y_spec = pl.BlockSpec((tm, d), lambda i, *_: (i, 0))
        y_shape = jax.ShapeDtypeStruct((n_tok, d), F32)
    return pl.pallas_call(
        partial(_ple_kernel, time_major=time_major, n_steps=n_steps),
        grid_spec=pltpu.PrefetchScalarGridSpec(
            num_scalar_prefetch=2,
            grid=(n_steps,),
            in_specs=[
                pl.BlockSpec((tm, d), lambda i, *_: (i, 0)),
                pl.BlockSpec((tm, LANES), lambda i, *_: (i, 0)),
                p_spec,
                pl.BlockSpec(memory_space=pl.ANY),
                pl.BlockSpec((1, d), lambda i, *_: (0, 0)),
                pl.BlockSpec(w_pg.shape, lambda i, *_: (0, 0)),
                pl.BlockSpec(w_p.shape, lambda i, *_: (0, 0)),
                pl.BlockSpec((1, d), lambda i, *_: (0, 0)),
            ],
            out_specs=y_spec,
            scratch_shapes=[
                pltpu.VMEM((min(n_steps, 2), 2, tm, yb.shape[1]), yb.dtype),
                pltpu.SemaphoreType.DMA((min(n_steps, 2),)),
            ],
        ),
        out_shape=y_shape,
        compiler_params=_cparams(("arbitrary",)),
        name="ple_tm" if time_major else "ple",
    )(d0, d1, x1, wcol, p2d, yb, g_ple, w_pg, w_p, g_final)


def _pad_lanes(v, offset=0):
    row = jnp.zeros((1, LANES), F32)
    return row.at[0, offset:offset + v.shape[0]].set(v.astype(F32))


def kernel(x_prompt, x_sample, state_conv, state_delta, p_prompt, p_sample, g_mix, w_in, w_s, b_s, g_v,
           conv_w, a_log, dt_bias, g_out, w_out, g_ffn, w_group, b_group, w_router, b_router, w_gate, w_up,
           w_down, g_ple, w_ple_gate, w_ple, g_final):
    n_batch, seq, d = x_prompt.shape
    n_seq, n_pos, _ = x_sample.shape
    depth = g_mix.shape[0]
    assert depth == 1
    a_width = A_GROUPS * A_GROUP_DIM
    hd = B_HEADS * B_DIM
    n_main = 2 * a_width + 4 * hd
    n_p = n_batch * seq
    n_s = n_seq * n_pos

    xp = x_prompt.reshape(n_p, d)
    xs = x_sample.reshape(n_seq, n_pos * d)
    l = 0

    w_main = jnp.swapaxes(w_in[l], 0, 1).astype(BF16)
    w_tail = jnp.pad(w_main[n_main:], ((0, LANES - 2 * B_HEADS), (0, 0)))
    gm = g_mix[l].reshape(1, d)
    causal = jnp.tril(jnp.ones((A_CHUNK, A_CHUNK), dtype=bool))
    ws_tril = jnp.where(causal, w_s[l], 0).astype(BF16)
    b_t = b_s[l].T
    gv = g_v[l].reshape(1, a_width)
    alog_row = _pad_lanes(a_log[l])
    dtb_row = _pad_lanes(dt_bias[l])
    gout = g_out[l].reshape(1, B_DIM)
    w_o = w_out[l].astype(BF16)
    gf = g_ffn[l].reshape(1, d)
    wr = jnp.pad(jnp.concatenate([w_group[l], w_router[l]], axis=1),
                 ((0, 0), (0, LANES - N_GROUPS - N_EXPERTS)))
    wr_hi = wr.astype(BF16)
    wr_t = jnp.concatenate([wr_hi, (wr - wr_hi.astype(F32)).astype(BF16)], axis=1)
    br_col = jnp.broadcast_to(
        jnp.pad(jnp.concatenate([b_group[l], b_router[l]]), (0, ROUTER_ROWS - N_GROUPS - N_EXPERTS))[:, None],
        (ROUTER_ROWS, LANES))
    gp = g_ple[l].reshape(1, d)
    w_pg = w_ple_gate[l].astype(BF16)
    w_p = w_ple[l].astype(BF16)
    gfin = g_final.reshape(1, d)

    proj_p, tail_p, oa_p = _inproj_mixa(xp, gm, w_main, w_tail, ws_tril, b_t, gv, n_main)
    proj_s, tail_s = _inproj(xs, gm, w_main, w_tail, n_main, time_major=True)
    oa_s, va_s = _mixa_sample(proj_s, w_s[l], b_s[l], gv, n_seq, n_pos)
    ob_p, sd_p, cb_p = _gdn_prompt(proj_p, tail_p, conv_w[l], alog_row, dtb_row, gout, n_batch, seq)
    cbuf = state_conv[l].reshape(n_seq, (CONV_W - 1) * 3 * hd)
    ob_s, sd_s, cb_s = _gdn_decode(proj_s, tail_s, cbuf, state_delta[l], conv_w[l], alog_row, dtb_row,
                                   gout, n_seq, n_pos)
    ob_s = ob_s.reshape(n_s, hd)

    x1_p, h2_p, lt_p = _outproj(xp, oa_p, ob_p, w_o, gf, wr_t, br_col, time_major=False)
    x1_s, h2_s, lt_s = _outproj(xs, oa_s, ob_s, w_o, gf, wr_t, br_col, time_major=True)
    dest, wcol, meta = _router(lt_p, lt_s)
    d0, d1 = dest[0], dest[1]
    n_active = meta[0, 0:1]
    block_start = meta[1, :N_EXPERTS]
    block_count = meta[2, :N_EXPERTS]

    n_tok = n_p + n_s
    n_blocks = (n_tok * 2) // MOE_BLOCK + N_EXPERTS
    xb = _dispatch(d0, d1, block_start, block_count, n_active, h2_p, h2_s, n_blocks)
    yb = _ffn(block_start, block_count, n_active, xb, w_gate[l], w_up[l], w_down[l])

    pp = p_prompt[l].reshape(n_p, -1)
    ps = p_sample[l].reshape(n_seq, -1)
    y_p = _ple(d0[:n_p], d1[:n_p], x1_p, wcol[:n_p], pp, yb, gp, w_pg, w_p, gfin, time_major=False)
    y_s = _ple(d0[n_p:], d1[n_p:], x1_s, wcol[n_p:], ps, yb, gp, w_pg, w_p, gfin, time_major=True)

    return (
        y_p.reshape(n_batch, seq, d),
        y_s.reshape(n_seq, n_pos, d),
        cb_p[None],
        sd_p[None],
        cb_s.reshape(n_seq, CONV_W - 1, 3 * hd)[None],
        sd_s[None],
        va_s.reshape(n_seq, n_pos, a_width)[None],
    )
```

```python
from functools import partial

import jax
import jax.numpy as jnp
from jax import lax
from jax.experimental import pallas as pl
from jax.experimental.pallas import tpu as pltpu

F32 = jnp.float32
BF16 = jnp.bfloat16
I32 = jnp.int32
U32 = jnp.uint32

EPS = 1e-6
LANES = 128
A_GROUPS = 8
A_GROUP_DIM = 128
A_CHUNK = 128
B_HEADS = 8
B_DIM = 128
CONV_W = 4
N_GROUPS = 4
EXPERTS_PER_GROUP = 8
N_EXPERTS = N_GROUPS * EXPERTS_PER_GROUP
ROUTER_ROWS = 40
TOKEN_TILE = 512
INPROJ_TILE = 1024
PLE_TILE = 256
GDN_CHUNK = 128
MOE_BLOCK = 128
FFN_RING = 4
DEC_SEQ_TILE = 16
ROW_DMA_UNROLL = 8
VMEM_LIMIT = 56 * 1024 * 1024


def _cparams(sem):
    return pltpu.CompilerParams(dimension_semantics=sem, vmem_limit_bytes=VMEM_LIMIT)


def _rms(x, g):
    return x * lax.rsqrt(jnp.mean(x * x, axis=-1, keepdims=True) + EPS) * g


def _softplus(x):
    return jnp.maximum(x, 0.0) + jnp.log(1.0 + jnp.exp(-jnp.abs(x)))


def _dot(a, b):
    return jnp.dot(a, b, preferred_element_type=F32)


def _dot_nt(a, b, precision=None):
    return lax.dot_general(a, b, (((1,), (1,)), ((), ())), precision=precision,
                           preferred_element_type=F32)


def _dot_tn(a, b):
    return lax.dot_general(a, b, (((0,), (0,)), ((), ())), preferred_element_type=F32)


def _pack_bf16_pairs(x):
    n = x.shape[1] // 2
    lo = lax.bitcast_convert_type(x[:, :n].astype(BF16).astype(F32), U32) >> 16
    hi = lax.bitcast_convert_type(x[:, n:].astype(BF16).astype(F32), U32) & jnp.uint32(0xFFFF0000)
    return hi | lo


def _unpack_pairs_f32(xu):
    lo = lax.bitcast_convert_type(xu << 16, F32)
    hi = lax.bitcast_convert_type(xu & jnp.uint32(0xFFFF0000), F32)
    return lo, hi


def _unpack_bf16_pairs(xu):
    lo, hi = _unpack_pairs_f32(xu)
    return lo.astype(BF16), hi.astype(BF16)


def _inproj_kernel(x_ref, g_ref, w_ref, wt_ref, proj_ref, tail_ref, h_scr, *, time_major):
    j = pl.program_id(1)

    @pl.when(j == 0)
    def _():
        g = g_ref[...]
        if time_major:
            d = g.shape[-1]
            rows = x_ref.shape[0]
            for t in range(x_ref.shape[1] // d):
                h_scr[t * rows:(t + 1) * rows, :] = _rms(x_ref[:, t * d:(t + 1) * d], g).astype(BF16)
        else:
            h_scr[...] = _rms(x_ref[...], g).astype(BF16)
        tail_ref[...] = _dot_nt(h_scr[...], wt_ref[...])

    proj_ref[...] = _dot_nt(h_scr[...], w_ref[...])


def _inproj(x2d, g_mix, w_main, w_tail, n_main, *, time_major, col_tile=1024):
    d = g_mix.shape[-1]
    if time_major:
        n_tok = x2d.shape[0] * (x2d.shape[1] // d)
        tm = n_tok
        x_spec = pl.BlockSpec(x2d.shape, lambda i, j: (0, 0))
    else:
        n_tok = x2d.shape[0]
        tm = min(INPROJ_TILE, n_tok)
        x_spec = pl.BlockSpec((tm, d), lambda i, j: (i, 0))
    grid = (n_tok // tm, n_main // col_tile)
    return pl.pallas_call(
        partial(_inproj_kernel, time_major=time_major),
        grid=grid,
        in_specs=[
            x_spec,
            pl.BlockSpec((1, d), lambda i, j: (0, 0)),
            pl.BlockSpec((col_tile, d), lambda i, j: (j, 0)),
            pl.BlockSpec((LANES, d), lambda i, j: (0, 0)),
        ],
        out_specs=[
            pl.BlockSpec((tm, col_tile), lambda i, j: (i, j)),
            pl.BlockSpec((tm, LANES), lambda i, j: (i, 0)),
        ],
        out_shape=[
            jax.ShapeDtypeStruct((n_tok, n_main), F32),
            jax.ShapeDtypeStruct((n_tok, LANES), F32),
        ],
        scratch_shapes=[pltpu.VMEM((tm, d), BF16)],
        compiler_params=_cparams(("parallel", "arbitrary")),
        name="inproj_tm" if time_major else "inproj",
    )(x2d, g_mix, w_main, w_tail)


def _group_ln(v, g):
    mu = jnp.mean(v, axis=-1, keepdims=True)
    dlt = v - mu
    var = jnp.mean(dlt * dlt, axis=-1, keepdims=True)
    return dlt * lax.rsqrt(var + EPS) * g


def _inproj_mixa_kernel(x_ref, g_ref, w_ref, wt_ref, ws_ref, bt_ref, gv_ref,
                        proj_ref, tail_ref, oa_ref, h_scr, u_scr, v_scr, *, n_later):
    j = pl.program_id(1)
    tm = h_scr.shape[0]

    @pl.when(j == 0)
    def _():
        h_scr[...] = _rms(x_ref[...], g_ref[...]).astype(BF16)
        tail_ref[...] = _dot_nt(h_scr[...], wt_ref[...])
        u_scr[...] = _dot_nt(h_scr[...], w_ref[...])

    @pl.when(j == 1)
    def _():
        v_scr[...] = _dot_nt(h_scr[...], w_ref[...])

    @pl.when(j >= 2)
    def _():
        proj_ref[...] = _dot_nt(h_scr[...], w_ref[...])
        share = tm // n_later
        r0 = (j - 2) * share
        for c in range(share // A_CHUNK):
            rows = pl.ds(pl.multiple_of(r0 + c * A_CHUNK, A_CHUNK), A_CHUNK)
            for h in range(A_GROUPS):
                cols = slice(h * A_GROUP_DIM, (h + 1) * A_GROUP_DIM)
                u = jax.nn.gelu(u_scr[rows, cols])
                v = _group_ln(jax.nn.gelu(v_scr[rows, cols]), gv_ref[:, cols])
                mixed = _dot(ws_ref[h], v.astype(BF16)) + bt_ref[:, h:h + 1]
                oa_ref[rows, cols] = (u * mixed).astype(BF16)


def _inproj_mixa(x2d, g_mix, w_main, w_tail, ws_tril, b_t, g_v, n_main):
    n_tok, d = x2d.shape
    a_width = A_GROUPS * A_GROUP_DIM
    tm = min(INPROJ_TILE, n_tok)
    n_col = n_main // a_width
    return pl.pallas_call(
        partial(_inproj_mixa_kernel, n_later=n_col - 2),
        grid=(n_tok // tm, n_col),
        in_specs=[
            pl.BlockSpec((tm, d), lambda i, j: (i, 0)),
            pl.BlockSpec((1, d), lambda i, j: (0, 0)),
            pl.BlockSpec((a_width, d), lambda i, j: (j, 0)),
            pl.BlockSpec((LANES, d), lambda i, j: (0, 0)),
            pl.BlockSpec((A_GROUPS, A_CHUNK, A_CHUNK), lambda i, j: (0, 0, 0)),
            pl.BlockSpec((A_CHUNK, A_GROUPS), lambda i, j: (0, 0)),
            pl.BlockSpec((1, a_width), lambda i, j: (0, 0)),
        ],
        out_specs=[
            pl.BlockSpec((tm, a_width), lambda i, j: (i, jnp.maximum(j - 2, 0))),
            pl.BlockSpec((tm, LANES), lambda i, j: (i, 0)),
            pl.BlockSpec((tm, a_width), lambda i, j: (i, 0)),
        ],
        out_shape=[
            jax.ShapeDtypeStruct((n_tok, n_main - 2 * a_width), F32),
            jax.ShapeDtypeStruct((n_tok, LANES), F32),
            jax.ShapeDtypeStruct((n_tok, a_width), BF16),
        ],
        scratch_shapes=[pltpu.VMEM((tm, d), BF16), pltpu.VMEM((tm, a_width), F32),
                        pltpu.VMEM((tm, a_width), F32)],
        compiler_params=_cparams(("parallel", "arbitrary")),
        name="inproj_mixa",
    )(x2d, g_mix, w_main, w_tail, ws_tril, b_t, g_v)


def _mixa_sample_kernel(ws_ref, bs_ref, au_ref, av_ref, gv_ref, out_ref, va_ref, *, n_seq, n_pos):
    a_width = A_GROUPS * A_GROUP_DIM
    vs = []
    for t in range(n_pos):
        rows = slice(t * n_seq, (t + 1) * n_seq)
        per_group = []
        for h in range(A_GROUPS):
            cols = slice(h * A_GROUP_DIM, (h + 1) * A_GROUP_DIM)
            per_group.append(_group_ln(jax.nn.gelu(av_ref[rows, cols]), gv_ref[:, cols]))
        vs.append(per_group)
        for h in range(A_GROUPS):
            va_ref[:, t * a_width + h * A_GROUP_DIM:t * a_width + (h + 1) * A_GROUP_DIM] = per_group[h]
    for t in range(n_pos):
        rows = slice(t * n_seq, (t + 1) * n_seq)
        for h in range(A_GROUPS):
            cols = slice(h * A_GROUP_DIM, (h + 1) * A_GROUP_DIM)
            mixed = jnp.zeros((n_seq, A_GROUP_DIM), F32) + bs_ref[h * n_pos + t]
            for s in range(t + 1):
                mixed = mixed + ws_ref[(h * n_pos + t) * n_pos + s] * vs[s][h]
            out_ref[rows, cols] = (jax.nn.gelu(au_ref[rows, cols]) * mixed).astype(BF16)


def _mixa_sample(proj, w_s, b_s, g_v, n_seq, n_pos):
    a_width = A_GROUPS * A_GROUP_DIM
    n_tok = n_seq * n_pos
    return pl.pallas_call(
        partial(_mixa_sample_kernel, n_seq=n_seq, n_pos=n_pos),
        grid_spec=pltpu.PrefetchScalarGridSpec(
            num_scalar_prefetch=2,
            grid=(1,),
            in_specs=[
                pl.BlockSpec((n_tok, a_width), lambda i, *_: (0, 0)),
                pl.BlockSpec((n_tok, a_width), lambda i, *_: (0, 1)),
                pl.BlockSpec((1, a_width), lambda i, *_: (0, 0)),
            ],
            out_specs=[
                pl.BlockSpec((n_tok, a_width), lambda i, *_: (0, 0)),
                pl.BlockSpec((n_seq, n_pos * a_width), lambda i, *_: (0, 0)),
            ],
        ),
        out_shape=[
            jax.ShapeDtypeStruct((n_tok, a_width), BF16),
            jax.ShapeDtypeStruct((n_seq, n_pos * a_width), F32),
        ],
        compiler_params=_cparams(("arbitrary",)),
        name="mixa_sample",
    )(w_s[:, :n_pos, :n_pos].reshape(-1), b_s[:, :n_pos].reshape(-1), proj, proj, g_v)


def _gdn_prompt_kernel(q_ref, k_ref, v_ref, z_ref, tail_ref, cw_ref, alog_ref, dtb_ref, gout_ref,
                       out_ref, s_out_ref, conv_out_ref, s_scr, xbuf):
    c = pl.program_id(1)
    n_c = pl.num_programs(1)
    C = q_ref.shape[0]
    hd = B_HEADS * B_DIM
    pad = 8
    lo = pad - (CONV_W - 1)

    @pl.when(c == 0)
    def _():
        s_scr[...] = jnp.zeros_like(s_scr)
        xbuf[0:pad, :] = jnp.zeros((pad, 3 * hd), F32)

    xbuf[pad:pad + C, 0:hd] = q_ref[...]
    xbuf[pad:pad + C, hd:2 * hd] = k_ref[...]
    xbuf[pad:pad + C, 2 * hd:3 * hd] = v_ref[...]
    xall = xbuf[...]
    y = xall * cw_ref[0:1, :]
    for j in range(1, CONV_W):
        y = pltpu.roll(y, 1, 0) + xall * cw_ref[j:j + 1, :]
    y = y[pad:pad + C, :]
    y = y * jax.nn.sigmoid(y)
    last = xbuf[pad + C - (CONV_W - 1):pad + C, :]
    xbuf[lo:pad, :] = last

    @pl.when(c == n_c - 1)
    def _():
        conv_out_ref[0] = last

    tail = tail_ref[...]
    g_all = -jnp.exp(alog_ref[...]) * _softplus(tail + dtb_ref[...])
    beta_all = jax.nn.sigmoid(tail)
    ri = lax.broadcasted_iota(I32, (C, C), 0)
    ci = lax.broadcasted_iota(I32, (C, C), 1)
    causal = ci <= ri
    strict = ci < ri
    tril = causal.astype(F32)
    G_all = jnp.dot(tril, g_all, precision=lax.Precision.HIGHEST, preferred_element_type=F32)
    G_rows = jnp.concatenate([G_all, jnp.zeros((LANES - C, LANES), F32)], axis=0).T if C < LANES else G_all.T
    scale = B_DIM ** -0.5
    n_levels = max(C.bit_length() - 2, 0)
    heads = range(B_HEADS)

    qs, ks, Gs, eGs, rhss, qks, Ps = [], [], [], [], [], [], []
    for h in heads:
        q = y[:, h * B_DIM:(h + 1) * B_DIM]
        k = y[:, hd + h * B_DIM:hd + (h + 1) * B_DIM]
        v = y[:, 2 * hd + h * B_DIM:2 * hd + (h + 1) * B_DIM]
        q = q * lax.rsqrt(jnp.sum(q * q, axis=-1, keepdims=True) + EPS) * scale
        k = k * lax.rsqrt(jnp.sum(k * k, axis=-1, keepdims=True) + EPS)
        G = G_all[:, h:h + 1]
        beta = beta_all[:, B_HEADS + h:B_HEADS + h + 1]
        eG = jnp.exp(G)
        decay = jnp.exp(jnp.where(causal, G - G_rows[h:h + 1, 0:C], -jnp.inf))
        qk_kk = _dot_nt(jnp.concatenate([q, k], axis=0), k)
        qks.append(qk_kk[:C] * decay)
        Ps.append(jnp.where(strict, -(beta * qk_kk[C:] * decay), 0.0))
        rhss.append(jnp.concatenate([v * beta, k * (beta * eG)], axis=1))
        qs.append(q)
        ks.append(k)
        Gs.append(G)
        eGs.append(eG)

    Ls = list(Ps)
    if n_levels > 0:
        Ps = [_dot(P, P) for P in Ps]
    for lvl in range(1, n_levels + 1):
        for h in heads:
            if lvl < n_levels:
                R = _dot(jnp.concatenate([Ps[h], Ls[h]], axis=0), Ps[h])
                Ls[h] = Ls[h] + Ps[h] + R[C:]
                Ps[h] = R[:C]
            else:
                Ls[h] = Ls[h] + Ps[h] + _dot(Ls[h], Ps[h])

    uws = [rhss[h] + _dot(Ls[h], rhss[h]) for h in heads]
    Ss = [s_scr[h] for h in heads]
    rs = [_dot(jnp.concatenate([uws[h][:, B_DIM:], qs[h] * eGs[h]], axis=0), Ss[h]) for h in heads]
    v_news = [uws[h][:, :B_DIM] - rs[h][:C] for h in heads]
    os_ = [rs[h][C:] + _dot(qks[h], v_news[h]) for h in heads]
    for h in heads:
        G_last = Gs[h][C - 1:C, :]
        kd = ks[h] * jnp.exp(G_last - Gs[h])
        s_scr[h] = Ss[h] * jnp.exp(G_last) + _dot_tn(kd, v_news[h])
    for h in heads:
        cols = slice(h * B_DIM, (h + 1) * B_DIM)
        o = _rms(os_[h], gout_ref[...])
        z = z_ref[:, cols]
        out_ref[:, cols] = (o * (z * jax.nn.sigmoid(z))).astype(BF16)

    @pl.when(c == n_c - 1)
    def _():
        s_out_ref[0] = s_scr[...]


def _gdn_prompt(proj, tail, conv_w, alog_row, dtb_row, g_out, n_batch, seq):
    C = GDN_CHUNK
    n_c = seq // C
    hd = B_HEADS * B_DIM
    row = lambda b, c: b * n_c + c
    return pl.pallas_call(
        _gdn_prompt_kernel,
        grid=(n_batch, n_c),
        in_specs=[
            pl.BlockSpec((C, hd), lambda b, c: (row(b, c), 0)),
            pl.BlockSpec((C, hd), lambda b, c: (row(b, c), 1)),
            pl.BlockSpec((C, hd), lambda b, c: (row(b, c), 2)),
            pl.BlockSpec((C, hd), lambda b, c: (row(b, c), 3)),
            pl.BlockSpec((C, LANES), lambda b, c: (row(b, c), 0)),
            pl.BlockSpec((CONV_W, 3 * hd), lambda b, c: (0, 0)),
            pl.BlockSpec((1, LANES), lambda b, c: (0, 0)),
            pl.BlockSpec((1, LANES), lambda b, c: (0, 0)),
            pl.BlockSpec((1, B_DIM), lambda b, c: (0, 0)),
        ],
        out_specs=[
            pl.BlockSpec((C, hd), lambda b, c: (row(b, c), 0)),
            pl.BlockSpec((1, B_HEADS, B_DIM, B_DIM), lambda b, c: (b, 0, 0, 0)),
            pl.BlockSpec((1, CONV_W - 1, 3 * hd), lambda b, c: (b, 0, 0)),
        ],
        out_shape=[
            jax.ShapeDtypeStruct((n_batch * seq, hd), BF16),
            jax.ShapeDtypeStruct((n_batch, B_HEADS, B_DIM, B_DIM), F32),
            jax.ShapeDtypeStruct((n_batch, CONV_W - 1, 3 * hd), F32),
        ],
        scratch_shapes=[
            pltpu.VMEM((B_HEADS, B_DIM, B_DIM), F32),
            pltpu.VMEM((8 + C, 3 * hd), F32),
        ],
        compiler_params=_cparams(("parallel", "arbitrary")),
        name="gdn_prompt",
    )(proj, proj, proj, proj, tail, conv_w, alog_row, dtb_row, g_out)


def _gdn_decode_kernel(*refs, n_pos):
    proj_refs = refs[:n_pos]
    tail_refs = refs[n_pos:2 * n_pos]
    (cbuf_ref, s_ref, cw_ref, alog_ref, dtb_ref, gout_ref,
     out_ref, s_out_ref, conv_out_ref, lhs_scr, res_scr, kd_scr, vn_scr, gl_scr) = refs[2 * n_pos:]
    SB = s_ref.shape[0]
    hd = B_HEADS * B_DIM
    qkv0 = 2 * hd
    z0 = qkv0 + 3 * hd
    scale = B_DIM ** -0.5

    xp = [cbuf_ref[:, j * 3 * hd:(j + 1) * 3 * hd] for j in range(CONV_W - 1)]
    xp += [proj_refs[t][:, qkv0:qkv0 + 3 * hd] for t in range(n_pos)]
    for j in range(CONV_W - 1):
        conv_out_ref[:, j * 3 * hd:(j + 1) * 3 * hd] = xp[n_pos + j]
    ys = []
    for t in range(n_pos):
        y = xp[t] * cw_ref[0:1, :]
        for j in range(1, CONV_W):
            y = y + xp[t + j] * cw_ref[j:j + 1, :]
        ys.append(y * jax.nn.sigmoid(y))
    gs, betas = [], []
    for t in range(n_pos):
        tail = tail_refs[t][...]
        gs.append(-jnp.exp(alog_ref[...]) * _softplus(tail + dtb_ref[...]))
        betas.append(jax.nn.sigmoid(tail))

    kd_scr[...] = jnp.zeros_like(kd_scr)
    vn_scr[...] = jnp.zeros_like(vn_scr)
    heads = range(B_HEADS)
    stride = 2 * n_pos

    qs, ks, Gs, us = [], [], [], []
    for h in heads:
        q_h, k_h, v_h, G_h, b_h = [], [], [], [], []
        G = None
        for t in range(n_pos):
            q = ys[t][:, h * B_DIM:(h + 1) * B_DIM]
            k = ys[t][:, hd + h * B_DIM:hd + (h + 1) * B_DIM]
            q_h.append(q * lax.rsqrt(jnp.sum(q * q, axis=-1, keepdims=True) + EPS) * scale)
            k_h.append(k * lax.rsqrt(jnp.sum(k * k, axis=-1, keepdims=True) + EPS))
            v_h.append(ys[t][:, 2 * hd + h * B_DIM:2 * hd + (h + 1) * B_DIM])
            g = gs[t][:, h:h + 1]
            G = g if G is None else G + g
            G_h.append(G)
            b_h.append(betas[t][:, B_HEADS + h:B_HEADS + h + 1])
        u_h, w_h = [], []
        for t in range(n_pos):
            u = v_h[t] * b_h[t]
            w = k_h[t] * (b_h[t] * jnp.exp(G_h[t]))
            for s in range(t):
                a = b_h[t] * jnp.sum(k_h[t] * k_h[s], axis=-1, keepdims=True) * jnp.exp(G_h[t] - G_h[s])
                u = u - a * u_h[s]
                w = w - a * w_h[s]
            u_h.append(u)
            w_h.append(w)
        for t in range(n_pos):
            lhs_scr[h, pl.ds(t, SB, stride=stride), :] = w_h[t]
            lhs_scr[h, pl.ds(n_pos + t, SB, stride=stride), :] = q_h[t] * jnp.exp(G_h[t])
        qs.append(q_h)
        ks.append(k_h)
        Gs.append(G_h)
        us.append(u_h)

    for h in heads:
        for s in range(SB):
            rows = slice(s * stride, (s + 1) * stride)
            res_scr[h, rows, :] = _dot(lhs_scr[h, rows, :], s_ref[s, h])

    for h in heads:
        cols = slice(h * B_DIM, (h + 1) * B_DIM)
        v_news = [us[h][t] - res_scr[h, pl.ds(t, SB, stride=stride), :] for t in range(n_pos)]
        G_last = Gs[h][n_pos - 1]
        for t in range(n_pos):
            o = res_scr[h, pl.ds(n_pos + t, SB, stride=stride), :]
            for s in range(t + 1):
                qk = jnp.sum(qs[h][t] * ks[h][s], axis=-1, keepdims=True) * jnp.exp(Gs[h][t] - Gs[h][s])
                o = o + qk * v_news[s]
            o = _rms(o, gout_ref[...])
            z = proj_refs[t][:, z0 + h * B_DIM:z0 + (h + 1) * B_DIM]
            out_ref[t, :, cols] = (o * (z * jax.nn.sigmoid(z))).astype(BF16)
            kd_scr[h, pl.ds(t, SB, stride=stride), :] = ks[h][t] * jnp.exp(G_last - Gs[h][t])
            vn_scr[h, pl.ds(t, SB, stride=stride), :] = v_news[t]
        gl_scr[h] = jnp.broadcast_to(jnp.exp(G_last), (SB, B_DIM))

    for h in heads:
        for s in range(SB):
            rows = slice(s * stride, (s + 1) * stride)
            s_out_ref[s, h] = (s_ref[s, h] * gl_scr[h, s:s + 1, 0:1]
                               + _dot_tn(kd_scr[h, rows, :], vn_scr[h, rows, :]))


def _gdn_decode(proj_s, tail_s, cbuf, s0, conv_w, alog_row, dtb_row, g_out, n_seq, n_pos):
    SB = DEC_SEQ_TILE
    hd = B_HEADS * B_DIM
    n_main = proj_s.shape[1]
    per_pos = n_seq // SB
    proj_specs = [pl.BlockSpec((SB, n_main), lambda i, t=t: (t * per_pos + i, 0)) for t in range(n_pos)]
    tail_specs = [pl.BlockSpec((SB, LANES), lambda i, t=t: (t * per_pos + i, 0)) for t in range(n_pos)]
    return pl.pallas_call(
        partial(_gdn_decode_kernel, n_pos=n_pos),
        grid=(per_pos,),
        in_specs=proj_specs + tail_specs + [
            pl.BlockSpec((SB, (CONV_W - 1) * 3 * hd), lambda i: (i, 0)),
            pl.BlockSpec((SB, B_HEADS, B_DIM, B_DIM), lambda i: (i, 0, 0, 0)),
            pl.BlockSpec((CONV_W, 3 * hd), lambda i: (0, 0)),
            pl.BlockSpec((1, LANES), lambda i: (0, 0)),
            pl.BlockSpec((1, LANES), lambda i: (0, 0)),
            pl.BlockSpec((1, B_DIM), lambda i: (0, 0)),
        ],
        out_specs=[
            pl.BlockSpec((n_pos, SB, hd), lambda i: (0, i, 0)),
            pl.BlockSpec((SB, B_HEADS, B_DIM, B_DIM), lambda i: (i, 0, 0, 0)),
            pl.BlockSpec((SB, (CONV_W - 1) * 3 * hd), lambda i: (i, 0)),
        ],
        out_shape=[
            jax.ShapeDtypeStruct((n_pos, n_seq, hd), BF16),
            jax.ShapeDtypeStruct((n_seq, B_HEADS, B_DIM, B_DIM), F32),
            jax.ShapeDtypeStruct((n_seq, (CONV_W - 1) * 3 * hd), F32),
        ],
        scratch_shapes=[
            pltpu.VMEM((B_HEADS, SB * 2 * n_pos, B_DIM), F32),
            pltpu.VMEM((B_HEADS, SB * 2 * n_pos, B_DIM), F32),
            pltpu.VMEM((B_HEADS, SB * 2 * n_pos, B_DIM), F32),
            pltpu.VMEM((B_HEADS, SB * 2 * n_pos, B_DIM), F32),
            pltpu.VMEM((B_HEADS, SB, B_DIM), F32),
        ],
        compiler_params=_cparams(("parallel",)),
        name="gdn_decode",
    )(*([proj_s] * n_pos), *([tail_s] * n_pos), cbuf, s0, conv_w, alog_row, dtb_row, g_out)


def _outproj_kernel(x_ref, oa_ref, ob_ref, w_ref, g_ref, wr_ref, br_ref,
                    x1_ref, h2_ref, lt_ref, w_bf, *, time_major):
    a_width = oa_ref.shape[1]

    @pl.when(pl.program_id(0) == 0)
    def _():
        w_bf[...] = w_ref[...].astype(BF16)

    acc = _dot(oa_ref[...], w_bf[0:a_width, :]) + _dot(ob_ref[...], w_bf[a_width:, :])
    if time_major:
        d = g_ref.shape[-1]
        rows = x_ref.shape[0]
        x = jnp.concatenate([x_ref[:, t * d:(t + 1) * d] for t in range(x_ref.shape[1] // d)], axis=0)
    else:
        x = x_ref[...]
    x1 = x + acc
    x1_ref[...] = x1
    h2 = _rms(x1, g_ref[...])
    h2_ref[...] = _pack_bf16_pairs(h2)
    h_hi = h2.astype(BF16)
    h_lo = (h2 - h_hi.astype(F32)).astype(BF16)
    tm = h2.shape[0]
    lg = _dot(jnp.concatenate([h_hi, h_lo], axis=0), wr_ref[...])
    lg = lg[:tm, :LANES] + lg[:tm, LANES:] + lg[tm:, :LANES] + lg[tm:, LANES:]
    lt_ref[...] = lg.T[0:ROUTER_ROWS, :] + br_ref[:, 0:1]


def _outproj(x2d, out_a, out_b, w_out, g_ffn, wr_t, br_col, *, time_major):
    d = g_ffn.shape[-1]
    a_width = out_a.shape[1]
    n_tok = out_a.shape[0]
    tm = TOKEN_TILE
    if time_major:
        x_spec = pl.BlockSpec(x2d.shape, lambda i: (0, 0))
    else:
        x_spec = pl.BlockSpec((tm, d), lambda i: (i, 0))
    return pl.pallas_call(
        partial(_outproj_kernel, time_major=time_major),
        grid=(n_tok // tm,),
        in_specs=[
            x_spec,
            pl.BlockSpec((tm, a_width), lambda i: (i, 0)),
            pl.BlockSpec((tm, out_b.shape[1]), lambda i: (i, 0)),
            pl.BlockSpec(w_out.shape, lambda i: (0, 0), pipeline_mode=pl.Buffered(1)),
            pl.BlockSpec((1, d), lambda i: (0, 0)),
            pl.BlockSpec(wr_t.shape, lambda i: (0, 0)),
            pl.BlockSpec(br_col.shape, lambda i: (0, 0)),
        ],
        out_specs=[
            pl.BlockSpec((tm, d), lambda i: (i, 0)),
            pl.BlockSpec((tm, d // 2), lambda i: (i, 0)),
            pl.BlockSpec((ROUTER_ROWS, tm), lambda i: (0, i)),
        ],
        out_shape=[
            jax.ShapeDtypeStruct((n_tok, d), F32),
            jax.ShapeDtypeStruct((n_tok, d // 2), U32),
            jax.ShapeDtypeStruct((ROUTER_ROWS, n_tok), F32),
        ],
        scratch_shapes=[pltpu.VMEM(w_out.shape, BF16)],
        compiler_params=_cparams(("arbitrary",)),
        name="outproj_tm" if time_major else "outproj",
    )(x2d, out_a, out_b, w_out, g_ffn, wr_t, br_col)


def _router_kernel(lp_ref, ls_ref, dest_ref, wcol_ref, meta_ref, cnt_scr, base_scr, *, n_prompt_tiles):
    p = pl.program_id(0)
    i = pl.program_id(1)
    tm = lp_ref.shape[1]
    lt = jnp.where(i < n_prompt_tiles, lp_ref[...], ls_ref[...])

    m = lt[0:1, :]
    sel = jnp.zeros((1, tm), I32)
    for r in range(1, N_GROUPS):
        upd = lt[r:r + 1, :] > m
        sel = jnp.where(upd, r, sel)
        m = jnp.where(upd, lt[r:r + 1, :], m)
    den = jnp.zeros((1, tm), F32)
    for r in range(N_GROUPS):
        den = den + jnp.exp(lt[r:r + 1, :] - m)
    g_w = 1.0 / den

    ev = []
    for j in range(EXPERTS_PER_GROUP):
        e = jnp.zeros((1, tm), F32)
        for g in range(N_GROUPS):
            row = N_GROUPS + g * EXPERTS_PER_GROUP + j
            e = jnp.where(sel == g, lt[row:row + 1, :], e)
        ev.append(e)
    v0 = ev[0]
    i0 = jnp.zeros((1, tm), I32)
    for j in range(1, EXPERTS_PER_GROUP):
        upd = ev[j] > v0
        i0 = jnp.where(upd, j, i0)
        v0 = jnp.where(upd, ev[j], v0)
    v1 = jnp.full((1, tm), -jnp.inf, F32)
    i1 = jnp.zeros((1, tm), I32)
    for j in range(EXPERTS_PER_GROUP):
        upd = (ev[j] > v1) & (i0 != j)
        i1 = jnp.where(upd, j, i1)
        v1 = jnp.where(upd, ev[j], v1)
    t = jnp.exp(v1 - v0)
    w0 = g_w / (1.0 + t)
    w1 = g_w * t / (1.0 + t)
    e0 = sel * EXPERTS_PER_GROUP + i0
    e1 = sel * EXPERTS_PER_GROUP + i1

    eio = lax.broadcasted_iota(I32, (N_EXPERTS, tm), 0)
    hit0 = eio == e0
    hit1 = eio == e1
    onehot = (hit0 | hit1).astype(F32)
    tile_cnt = jnp.sum(onehot, axis=1, keepdims=True)

    @pl.when((p == 0) & (i == 0))
    def _():
        cnt_scr[...] = jnp.zeros_like(cnt_scr)

    @pl.when(p == 0)
    def _():
        cnt_scr[...] = cnt_scr[...] + tile_cnt

    @pl.when((p == 1) & (i == 0))
    def _():
        cnt = cnt_scr[...]
        padded = jnp.floor((cnt + (MOE_BLOCK - 1)) * (1.0 / MOE_BLOCK)) * MOE_BLOCK
        ri = lax.broadcasted_iota(I32, (N_EXPERTS, N_EXPERTS), 0)
        ci = lax.broadcasted_iota(I32, (N_EXPERTS, N_EXPERTS), 1)
        lower = (ci < ri).astype(F32)
        pad_start = jnp.dot(lower, jnp.broadcast_to(padded, (N_EXPERTS, LANES)),
                            precision=lax.Precision.HIGHEST, preferred_element_type=F32)[:, 0:1]
        base_scr[...] = pad_start
        pad_end = pad_start + padded
        n_active = jnp.broadcast_to(pad_end[N_EXPERTS - 1:N_EXPERTS, :] * (1.0 / MOE_BLOCK), (1, LANES))
        diag = (lax.broadcasted_iota(I32, (N_EXPERTS, LANES), 0)
                == lax.broadcasted_iota(I32, (N_EXPERTS, LANES), 1))
        blk_start = jnp.sum(jnp.where(diag, pad_start * (1.0 / MOE_BLOCK), 0.0), axis=0, keepdims=True)
        blk_count = jnp.sum(jnp.where(diag, padded * (1.0 / MOE_BLOCK), 0.0), axis=0, keepdims=True)
        meta_ref[...] = jnp.concatenate(
            [n_active, blk_start, blk_count, jnp.zeros((5, LANES), F32)], axis=0).astype(I32)

    @pl.when(p == 1)
    def _():
        ui = lax.broadcasted_iota(I32, (tm, tm), 0)
        uj = lax.broadcasted_iota(I32, (tm, tm), 1)
        upper = (ui < uj).astype(BF16)
        excl = _dot(onehot.astype(BF16), upper)
        pos = base_scr[...] + excl
        d0 = jnp.sum(jnp.where(hit0, pos, 0.0), axis=0, keepdims=True)
        d1 = jnp.sum(jnp.where(hit1, pos, 0.0), axis=0, keepdims=True)
        dest_ref[...] = jnp.concatenate([d0, d1, jnp.zeros((6, tm), F32)], axis=0).astype(I32)
        wmat = jnp.concatenate([w0, w1, jnp.zeros((LANES - 2, tm), F32)], axis=0)
        wcol_ref[...] = wmat.T
        base_scr[...] = base_scr[...] + tile_cnt


def _router(lt_p, lt_s):
    tm = TOKEN_TILE
    n_p = lt_p.shape[1] // tm
    n_s = lt_s.shape[1] // tm
    n_tok = lt_p.shape[1] + lt_s.shape[1]
    return pl.pallas_call(
        partial(_router_kernel, n_prompt_tiles=n_p),
        grid=(2, n_p + n_s),
        in_specs=[
            pl.BlockSpec((ROUTER_ROWS, tm), lambda p, i: (0, jnp.minimum(i, n_p - 1))),
            pl.BlockSpec((ROUTER_ROWS, tm), lambda p, i: (0, jnp.maximum(i - n_p, 0))),
        ],
        out_specs=[
            pl.BlockSpec((8, tm), lambda p, i: (0, i * p)),
            pl.BlockSpec((tm, LANES), lambda p, i: (i * p, 0)),
            pl.BlockSpec((8, LANES), lambda p, i: (0, 0)),
        ],
        out_shape=[
            jax.ShapeDtypeStruct((8, n_tok), I32),
            jax.ShapeDtypeStruct((n_tok, LANES), F32),
            jax.ShapeDtypeStruct((8, LANES), I32),
        ],
        scratch_shapes=[pltpu.VMEM((N_EXPERTS, 1), F32), pltpu.VMEM((N_EXPERTS, 1), F32)],
        compiler_params=_cparams(("arbitrary", "arbitrary")),
        name="router",
    )(lt_p, lt_s)


def _dispatch_kernel(d0_ref, d1_ref, bs_ref, nb_ref, na_ref, hp_ref, hs_ref, xb_ref, sem, zbuf, zsem,
                     *, n_prompt_tiles, n_blocks):
    i = pl.program_id(0)
    tm = hp_ref.shape[0]

    @pl.when(i == 0)
    def _():
        zbuf[...] = jnp.zeros_like(zbuf)

        def zero_copy(blk):
            rows = pl.ds(pl.multiple_of(blk * MOE_BLOCK, MOE_BLOCK), MOE_BLOCK)
            return pltpu.make_async_copy(zbuf, xb_ref.at[rows], zsem)

        def last_block(e):
            return bs_ref[e] + jnp.maximum(nb_ref[e], 1) - 1

        def start_e(e, carry):
            @pl.when(nb_ref[e] > 0)
            def _():
                zero_copy(last_block(e)).start()
            return carry

        def wait_e(e, carry):
            @pl.when(nb_ref[e] > 0)
            def _():
                zero_copy(last_block(e)).wait()
            return carry

        def start_b(b, carry):
            zero_copy(b).start()
            return carry

        def wait_b(b, carry):
            zero_copy(b).wait()
            return carry

        lax.fori_loop(0, N_EXPERTS, start_e, 0)
        lax.fori_loop(na_ref[0], n_blocks, start_b, 0)
        lax.fori_loop(0, N_EXPERTS, wait_e, 0)
        lax.fori_loop(na_ref[0], n_blocks, wait_b, 0)

    def scatter(h_ref):
        def copy(r, d_ref):
            return pltpu.make_async_copy(h_ref.at[pl.ds(r, 1)], xb_ref.at[pl.ds(d_ref[i * tm + r], 1)], sem)

        def issue(r, carry):
            copy(r, d0_ref).start(priority=0)
            copy(r, d1_ref).start(priority=1)
            return carry

        def drain(r, carry):
            copy(r, d0_ref).wait()
            copy(r, d1_ref).wait()
            return carry

        lax.fori_loop(0, tm, issue, 0, unroll=ROW_DMA_UNROLL)
        lax.fori_loop(0, tm, drain, 0, unroll=ROW_DMA_UNROLL)

    @pl.when(i < n_prompt_tiles)
    def _():
        scatter(hp_ref)

    @pl.when(i >= n_prompt_tiles)
    def _():
        scatter(hs_ref)


def _dispatch(d0, d1, block_start, block_count, n_active, h2_p, h2_s, n_blocks):
    tm = TOKEN_TILE
    d = h2_p.shape[1]
    n_p = h2_p.shape[0] // tm
    n_s = h2_s.shape[0] // tm
    return pl.pallas_call(
        partial(_dispatch_kernel, n_prompt_tiles=n_p, n_blocks=n_blocks),
        grid_spec=pltpu.PrefetchScalarGridSpec(
            num_scalar_prefetch=5,
            grid=(n_p + n_s,),
            in_specs=[
                pl.BlockSpec((tm, d), lambda i, *_: (jnp.minimum(i, n_p - 1), 0)),
                pl.BlockSpec((tm, d), lambda i, *_: (jnp.maximum(i - n_p, 0), 0)),
            ],
            out_specs=pl.BlockSpec(memory_space=pl.ANY),
            scratch_shapes=[
                pltpu.SemaphoreType.DMA(()),
                pltpu.VMEM((MOE_BLOCK, d), U32),
                pltpu.SemaphoreType.DMA(()),
            ],
        ),
        out_shape=jax.ShapeDtypeStruct((n_blocks * MOE_BLOCK, d), U32),
        compiler_params=_cparams(("arbitrary",)),
        name="dispatch",
    )(d0, d1, block_start, block_count, n_active, h2_p, h2_s)


def _ffn_kernel(bs_ref, nb_ref, na_ref, xb_ref, wg_ref, wu_ref, wd_ref, yb_ref,
                xbuf, ybuf, wg_bf, wu_bf, wd_bf, xsem, ysem, *, n_blocks):
    e = pl.program_id(0)
    nb = nb_ref[e]
    b0 = bs_ref[e]
    na = na_ref[0]
    ring = xbuf.shape[0]

    def x_copy(blk):
        rows = pl.ds(pl.multiple_of(blk * MOE_BLOCK, MOE_BLOCK), MOE_BLOCK)
        slot = lax.rem(blk, ring)
        return pltpu.make_async_copy(xb_ref.at[rows], xbuf.at[slot], xsem.at[slot])

    def y_copy(blk):
        rows = pl.ds(pl.multiple_of(blk * MOE_BLOCK, MOE_BLOCK), MOE_BLOCK)
        slot = lax.rem(blk, ring)
        return pltpu.make_async_copy(ybuf.at[slot], yb_ref.at[rows], ysem.at[slot])

    @pl.when(e == 0)
    def _():
        for ahead in range(ring - 1):
            @pl.when(ahead < na)
            def _():
                x_copy(ahead).start(priority=1)

    @pl.when(nb > 0)
    def _():
        wg_bf[...] = wg_ref[0].astype(BF16)
        wu_bf[...] = wu_ref[0].astype(BF16)
        wd_bf[...] = wd_ref[0].astype(BF16)

    def block(i, carry):
        blk = b0 + i
        slot = lax.rem(blk, ring)
        x_copy(blk).wait()

        @pl.when(blk + ring - 1 < na)
        def _():
            x_copy(blk + ring - 1).start(priority=1)

        x_lo, x_hi = _unpack_bf16_pairs(xbuf[slot])
        half = x_lo.shape[1]
        g = _dot(x_lo, wg_bf[0:half, :]) + _dot(x_hi, wg_bf[half:, :])
        u = _dot(x_lo, wu_bf[0:half, :]) + _dot(x_hi, wu_bf[half:, :])
        a = (g * jax.nn.sigmoid(g) * u).astype(BF16)
        y = _dot(a, wd_bf[...])

        @pl.when(blk >= ring)
        def _():
            y_copy(blk - ring).wait()

        ybuf[slot] = _pack_bf16_pairs(y)
        y_copy(blk).start(priority=1)
        return carry

    lax.fori_loop(0, nb, block, 0)

    @pl.when(e == pl.num_programs(0) - 1)
    def _():
        for back in range(1, ring + 1):
            @pl.when(na - back >= 0)
            def _():
                y_copy(na - back).wait()

        ybuf[0] = jnp.zeros(ybuf.shape[1:], U32)

        def zero_block(b, carry):
            rows = pl.ds(pl.multiple_of(b * MOE_BLOCK, MOE_BLOCK), MOE_BLOCK)
            cp = pltpu.make_async_copy(ybuf.at[0], yb_ref.at[rows], ysem.at[0])
            cp.start()
            cp.wait()
            return carry

        lax.fori_loop(na_ref[0], n_blocks, zero_block, 0)


def _ffn(block_start, block_count, n_active, xb, w_gate, w_up, w_down):
    n_rows, d_packed = xb.shape
    n_exp, d, d_e = w_gate.shape
    n_blocks = n_rows // MOE_BLOCK
    return pl.pallas_call(
        partial(_ffn_kernel, n_blocks=n_blocks),
        grid_spec=pltpu.PrefetchScalarGridSpec(
            num_scalar_prefetch=3,
            grid=(n_exp,),
            in_specs=[
                pl.BlockSpec(memory_space=pl.ANY),
                pl.BlockSpec((1, d, d_e), lambda e, *_: (e, 0, 0)),
                pl.BlockSpec((1, d, d_e), lambda e, *_: (e, 0, 0)),
                pl.BlockSpec((1, d_e, d), lambda e, *_: (e, 0, 0)),
            ],
            out_specs=pl.BlockSpec(memory_space=pl.ANY),
            scratch_shapes=[
                pltpu.VMEM((FFN_RING, MOE_BLOCK, d_packed), U32),
                pltpu.VMEM((FFN_RING, MOE_BLOCK, d_packed), U32),
                pltpu.VMEM((d, d_e), BF16),
                pltpu.VMEM((d, d_e), BF16),
                pltpu.VMEM((d_e, d), BF16),
                pltpu.SemaphoreType.DMA((FFN_RING,)),
                pltpu.SemaphoreType.DMA((FFN_RING,)),
            ],
        ),
        out_shape=jax.ShapeDtypeStruct((n_rows, d_packed), U32),
        compiler_params=_cparams(("arbitrary",)),
        name="moe_ffn",
    )(block_start, block_count, n_active, xb, w_gate, w_up, w_down)


def _ple_kernel(d0_ref, d1_ref, x1_ref, wcol_ref, p_ref, yb_ref, gple_ref, wpg_ref, wp_ref, gfin_ref,
                y_ref, g_scr, wpg_bf, sem, *, time_major, n_steps):
    i = pl.program_id(0)
    tm = x1_ref.shape[0]
    slot = lax.rem(i, 2) if n_steps > 1 else 0

    def copy(tile, r, k, to_slot):
        d_ref = d0_ref if k == 0 else d1_ref
        return pltpu.make_async_copy(yb_ref.at[pl.ds(d_ref[tile * tm + r], 1)],
                                     g_scr.at[to_slot, k, pl.ds(r, 1)], sem.at[to_slot])

    def issue(tile, to_slot):
        def body(r, carry):
            copy(tile, r, 0, to_slot).start(priority=0)
            copy(tile, r, 1, to_slot).start(priority=1)
            return carry
        lax.fori_loop(0, tm, body, 0, unroll=ROW_DMA_UNROLL)

    def drain(tile, from_slot):
        def body(r, carry):
            copy(tile, r, 0, from_slot).wait()
            copy(tile, r, 1, from_slot).wait()
            return carry
        lax.fori_loop(0, tm, body, 0, unroll=ROW_DMA_UNROLL)

    @pl.when(i == 0)
    def _():
        issue(0, 0)
        wpg_bf[...] = wpg_ref[...].astype(BF16)

    drain(i, slot)

    y0_lo, y0_hi = _unpack_pairs_f32(g_scr[slot, 0])
    y1_lo, y1_hi = _unpack_pairs_f32(g_scr[slot, 1])
    w0 = wcol_ref[:, 0:1]
    w1 = wcol_ref[:, 1:2]
    moe = jnp.concatenate([w0 * y0_lo + w1 * y1_lo, w0 * y0_hi + w1 * y1_hi], axis=1)
    x2 = x1_ref[...] + moe

    if n_steps > 1:
        nxt = jnp.minimum(i + 1, n_steps - 1)
        for r in range(tm):
            copy(nxt, r, 0, 1 - slot).start(priority=0)
            copy(nxt, r, 1, 1 - slot).start(priority=1)
    hn = _rms(x2, gple_ref[...]).astype(BF16)
    gate = jax.nn.sigmoid(_dot(hn, wpg_bf[...]))
    if time_major:
        pd = wp_ref.shape[0]
        pp = jnp.concatenate([p_ref[:, t * pd:(t + 1) * pd] for t in range(p_ref.shape[1] // pd)], axis=0)
    else:
        pp = p_ref[...]
    x3 = x2 + _dot(pp.astype(BF16), wp_ref[...]) * gate
    y = _rms(x3, gfin_ref[...])
    if time_major:
        d = y.shape[1]
        rows = y_ref.shape[0]
        for t in range(y_ref.shape[1] // d):
            y_ref[:, t * d:(t + 1) * d] = y[t * rows:(t + 1) * rows, :]
    else:
        y_ref[...] = y

    if n_steps > 1:
        @pl.when(i == n_steps - 1)
        def _():
            drain(n_steps - 1, 1 - slot)


def _ple(d0, d1, x1, wcol, p2d, yb, g_ple, w_pg, w_p, g_final, *, time_major):
    n_tok, d = x1.shape
    tm = n_tok if time_major else PLE_TILE
    n_steps = n_tok // tm
    pd = w_p.shape[0]
    if time_major:
        n_seq = p2d.shape[0]
        p_spec = pl.BlockSpec(p2d.shape, lambda i, *_: (0, 0))
        y_spec = pl.BlockSpec((n_seq, (n_tok // n_seq) * d), lambda i, *_: (0, 0))
        y_shape = jax.ShapeDtypeStruct((n_seq, (n_tok // n_seq) * d), F32)
    else:
        p_spec = pl.BlockSpec((tm, pd), lambda i, *_: (i, 0))
        y_spec = pl.BlockSpec((tm, d), lambda i, *_: (i, 0))
        y_shape = jax.ShapeDtypeStruct((n_tok, d), F32)
    return pl.pallas_call(
        partial(_ple_kernel, time_major=time_major, n_steps=n_steps),
        grid_spec=pltpu.PrefetchScalarGridSpec(
            num_scalar_prefetch=2,
            grid=(n_steps,),
            in_specs=[
                pl.BlockSpec((tm, d), lambda i, *_: (i, 0)),
                pl.BlockSpec((tm, LANES), lambda i, *_: (i, 0)),
                p_spec,
                pl.BlockSpec(memory_space=pl.ANY),
                pl.BlockSpec((1, d), lambda i, *_: (0, 0)),
                pl.BlockSpec(w_pg.shape, lambda i, *_: (0, 0), pipeline_mode=pl.Buffered(1)),
                pl.BlockSpec(w_p.shape, lambda i, *_: (0, 0)),
                pl.BlockSpec((1, d), lambda i, *_: (0, 0)),
            ],
            out_specs=y_spec,
            scratch_shapes=[
                pltpu.VMEM((min(n_steps, 2), 2, tm, yb.shape[1]), yb.dtype),
                pltpu.VMEM(w_pg.shape, BF16),
                pltpu.SemaphoreType.DMA((min(n_steps, 2),)),
            ],
        ),
        out_shape=y_shape,
        compiler_params=_cparams(("arbitrary",)),
        name="ple_tm" if time_major else "ple",
    )(d0, d1, x1, wcol, p2d, yb, g_ple, w_pg, w_p, g_final)


def _pad_lanes(v, offset=0):
    row = jnp.zeros((1, LANES), F32)
    return row.at[0, offset:offset + v.shape[0]].set(v.astype(F32))


def kernel(x_prompt, x_sample, state_conv, state_delta, p_prompt, p_sample, g_mix, w_in, w_s, b_s, g_v,
           conv_w, a_log, dt_bias, g_out, w_out, g_ffn, w_group, b_group, w_router, b_router, w_gate, w_up,
           w_down, g_ple, w_ple_gate, w_ple, g_final):
    n_batch, seq, d = x_prompt.shape
    n_seq, n_pos, _ = x_sample.shape
    depth = g_mix.shape[0]
    assert depth == 1
    a_width = A_GROUPS * A_GROUP_DIM
    hd = B_HEADS * B_DIM
    n_main = 2 * a_width + 4 * hd
    n_p = n_batch * seq
    n_s = n_seq * n_pos

    xp = x_prompt.reshape(n_p, d)
    xs = x_sample.reshape(n_seq, n_pos * d)
    l = 0

    w_main = jnp.swapaxes(w_in[l], 0, 1).astype(BF16)
    w_tail = jnp.pad(w_main[n_main:], ((0, LANES - 2 * B_HEADS), (0, 0)))
    gm = g_mix[l].reshape(1, d)
    causal = jnp.tril(jnp.ones((A_CHUNK, A_CHUNK), dtype=bool))
    ws_tril = jnp.where(causal, w_s[l], 0).astype(BF16)
    b_t = b_s[l].T
    gv = g_v[l].reshape(1, a_width)
    alog_row = _pad_lanes(a_log[l])
    dtb_row = _pad_lanes(dt_bias[l])
    gout = g_out[l].reshape(1, B_DIM)
    w_o = w_out[l]
    gf = g_ffn[l].reshape(1, d)
    wr = jnp.pad(jnp.concatenate([w_group[l], w_router[l]], axis=1),
                 ((0, 0), (0, LANES - N_GROUPS - N_EXPERTS)))
    wr_hi = wr.astype(BF16)
    wr_t = jnp.concatenate([wr_hi, (wr - wr_hi.astype(F32)).astype(BF16)], axis=1)
    br_col = jnp.broadcast_to(
        jnp.pad(jnp.concatenate([b_group[l], b_router[l]]), (0, ROUTER_ROWS - N_GROUPS - N_EXPERTS))[:, None],
        (ROUTER_ROWS, LANES))
    gp = g_ple[l].reshape(1, d)
    w_pg = w_ple_gate[l]
    w_p = w_ple[l].astype(BF16)
    gfin = g_final.reshape(1, d)

    proj_p, tail_p, oa_p = _inproj_mixa(xp, gm, w_main, w_tail, ws_tril, b_t, gv, n_main)
    proj_s, tail_s = _inproj(xs, gm, w_main, w_tail, n_main, time_major=True)
    oa_s, va_s = _mixa_sample(proj_s, w_s[l], b_s[l], gv, n_seq, n_pos)
    ob_p, sd_p, cb_p = _gdn_prompt(proj_p, tail_p, conv_w[l], alog_row, dtb_row, gout, n_batch, seq)
    cbuf = state_conv[l].reshape(n_seq, (CONV_W - 1) * 3 * hd)
    ob_s, sd_s, cb_s = _gdn_decode(proj_s, tail_s, cbuf, state_delta[l], conv_w[l], alog_row, dtb_row,
                                   gout, n_seq, n_pos)
    ob_s = ob_s.reshape(n_s, hd)

    x1_p, h2_p, lt_p = _outproj(xp, oa_p, ob_p, w_o, gf, wr_t, br_col, time_major=False)
    x1_s, h2_s, lt_s = _outproj(xs, oa_s, ob_s, w_o, gf, wr_t, br_col, time_major=True)
    dest, wcol, meta = _router(lt_p, lt_s)
    d0, d1 = dest[0], dest[1]
    n_active = meta[0, 0:1]
    block_start = meta[1, :N_EXPERTS]
    block_count = meta[2, :N_EXPERTS]

    n_tok = n_p + n_s
    n_blocks = (n_tok * 2) // MOE_BLOCK + N_EXPERTS
    xb = _dispatch(d0, d1, block_start, block_count, n_active, h2_p, h2_s, n_blocks)
    yb = _ffn(block_start, block_count, n_active, xb, w_gate[l], w_up[l], w_down[l])

    pp = p_prompt[l].reshape(n_p, -1)
    ps = p_sample[l].reshape(n_seq, -1)
    y_p = _ple(d0[:n_p], d1[:n_p], x1_p, wcol[:n_p], pp, yb, gp, w_pg, w_p, gfin, time_major=False)
    y_s = _ple(d0[n_p:], d1[n_p:], x1_s, wcol[n_p:], ps, yb, gp, w_pg, w_p, gfin, time_major=True)

    return (
        y_p.reshape(n_batch, seq, d),
        y_s.reshape(n_seq, n_pos, d),
        cb_p[None],
        sd_p[None],
        cb_s.reshape(n_seq, CONV_W - 1, 3 * hd)[None],
        sd_s[None],
        va_s.reshape(n_seq, n_pos, a_width)[None],
    )
```

```python
from functools import partial

import jax
import jax.numpy as jnp
from jax import lax
from jax.experimental import pallas as pl
from jax.experimental.pallas import tpu as pltpu

F32 = jnp.float32
BF16 = jnp.bfloat16
I32 = jnp.int32
U32 = jnp.uint32

EPS = 1e-6
LANES = 128
A_GROUPS = 8
A_GROUP_DIM = 128
A_CHUNK = 128
B_HEADS = 8
B_DIM = 128
CONV_W = 4
N_GROUPS = 4
EXPERTS_PER_GROUP = 8
N_EXPERTS = N_GROUPS * EXPERTS_PER_GROUP
ROUTER_ROWS = 40
TOKEN_TILE = 512
INPROJ_TILE = 1024
PLE_TILE = 256
GDN_CHUNK = 128
MOE_BLOCK = 128
FFN_RING = 8
DEC_SEQ_TILE = 16
ROW_DMA_UNROLL = 8
VMEM_LIMIT = 56 * 1024 * 1024


def _cparams(sem):
    return pltpu.CompilerParams(dimension_semantics=sem, vmem_limit_bytes=VMEM_LIMIT)


def _rms(x, g):
    return x * lax.rsqrt(jnp.mean(x * x, axis=-1, keepdims=True) + EPS) * g


def _softplus(x):
    return jnp.maximum(x, 0.0) + jnp.log(1.0 + jnp.exp(-jnp.abs(x)))


def _dot(a, b):
    return jnp.dot(a, b, preferred_element_type=F32)


def _dot_nt(a, b, precision=None):
    return lax.dot_general(a, b, (((1,), (1,)), ((), ())), precision=precision,
                           preferred_element_type=F32)


def _dot_tn(a, b):
    return lax.dot_general(a, b, (((0,), (0,)), ((), ())), preferred_element_type=F32)


def _pack_bf16_pairs(x):
    n = x.shape[1] // 2
    lo = lax.bitcast_convert_type(x[:, :n].astype(BF16).astype(F32), U32) >> 16
    hi = lax.bitcast_convert_type(x[:, n:].astype(BF16).astype(F32), U32) & jnp.uint32(0xFFFF0000)
    return hi | lo


def _unpack_pairs_f32(xu):
    lo = lax.bitcast_convert_type(xu << 16, F32)
    hi = lax.bitcast_convert_type(xu & jnp.uint32(0xFFFF0000), F32)
    return lo, hi


def _unpack_bf16_pairs(xu):
    lo, hi = _unpack_pairs_f32(xu)
    return lo.astype(BF16), hi.astype(BF16)


def _inproj_kernel(x_ref, g_ref, w_ref, wt_ref, proj_ref, tail_ref, h_scr, *, time_major):
    j = pl.program_id(1)

    @pl.when(j == 0)
    def _():
        g = g_ref[...]
        if time_major:
            d = g.shape[-1]
            rows = x_ref.shape[0]
            for t in range(x_ref.shape[1] // d):
                h_scr[t * rows:(t + 1) * rows, :] = _rms(x_ref[:, t * d:(t + 1) * d], g).astype(BF16)
        else:
            h_scr[...] = _rms(x_ref[...], g).astype(BF16)
        tail_ref[...] = _dot_nt(h_scr[...], wt_ref[...])

    proj_ref[...] = _dot_nt(h_scr[...], w_ref[...])


def _inproj(x2d, g_mix, w_main, w_tail, n_main, *, time_major, col_tile=1024):
    d = g_mix.shape[-1]
    if time_major:
        n_tok = x2d.shape[0] * (x2d.shape[1] // d)
        tm = n_tok
        x_spec = pl.BlockSpec(x2d.shape, lambda i, j: (0, 0))
    else:
        n_tok = x2d.shape[0]
        tm = min(INPROJ_TILE, n_tok)
        x_spec = pl.BlockSpec((tm, d), lambda i, j: (i, 0))
    grid = (n_tok // tm, n_main // col_tile)
    return pl.pallas_call(
        partial(_inproj_kernel, time_major=time_major),
        grid=grid,
        in_specs=[
            x_spec,
            pl.BlockSpec((1, d), lambda i, j: (0, 0)),
            pl.BlockSpec((col_tile, d), lambda i, j: (j, 0)),
            pl.BlockSpec((LANES, d), lambda i, j: (0, 0)),
        ],
        out_specs=[
            pl.BlockSpec((tm, col_tile), lambda i, j: (i, j)),
            pl.BlockSpec((tm, LANES), lambda i, j: (i, 0)),
        ],
        out_shape=[
            jax.ShapeDtypeStruct((n_tok, n_main), F32),
            jax.ShapeDtypeStruct((n_tok, LANES), F32),
        ],
        scratch_shapes=[pltpu.VMEM((tm, d), BF16)],
        compiler_params=_cparams(("parallel", "arbitrary")),
        name="inproj_tm" if time_major else "inproj",
    )(x2d, g_mix, w_main, w_tail)


def _group_ln(v, g):
    mu = jnp.mean(v, axis=-1, keepdims=True)
    dlt = v - mu
    var = jnp.mean(dlt * dlt, axis=-1, keepdims=True)
    return dlt * lax.rsqrt(var + EPS) * g


def _inproj_mixa_kernel(x_ref, g_ref, w_ref, wt_ref, ws_ref, bt_ref, gv_ref,
                        proj_ref, tail_ref, oa_ref, h_scr, u_scr, v_scr, *, n_later):
    j = pl.program_id(1)
    tm = h_scr.shape[0]

    @pl.when(j == 0)
    def _():
        h_scr[...] = _rms(x_ref[...], g_ref[...]).astype(BF16)
        tail_ref[...] = _dot_nt(h_scr[...], wt_ref[...])
        u_scr[...] = _dot_nt(h_scr[...], w_ref[...])

    @pl.when(j == 1)
    def _():
        v_scr[...] = _dot_nt(h_scr[...], w_ref[...])

    @pl.when(j >= 2)
    def _():
        proj_ref[...] = _dot_nt(h_scr[...], w_ref[...])
        share = tm // n_later
        r0 = (j - 2) * share
        for c in range(share // A_CHUNK):
            rows = pl.ds(pl.multiple_of(r0 + c * A_CHUNK, A_CHUNK), A_CHUNK)
            for h in range(A_GROUPS):
                cols = slice(h * A_GROUP_DIM, (h + 1) * A_GROUP_DIM)
                u = jax.nn.gelu(u_scr[rows, cols])
                v = _group_ln(jax.nn.gelu(v_scr[rows, cols]), gv_ref[:, cols])
                mixed = _dot(ws_ref[h], v.astype(BF16)) + bt_ref[:, h:h + 1]
                oa_ref[rows, cols] = (u * mixed).astype(BF16)


def _inproj_mixa(x2d, g_mix, w_main, w_tail, ws_tril, b_t, g_v, n_main):
    n_tok, d = x2d.shape
    a_width = A_GROUPS * A_GROUP_DIM
    tm = min(INPROJ_TILE, n_tok)
    n_col = n_main // a_width
    return pl.pallas_call(
        partial(_inproj_mixa_kernel, n_later=n_col - 2),
        grid=(n_tok // tm, n_col),
        in_specs=[
            pl.BlockSpec((tm, d), lambda i, j: (i, 0)),
            pl.BlockSpec((1, d), lambda i, j: (0, 0)),
            pl.BlockSpec((a_width, d), lambda i, j: (j, 0)),
            pl.BlockSpec((LANES, d), lambda i, j: (0, 0)),
            pl.BlockSpec((A_GROUPS, A_CHUNK, A_CHUNK), lambda i, j: (0, 0, 0)),
            pl.BlockSpec((A_CHUNK, A_GROUPS), lambda i, j: (0, 0)),
            pl.BlockSpec((1, a_width), lambda i, j: (0, 0)),
        ],
        out_specs=[
            pl.BlockSpec((tm, a_width), lambda i, j: (i, jnp.maximum(j - 2, 0))),
            pl.BlockSpec((tm, LANES), lambda i, j: (i, 0)),
            pl.BlockSpec((tm, a_width), lambda i, j: (i, 0)),
        ],
        out_shape=[
            jax.ShapeDtypeStruct((n_tok, n_main - 2 * a_width), F32),
            jax.ShapeDtypeStruct((n_tok, LANES), F32),
            jax.ShapeDtypeStruct((n_tok, a_width), BF16),
        ],
        scratch_shapes=[pltpu.VMEM((tm, d), BF16), pltpu.VMEM((tm, a_width), F32),
                        pltpu.VMEM((tm, a_width), F32)],
        compiler_params=_cparams(("parallel", "arbitrary")),
        name="inproj_mixa",
    )(x2d, g_mix, w_main, w_tail, ws_tril, b_t, g_v)


def _mixa_sample_kernel(ws_ref, bs_ref, au_ref, av_ref, gv_ref, out_ref, va_ref, *, n_seq, n_pos):
    a_width = A_GROUPS * A_GROUP_DIM
    vs = []
    for t in range(n_pos):
        rows = slice(t * n_seq, (t + 1) * n_seq)
        per_group = []
        for h in range(A_GROUPS):
            cols = slice(h * A_GROUP_DIM, (h + 1) * A_GROUP_DIM)
            per_group.append(_group_ln(jax.nn.gelu(av_ref[rows, cols]), gv_ref[:, cols]))
        vs.append(per_group)
        for h in range(A_GROUPS):
            va_ref[:, t * a_width + h * A_GROUP_DIM:t * a_width + (h + 1) * A_GROUP_DIM] = per_group[h]
    for t in range(n_pos):
        rows = slice(t * n_seq, (t + 1) * n_seq)
        for h in range(A_GROUPS):
            cols = slice(h * A_GROUP_DIM, (h + 1) * A_GROUP_DIM)
            mixed = jnp.zeros((n_seq, A_GROUP_DIM), F32) + bs_ref[h * n_pos + t]
            for s in range(t + 1):
                mixed = mixed + ws_ref[(h * n_pos + t) * n_pos + s] * vs[s][h]
            out_ref[rows, cols] = (jax.nn.gelu(au_ref[rows, cols]) * mixed).astype(BF16)


def _mixa_sample(proj, w_s, b_s, g_v, n_seq, n_pos):
    a_width = A_GROUPS * A_GROUP_DIM
    n_tok = n_seq * n_pos
    return pl.pallas_call(
        partial(_mixa_sample_kernel, n_seq=n_seq, n_pos=n_pos),
        grid_spec=pltpu.PrefetchScalarGridSpec(
            num_scalar_prefetch=2,
            grid=(1,),
            in_specs=[
                pl.BlockSpec((n_tok, a_width), lambda i, *_: (0, 0)),
                pl.BlockSpec((n_tok, a_width), lambda i, *_: (0, 1)),
                pl.BlockSpec((1, a_width), lambda i, *_: (0, 0)),
            ],
            out_specs=[
                pl.BlockSpec((n_tok, a_width), lambda i, *_: (0, 0)),
                pl.BlockSpec((n_seq, n_pos * a_width), lambda i, *_: (0, 0)),
            ],
        ),
        out_shape=[
            jax.ShapeDtypeStruct((n_tok, a_width), BF16),
            jax.ShapeDtypeStruct((n_seq, n_pos * a_width), F32),
        ],
        compiler_params=_cparams(("arbitrary",)),
        name="mixa_sample",
    )(w_s[:, :n_pos, :n_pos].reshape(-1), b_s[:, :n_pos].reshape(-1), proj, proj, g_v)


def _gdn_prompt_kernel(q_ref, k_ref, v_ref, z_ref, tail_ref, cw_ref, alog_ref, dtb_ref, gout_ref,
                       out_ref, s_out_ref, conv_out_ref, s_scr, xbuf):
    c = pl.program_id(1)
    n_c = pl.num_programs(1)
    C = q_ref.shape[0]
    hd = B_HEADS * B_DIM
    pad = 8
    lo = pad - (CONV_W - 1)

    @pl.when(c == 0)
    def _():
        s_scr[...] = jnp.zeros_like(s_scr)
        xbuf[0:pad, :] = jnp.zeros((pad, 3 * hd), F32)

    xbuf[pad:pad + C, 0:hd] = q_ref[...]
    xbuf[pad:pad + C, hd:2 * hd] = k_ref[...]
    xbuf[pad:pad + C, 2 * hd:3 * hd] = v_ref[...]
    xall = xbuf[...]
    y = xall * cw_ref[0:1, :]
    for j in range(1, CONV_W):
        y = pltpu.roll(y, 1, 0) + xall * cw_ref[j:j + 1, :]
    y = y[pad:pad + C, :]
    y = y * jax.nn.sigmoid(y)
    last = xbuf[pad + C - (CONV_W - 1):pad + C, :]
    xbuf[lo:pad, :] = last

    @pl.when(c == n_c - 1)
    def _():
        conv_out_ref[0] = last

    tail = tail_ref[...]
    g_all = -jnp.exp(alog_ref[...]) * _softplus(tail + dtb_ref[...])
    beta_all = jax.nn.sigmoid(tail)
    ri = lax.broadcasted_iota(I32, (C, C), 0)
    ci = lax.broadcasted_iota(I32, (C, C), 1)
    causal = ci <= ri
    strict = ci < ri
    tril = causal.astype(F32)
    G_all = jnp.dot(tril, g_all, precision=lax.Precision.HIGHEST, preferred_element_type=F32)
    G_rows = jnp.concatenate([G_all, jnp.zeros((LANES - C, LANES), F32)], axis=0).T if C < LANES else G_all.T
    scale = B_DIM ** -0.5
    n_levels = max(C.bit_length() - 2, 0)
    heads = range(B_HEADS)

    qs, ks, Gs, eGs, rhss, qks, Ps = [], [], [], [], [], [], []
    for h in heads:
        q = y[:, h * B_DIM:(h + 1) * B_DIM]
        k = y[:, hd + h * B_DIM:hd + (h + 1) * B_DIM]
        v = y[:, 2 * hd + h * B_DIM:2 * hd + (h + 1) * B_DIM]
        q = q * lax.rsqrt(jnp.sum(q * q, axis=-1, keepdims=True) + EPS) * scale
        k = k * lax.rsqrt(jnp.sum(k * k, axis=-1, keepdims=True) + EPS)
        G = G_all[:, h:h + 1]
        beta = beta_all[:, B_HEADS + h:B_HEADS + h + 1]
        eG = jnp.exp(G)
        decay = jnp.exp(jnp.where(causal, G - G_rows[h:h + 1, 0:C], -jnp.inf))
        qk_kk = _dot_nt(jnp.concatenate([q, k], axis=0), k)
        qks.append(qk_kk[:C] * decay)
        Ps.append(jnp.where(strict, -(beta * qk_kk[C:] * decay), 0.0))
        rhss.append(jnp.concatenate([v * beta, k * (beta * eG)], axis=1))
        qs.append(q)
        ks.append(k)
        Gs.append(G)
        eGs.append(eG)

    eye = (ri == ci).astype(F32)
    Ms = [eye + P for P in Ps]
    if n_levels > 0:
        Ps = [_dot(P, P) for P in Ps]
    for lvl in range(1, n_levels + 1):
        for h in heads:
            if lvl < n_levels:
                R = _dot(jnp.concatenate([Ps[h], Ms[h]], axis=0), Ps[h])
                Ms[h] = Ms[h] + R[C:]
                Ps[h] = R[:C]
            else:
                Ms[h] = Ms[h] + _dot(Ms[h], Ps[h])
    Ls = [M - eye for M in Ms]

    uws = [rhss[h] + _dot(Ls[h], rhss[h]) for h in heads]
    Ss = [s_scr[h] for h in heads]
    rs = [_dot(jnp.concatenate([uws[h][:, B_DIM:], qs[h] * eGs[h]], axis=0), Ss[h]) for h in heads]
    v_news = [uws[h][:, :B_DIM] - rs[h][:C] for h in heads]
    os_ = [rs[h][C:] + _dot(qks[h], v_news[h]) for h in heads]
    for h in heads:
        G_last = Gs[h][C - 1:C, :]
        kd = ks[h] * jnp.exp(G_last - Gs[h])
        s_scr[h] = Ss[h] * jnp.exp(G_last) + _dot_tn(kd, v_news[h])
    for h in heads:
        cols = slice(h * B_DIM, (h + 1) * B_DIM)
        o = _rms(os_[h], gout_ref[...])
        z = z_ref[:, cols]
        out_ref[:, cols] = (o * (z * jax.nn.sigmoid(z))).astype(BF16)

    @pl.when(c == n_c - 1)
    def _():
        s_out_ref[0] = s_scr[...]


def _gdn_prompt(proj, tail, conv_w, alog_row, dtb_row, g_out, n_batch, seq):
    C = GDN_CHUNK
    n_c = seq // C
    hd = B_HEADS * B_DIM
    row = lambda b, c: b * n_c + c
    return pl.pallas_call(
        _gdn_prompt_kernel,
        grid=(n_batch, n_c),
        in_specs=[
            pl.BlockSpec((C, hd), lambda b, c: (row(b, c), 0)),
            pl.BlockSpec((C, hd), lambda b, c: (row(b, c), 1)),
            pl.BlockSpec((C, hd), lambda b, c: (row(b, c), 2)),
            pl.BlockSpec((C, hd), lambda b, c: (row(b, c), 3)),
            pl.BlockSpec((C, LANES), lambda b, c: (row(b, c), 0)),
            pl.BlockSpec((CONV_W, 3 * hd), lambda b, c: (0, 0)),
            pl.BlockSpec((1, LANES), lambda b, c: (0, 0)),
            pl.BlockSpec((1, LANES), lambda b, c: (0, 0)),
            pl.BlockSpec((1, B_DIM), lambda b, c: (0, 0)),
        ],
        out_specs=[
            pl.BlockSpec((C, hd), lambda b, c: (row(b, c), 0)),
            pl.BlockSpec((1, B_HEADS, B_DIM, B_DIM), lambda b, c: (b, 0, 0, 0)),
            pl.BlockSpec((1, CONV_W - 1, 3 * hd), lambda b, c: (b, 0, 0)),
        ],
        out_shape=[
            jax.ShapeDtypeStruct((n_batch * seq, hd), BF16),
            jax.ShapeDtypeStruct((n_batch, B_HEADS, B_DIM, B_DIM), F32),
            jax.ShapeDtypeStruct((n_batch, CONV_W - 1, 3 * hd), F32),
        ],
        scratch_shapes=[
            pltpu.VMEM((B_HEADS, B_DIM, B_DIM), F32),
            pltpu.VMEM((8 + C, 3 * hd), F32),
        ],
        compiler_params=_cparams(("parallel", "arbitrary")),
        name="gdn_prompt",
    )(proj, proj, proj, proj, tail, conv_w, alog_row, dtb_row, g_out)


def _gdn_decode_kernel(*refs, n_pos):
    proj_refs = refs[:n_pos]
    tail_refs = refs[n_pos:2 * n_pos]
    (cbuf_ref, s_ref, cw_ref, alog_ref, dtb_ref, gout_ref,
     out_ref, s_out_ref, conv_out_ref, lhs_scr, res_scr, kd_scr, vn_scr, gl_scr) = refs[2 * n_pos:]
    SB = s_ref.shape[0]
    hd = B_HEADS * B_DIM
    qkv0 = 2 * hd
    z0 = qkv0 + 3 * hd
    scale = B_DIM ** -0.5

    xp = [cbuf_ref[:, j * 3 * hd:(j + 1) * 3 * hd] for j in range(CONV_W - 1)]
    xp += [proj_refs[t][:, qkv0:qkv0 + 3 * hd] for t in range(n_pos)]
    for j in range(CONV_W - 1):
        conv_out_ref[:, j * 3 * hd:(j + 1) * 3 * hd] = xp[n_pos + j]
    ys = []
    for t in range(n_pos):
        y = xp[t] * cw_ref[0:1, :]
        for j in range(1, CONV_W):
            y = y + xp[t + j] * cw_ref[j:j + 1, :]
        ys.append(y * jax.nn.sigmoid(y))
    gs, betas = [], []
    for t in range(n_pos):
        tail = tail_refs[t][...]
        gs.append(-jnp.exp(alog_ref[...]) * _softplus(tail + dtb_ref[...]))
        betas.append(jax.nn.sigmoid(tail))

    kd_scr[...] = jnp.zeros_like(kd_scr)
    vn_scr[...] = jnp.zeros_like(vn_scr)
    heads = range(B_HEADS)
    stride = 2 * n_pos

    qs, ks, Gs, us = [], [], [], []
    for h in heads:
        q_h, k_h, v_h, G_h, b_h = [], [], [], [], []
        G = None
        for t in range(n_pos):
            q = ys[t][:, h * B_DIM:(h + 1) * B_DIM]
            k = ys[t][:, hd + h * B_DIM:hd + (h + 1) * B_DIM]
            q_h.append(q * lax.rsqrt(jnp.sum(q * q, axis=-1, keepdims=True) + EPS) * scale)
            k_h.append(k * lax.rsqrt(jnp.sum(k * k, axis=-1, keepdims=True) + EPS))
            v_h.append(ys[t][:, 2 * hd + h * B_DIM:2 * hd + (h + 1) * B_DIM])
            g = gs[t][:, h:h + 1]
            G = g if G is None else G + g
            G_h.append(G)
            b_h.append(betas[t][:, B_HEADS + h:B_HEADS + h + 1])
        u_h, w_h = [], []
        for t in range(n_pos):
            u = v_h[t] * b_h[t]
            w = k_h[t] * (b_h[t] * jnp.exp(G_h[t]))
            for s in range(t):
                a = b_h[t] * jnp.sum(k_h[t] * k_h[s], axis=-1, keepdims=True) * jnp.exp(G_h[t] - G_h[s])
                u = u - a * u_h[s]
                w = w - a * w_h[s]
            u_h.append(u)
            w_h.append(w)
        for t in range(n_pos):
            lhs_scr[h, pl.ds(t, SB, stride=stride), :] = w_h[t]
            lhs_scr[h, pl.ds(n_pos + t, SB, stride=stride), :] = q_h[t] * jnp.exp(G_h[t])
        qs.append(q_h)
        ks.append(k_h)
        Gs.append(G_h)
        us.append(u_h)

    for h in heads:
        for s in range(SB):
            rows = slice(s * stride, (s + 1) * stride)
            res_scr[h, rows, :] = _dot(lhs_scr[h, rows, :], s_ref[s, h])

    for h in heads:
        cols = slice(h * B_DIM, (h + 1) * B_DIM)
        v_news = [us[h][t] - res_scr[h, pl.ds(t, SB, stride=stride), :] for t in range(n_pos)]
        G_last = Gs[h][n_pos - 1]
        for t in range(n_pos):
            o = res_scr[h, pl.ds(n_pos + t, SB, stride=stride), :]
            for s in range(t + 1):
                qk = jnp.sum(qs[h][t] * ks[h][s], axis=-1, keepdims=True) * jnp.exp(Gs[h][t] - Gs[h][s])
                o = o + qk * v_news[s]
            o = _rms(o, gout_ref[...])
            z = proj_refs[t][:, z0 + h * B_DIM:z0 + (h + 1) * B_DIM]
            out_ref[t, :, cols] = (o * (z * jax.nn.sigmoid(z))).astype(BF16)
            kd_scr[h, pl.ds(t, SB, stride=stride), :] = ks[h][t] * jnp.exp(G_last - Gs[h][t])
            vn_scr[h, pl.ds(t, SB, stride=stride), :] = v_news[t]
        gl_scr[h] = jnp.broadcast_to(jnp.exp(G_last), (SB, B_DIM))

    for h in heads:
        for s in range(SB):
            rows = slice(s * stride, (s + 1) * stride)
            s_out_ref[s, h] = (s_ref[s, h] * gl_scr[h, s:s + 1, 0:1]
                               + _dot_tn(kd_scr[h, rows, :], vn_scr[h, rows, :]))


def _gdn_decode(proj_s, tail_s, cbuf, s0, conv_w, alog_row, dtb_row, g_out, n_seq, n_pos):
    SB = DEC_SEQ_TILE
    hd = B_HEADS * B_DIM
    n_main = proj_s.shape[1]
    per_pos = n_seq // SB
    proj_specs = [pl.BlockSpec((SB, n_main), lambda i, t=t: (t * per_pos + i, 0)) for t in range(n_pos)]
    tail_specs = [pl.BlockSpec((SB, LANES), lambda i, t=t: (t * per_pos + i, 0)) for t in range(n_pos)]
    return pl.pallas_call(
        partial(_gdn_decode_kernel, n_pos=n_pos),
        grid=(per_pos,),
        in_specs=proj_specs + tail_specs + [
            pl.BlockSpec((SB, (CONV_W - 1) * 3 * hd), lambda i: (i, 0)),
            pl.BlockSpec((SB, B_HEADS, B_DIM, B_DIM), lambda i: (i, 0, 0, 0)),
            pl.BlockSpec((CONV_W, 3 * hd), lambda i: (0, 0)),
            pl.BlockSpec((1, LANES), lambda i: (0, 0)),
            pl.BlockSpec((1, LANES), lambda i: (0, 0)),
            pl.BlockSpec((1, B_DIM), lambda i: (0, 0)),
        ],
        out_specs=[
            pl.BlockSpec((n_pos, SB, hd), lambda i: (0, i, 0)),
            pl.BlockSpec((SB, B_HEADS, B_DIM, B_DIM), lambda i: (i, 0, 0, 0)),
            pl.BlockSpec((SB, (CONV_W - 1) * 3 * hd), lambda i: (i, 0)),
        ],
        out_shape=[
            jax.ShapeDtypeStruct((n_pos, n_seq, hd), BF16),
            jax.ShapeDtypeStruct((n_seq, B_HEADS, B_DIM, B_DIM), F32),
            jax.ShapeDtypeStruct((n_seq, (CONV_W - 1) * 3 * hd), F32),
        ],
        scratch_shapes=[
            pltpu.VMEM((B_HEADS, SB * 2 * n_pos, B_DIM), F32),
            pltpu.VMEM((B_HEADS, SB * 2 * n_pos, B_DIM), F32),
            pltpu.VMEM((B_HEADS, SB * 2 * n_pos, B_DIM), F32),
            pltpu.VMEM((B_HEADS, SB * 2 * n_pos, B_DIM), F32),
            pltpu.VMEM((B_HEADS, SB, B_DIM), F32),
        ],
        compiler_params=_cparams(("parallel",)),
        name="gdn_decode",
    )(*([proj_s] * n_pos), *([tail_s] * n_pos), cbuf, s0, conv_w, alog_row, dtb_row, g_out)


def _outproj_kernel(x_ref, oa_ref, ob_ref, w_ref, g_ref, wr_ref, br_ref,
                    x1_ref, h2_ref, lt_ref, w_bf, *, time_major):
    a_width = oa_ref.shape[1]

    @pl.when(pl.program_id(0) == 0)
    def _():
        w_bf[...] = w_ref[...].astype(BF16)

    acc = _dot(oa_ref[...], w_bf[0:a_width, :]) + _dot(ob_ref[...], w_bf[a_width:, :])
    if time_major:
        d = g_ref.shape[-1]
        rows = x_ref.shape[0]
        x = jnp.concatenate([x_ref[:, t * d:(t + 1) * d] for t in range(x_ref.shape[1] // d)], axis=0)
    else:
        x = x_ref[...]
    x1 = x + acc
    x1_ref[...] = x1
    h2 = _rms(x1, g_ref[...])
    h2_ref[...] = _pack_bf16_pairs(h2)
    h_hi = h2.astype(BF16)
    h_lo = (h2 - h_hi.astype(F32)).astype(BF16)
    tm = h2.shape[0]
    lg = _dot(jnp.concatenate([h_hi, h_lo], axis=0), wr_ref[...])
    lg = lg[:tm, :LANES] + lg[:tm, LANES:] + lg[tm:, :LANES] + lg[tm:, LANES:]
    lt_ref[...] = lg.T[0:ROUTER_ROWS, :] + br_ref[:, 0:1]


def _outproj(x2d, out_a, out_b, w_out, g_ffn, wr_t, br_col, *, time_major):
    d = g_ffn.shape[-1]
    a_width = out_a.shape[1]
    n_tok = out_a.shape[0]
    tm = TOKEN_TILE
    if time_major:
        x_spec = pl.BlockSpec(x2d.shape, lambda i: (0, 0))
    else:
        x_spec = pl.BlockSpec((tm, d), lambda i: (i, 0))
    return pl.pallas_call(
        partial(_outproj_kernel, time_major=time_major),
        grid=(n_tok // tm,),
        in_specs=[
            x_spec,
            pl.BlockSpec((tm, a_width), lambda i: (i, 0)),
            pl.BlockSpec((tm, out_b.shape[1]), lambda i: (i, 0)),
            pl.BlockSpec(w_out.shape, lambda i: (0, 0), pipeline_mode=pl.Buffered(1)),
            pl.BlockSpec((1, d), lambda i: (0, 0)),
            pl.BlockSpec(wr_t.shape, lambda i: (0, 0)),
            pl.BlockSpec(br_col.shape, lambda i: (0, 0)),
        ],
        out_specs=[
            pl.BlockSpec((tm, d), lambda i: (i, 0)),
            pl.BlockSpec((tm, d // 2), lambda i: (i, 0)),
            pl.BlockSpec((ROUTER_ROWS, tm), lambda i: (0, i)),
        ],
        out_shape=[
            jax.ShapeDtypeStruct((n_tok, d), F32),
            jax.ShapeDtypeStruct((n_tok, d // 2), U32),
            jax.ShapeDtypeStruct((ROUTER_ROWS, n_tok), F32),
        ],
        scratch_shapes=[pltpu.VMEM(w_out.shape, BF16)],
        compiler_params=_cparams(("arbitrary",)),
        name="outproj_tm" if time_major else "outproj",
    )(x2d, out_a, out_b, w_out, g_ffn, wr_t, br_col)


def _router_kernel(lp_ref, ls_ref, dest_ref, wcol_ref, meta_ref, cnt_scr, base_scr, *, n_prompt_tiles):
    p = pl.program_id(0)
    i = pl.program_id(1)
    tm = lp_ref.shape[1]
    lt = jnp.where(i < n_prompt_tiles, lp_ref[...], ls_ref[...])

    m = lt[0:1, :]
    sel = jnp.zeros((1, tm), I32)
    for r in range(1, N_GROUPS):
        upd = lt[r:r + 1, :] > m
        sel = jnp.where(upd, r, sel)
        m = jnp.where(upd, lt[r:r + 1, :], m)
    den = jnp.zeros((1, tm), F32)
    for r in range(N_GROUPS):
        den = den + jnp.exp(lt[r:r + 1, :] - m)
    g_w = 1.0 / den

    ev = []
    for j in range(EXPERTS_PER_GROUP):
        e = jnp.zeros((1, tm), F32)
        for g in range(N_GROUPS):
            row = N_GROUPS + g * EXPERTS_PER_GROUP + j
            e = jnp.where(sel == g, lt[row:row + 1, :], e)
        ev.append(e)
    v0 = ev[0]
    i0 = jnp.zeros((1, tm), I32)
    for j in range(1, EXPERTS_PER_GROUP):
        upd = ev[j] > v0
        i0 = jnp.where(upd, j, i0)
        v0 = jnp.where(upd, ev[j], v0)
    v1 = jnp.full((1, tm), -jnp.inf, F32)
    i1 = jnp.zeros((1, tm), I32)
    for j in range(EXPERTS_PER_GROUP):
        upd = (ev[j] > v1) & (i0 != j)
        i1 = jnp.where(upd, j, i1)
        v1 = jnp.where(upd, ev[j], v1)
    t = jnp.exp(v1 - v0)
    w0 = g_w / (1.0 + t)
    w1 = g_w * t / (1.0 + t)
    e0 = sel * EXPERTS_PER_GROUP + i0
    e1 = sel * EXPERTS_PER_GROUP + i1

    eio = lax.broadcasted_iota(I32, (N_EXPERTS, tm), 0)
    hit0 = eio == e0
    hit1 = eio == e1
    onehot = (hit0 | hit1).astype(F32)
    tile_cnt = jnp.sum(onehot, axis=1, keepdims=True)

    @pl.when((p == 0) & (i == 0))
    def _():
        cnt_scr[...] = jnp.zeros_like(cnt_scr)

    @pl.when(p == 0)
    def _():
        cnt_scr[...] = cnt_scr[...] + tile_cnt

    @pl.when((p == 1) & (i == 0))
    def _():
        cnt = cnt_scr[...]
        padded = jnp.floor((cnt + (MOE_BLOCK - 1)) * (1.0 / MOE_BLOCK)) * MOE_BLOCK
        ri = lax.broadcasted_iota(I32, (N_EXPERTS, N_EXPERTS), 0)
        ci = lax.broadcasted_iota(I32, (N_EXPERTS, N_EXPERTS), 1)
        lower = (ci < ri).astype(F32)
        pad_start = jnp.dot(lower, jnp.broadcast_to(padded, (N_EXPERTS, LANES)),
                            precision=lax.Precision.HIGHEST, preferred_element_type=F32)[:, 0:1]
        base_scr[...] = pad_start
        pad_end = pad_start + padded
        n_active = jnp.broadcast_to(pad_end[N_EXPERTS - 1:N_EXPERTS, :] * (1.0 / MOE_BLOCK), (1, LANES))
        diag = (lax.broadcasted_iota(I32, (N_EXPERTS, LANES), 0)
                == lax.broadcasted_iota(I32, (N_EXPERTS, LANES), 1))
        blk_start = jnp.sum(jnp.where(diag, pad_start * (1.0 / MOE_BLOCK), 0.0), axis=0, keepdims=True)
        blk_count = jnp.sum(jnp.where(diag, padded * (1.0 / MOE_BLOCK), 0.0), axis=0, keepdims=True)
        meta_ref[...] = jnp.concatenate(
            [n_active, blk_start, blk_count, jnp.zeros((5, LANES), F32)], axis=0).astype(I32)

    @pl.when(p == 1)
    def _():
        ui = lax.broadcasted_iota(I32, (tm, tm), 0)
        uj = lax.broadcasted_iota(I32, (tm, tm), 1)
        upper = (ui < uj).astype(BF16)
        excl = _dot(onehot.astype(BF16), upper)
        pos = base_scr[...] + excl
        d0 = jnp.sum(jnp.where(hit0, pos, 0.0), axis=0, keepdims=True)
        d1 = jnp.sum(jnp.where(hit1, pos, 0.0), axis=0, keepdims=True)
        dest_ref[...] = jnp.concatenate([d0, d1, jnp.zeros((6, tm), F32)], axis=0).astype(I32)
        wmat = jnp.concatenate([w0, w1, jnp.zeros((LANES - 2, tm), F32)], axis=0)
        wcol_ref[...] = wmat.T
        base_scr[...] = base_scr[...] + tile_cnt


def _router(lt_p, lt_s):
    tm = TOKEN_TILE
    n_p = lt_p.shape[1] // tm
    n_s = lt_s.shape[1] // tm
    n_tok = lt_p.shape[1] + lt_s.shape[1]
    return pl.pallas_call(
        partial(_router_kernel, n_prompt_tiles=n_p),
        grid=(2, n_p + n_s),
        in_specs=[
            pl.BlockSpec((ROUTER_ROWS, tm), lambda p, i: (0, jnp.minimum(i, n_p - 1))),
            pl.BlockSpec((ROUTER_ROWS, tm), lambda p, i: (0, jnp.maximum(i - n_p, 0))),
        ],
        out_specs=[
            pl.BlockSpec((8, tm), lambda p, i: (0, i * p)),
            pl.BlockSpec((tm, LANES), lambda p, i: (i * p, 0)),
            pl.BlockSpec((8, LANES), lambda p, i: (0, 0)),
        ],
        out_shape=[
            jax.ShapeDtypeStruct((8, n_tok), I32),
            jax.ShapeDtypeStruct((n_tok, LANES), F32),
            jax.ShapeDtypeStruct((8, LANES), I32),
        ],
        scratch_shapes=[pltpu.VMEM((N_EXPERTS, 1), F32), pltpu.VMEM((N_EXPERTS, 1), F32)],
        compiler_params=_cparams(("arbitrary", "arbitrary")),
        name="router",
    )(lt_p, lt_s)


def _dispatch_kernel(d0_ref, d1_ref, bs_ref, nb_ref, na_ref, hp_ref, hs_ref, xb_ref, sem, zbuf, zsem,
                     *, n_prompt_tiles, n_blocks):
    i = pl.program_id(0)
    tm = hp_ref.shape[0]

    @pl.when(i == 0)
    def _():
        zbuf[...] = jnp.zeros_like(zbuf)

        def zero_copy(blk):
            rows = pl.ds(pl.multiple_of(blk * MOE_BLOCK, MOE_BLOCK), MOE_BLOCK)
            return pltpu.make_async_copy(zbuf, xb_ref.at[rows], zsem)

        def last_block(e):
            return bs_ref[e] + jnp.maximum(nb_ref[e], 1) - 1

        def start_e(e, carry):
            @pl.when(nb_ref[e] > 0)
            def _():
                zero_copy(last_block(e)).start()
            return carry

        def wait_e(e, carry):
            @pl.when(nb_ref[e] > 0)
            def _():
                zero_copy(last_block(e)).wait()
            return carry

        def start_b(b, carry):
            zero_copy(b).start()
            return carry

        def wait_b(b, carry):
            zero_copy(b).wait()
            return carry

        lax.fori_loop(0, N_EXPERTS, start_e, 0)
        lax.fori_loop(na_ref[0], n_blocks, start_b, 0)
        lax.fori_loop(0, N_EXPERTS, wait_e, 0)
        lax.fori_loop(na_ref[0], n_blocks, wait_b, 0)

    def scatter(h_ref):
        def copy(r, d_ref):
            return pltpu.make_async_copy(h_ref.at[pl.ds(r, 1)], xb_ref.at[pl.ds(d_ref[i * tm + r], 1)], sem)

        def issue(r, carry):
            copy(r, d0_ref).start(priority=0)
            copy(r, d1_ref).start(priority=1)
            return carry

        def drain(r, carry):
            copy(r, d0_ref).wait()
            copy(r, d1_ref).wait()
            return carry

        lax.fori_loop(0, tm, issue, 0, unroll=ROW_DMA_UNROLL)
        lax.fori_loop(0, tm, drain, 0, unroll=ROW_DMA_UNROLL)

    @pl.when(i < n_prompt_tiles)
    def _():
        scatter(hp_ref)

    @pl.when(i >= n_prompt_tiles)
    def _():
        scatter(hs_ref)


def _dispatch(d0, d1, block_start, block_count, n_active, h2_p, h2_s, n_blocks):
    tm = TOKEN_TILE
    d = h2_p.shape[1]
    n_p = h2_p.shape[0] // tm
    n_s = h2_s.shape[0] // tm
    return pl.pallas_call(
        partial(_dispatch_kernel, n_prompt_tiles=n_p, n_blocks=n_blocks),
        grid_spec=pltpu.PrefetchScalarGridSpec(
            num_scalar_prefetch=5,
            grid=(n_p + n_s,),
            in_specs=[
                pl.BlockSpec((tm, d), lambda i, *_: (jnp.minimum(i, n_p - 1), 0)),
                pl.BlockSpec((tm, d), lambda i, *_: (jnp.maximum(i - n_p, 0), 0)),
            ],
            out_specs=pl.BlockSpec(memory_space=pl.ANY),
            scratch_shapes=[
                pltpu.SemaphoreType.DMA(()),
                pltpu.VMEM((MOE_BLOCK, d), U32),
                pltpu.SemaphoreType.DMA(()),
            ],
        ),
        out_shape=jax.ShapeDtypeStruct((n_blocks * MOE_BLOCK, d), U32),
        compiler_params=_cparams(("arbitrary",)),
        name="dispatch",
    )(d0, d1, block_start, block_count, n_active, h2_p, h2_s)


def _ffn_kernel(bs_ref, nb_ref, na_ref, xb_ref, wg_ref, wu_ref, wd_ref, yb_ref,
                xbuf, ybuf, wg_bf, wu_bf, wd_bf, xsem, ysem, *, n_blocks):
    e = pl.program_id(0)
    nb = nb_ref[e]
    b0 = bs_ref[e]
    na = na_ref[0]
    ring = xbuf.shape[0]

    def x_copy(blk):
        rows = pl.ds(pl.multiple_of(blk * MOE_BLOCK, MOE_BLOCK), MOE_BLOCK)
        slot = lax.rem(blk, ring)
        return pltpu.make_async_copy(xb_ref.at[rows], xbuf.at[slot], xsem.at[slot])

    def y_copy(blk):
        rows = pl.ds(pl.multiple_of(blk * MOE_BLOCK, MOE_BLOCK), MOE_BLOCK)
        slot = lax.rem(blk, ring)
        return pltpu.make_async_copy(ybuf.at[slot], yb_ref.at[rows], ysem.at[slot])

    @pl.when(e == 0)
    def _():
        for ahead in range(ring - 1):
            @pl.when(ahead < na)
            def _():
                x_copy(ahead).start(priority=1)

    @pl.when(nb > 0)
    def _():
        wg_bf[...] = wg_ref[0].astype(BF16)
        wu_bf[...] = wu_ref[0].astype(BF16)
        wd_bf[...] = wd_ref[0].astype(BF16)

    def block(i, carry):
        blk = b0 + i
        slot = lax.rem(blk, ring)
        x_copy(blk).wait()

        @pl.when(blk + ring - 1 < na)
        def _():
            x_copy(blk + ring - 1).start(priority=1)

        x_lo, x_hi = _unpack_bf16_pairs(xbuf[slot])
        half = x_lo.shape[1]
        g = _dot(x_lo, wg_bf[0:half, :]) + _dot(x_hi, wg_bf[half:, :])
        u = _dot(x_lo, wu_bf[0:half, :]) + _dot(x_hi, wu_bf[half:, :])
        a = (g * jax.nn.sigmoid(g) * u).astype(BF16)
        y = _dot(a, wd_bf[...])

        @pl.when(blk >= ring)
        def _():
            y_copy(blk - ring).wait()

        ybuf[slot] = _pack_bf16_pairs(y)
        y_copy(blk).start(priority=1)
        return carry

    lax.fori_loop(0, nb, block, 0)

    @pl.when(e == pl.num_programs(0) - 1)
    def _():
        for back in range(1, ring + 1):
            @pl.when(na - back >= 0)
            def _():
                y_copy(na - back).wait()

        ybuf[0] = jnp.zeros(ybuf.shape[1:], U32)

        def zero_block(b, carry):
            rows = pl.ds(pl.multiple_of(b * MOE_BLOCK, MOE_BLOCK), MOE_BLOCK)
            cp = pltpu.make_async_copy(ybuf.at[0], yb_ref.at[rows], ysem.at[0])
            cp.start()
            cp.wait()
            return carry

        lax.fori_loop(na_ref[0], n_blocks, zero_block, 0)


def _ffn(block_start, block_count, n_active, xb, w_gate, w_up, w_down):
    n_rows, d_packed = xb.shape
    n_exp, d, d_e = w_gate.shape
    n_blocks = n_rows // MOE_BLOCK
    return pl.pallas_call(
        partial(_ffn_kernel, n_blocks=n_blocks),
        grid_spec=pltpu.PrefetchScalarGridSpec(
            num_scalar_prefetch=3,
            grid=(n_exp,),
            in_specs=[
                pl.BlockSpec(memory_space=pl.ANY),
                pl.BlockSpec((1, d, d_e), lambda e, *_: (e, 0, 0)),
                pl.BlockSpec((1, d, d_e), lambda e, *_: (e, 0, 0)),
                pl.BlockSpec((1, d_e, d), lambda e, *_: (e, 0, 0)),
            ],
            out_specs=pl.BlockSpec(memory_space=pl.ANY),
            scratch_shapes=[
                pltpu.VMEM((FFN_RING, MOE_BLOCK, d_packed), U32),
                pltpu.VMEM((FFN_RING, MOE_BLOCK, d_packed), U32),
                pltpu.VMEM((d, d_e), BF16),
                pltpu.VMEM((d, d_e), BF16),
                pltpu.VMEM((d_e, d), BF16),
                pltpu.SemaphoreType.DMA((FFN_RING,)),
                pltpu.SemaphoreType.DMA((FFN_RING,)),
            ],
        ),
        out_shape=jax.ShapeDtypeStruct((n_rows, d_packed), U32),
        compiler_params=_cparams(("arbitrary",)),
        name="moe_ffn",
    )(block_start, block_count, n_active, xb, w_gate, w_up, w_down)


def _ple_kernel(d0_ref, d1_ref, x1_ref, wcol_ref, p_ref, yb_ref, gple_ref, wpg_ref, wp_ref, gfin_ref,
                y_ref, g_scr, wpg_bf, sem, *, time_major, n_steps):
    i = pl.program_id(0)
    tm = x1_ref.shape[0]
    slot = lax.rem(i, 2) if n_steps > 1 else 0

    def copy(tile, r, k, to_slot):
        d_ref = d0_ref if k == 0 else d1_ref
        return pltpu.make_async_copy(yb_ref.at[pl.ds(d_ref[tile * tm + r], 1)],
                                     g_scr.at[to_slot, k, pl.ds(r, 1)], sem.at[to_slot])

    def issue(tile, to_slot):
        def body(r, carry):
            copy(tile, r, 0, to_slot).start(priority=0)
            copy(tile, r, 1, to_slot).start(priority=1)
            return carry
        lax.fori_loop(0, tm, body, 0, unroll=ROW_DMA_UNROLL)

    def drain(tile, from_slot):
        def body(r, carry):
            copy(tile, r, 0, from_slot).wait()
            copy(tile, r, 1, from_slot).wait()
            return carry
        lax.fori_loop(0, tm, body, 0, unroll=ROW_DMA_UNROLL)

    @pl.when(i == 0)
    def _():
        issue(0, 0)
        wpg_bf[...] = wpg_ref[...].astype(BF16)

    drain(i, slot)

    y0_lo, y0_hi = _unpack_pairs_f32(g_scr[slot, 0])
    y1_lo, y1_hi = _unpack_pairs_f32(g_scr[slot, 1])
    w0 = wcol_ref[:, 0:1]
    w1 = wcol_ref[:, 1:2]
    moe = jnp.concatenate([w0 * y0_lo + w1 * y1_lo, w0 * y0_hi + w1 * y1_hi], axis=1)
    x2 = x1_ref[...] + moe

    if n_steps > 1:
        nxt = jnp.minimum(i + 1, n_steps - 1)
        for r in range(tm):
            copy(nxt, r, 0, 1 - slot).start(priority=0)
            copy(nxt, r, 1, 1 - slot).start(priority=1)
    hn = _rms(x2, gple_ref[...]).astype(BF16)
    gate = jax.nn.sigmoid(_dot(hn, wpg_bf[...]))
    if time_major:
        pd = wp_ref.shape[0]
        pp = jnp.concatenate([p_ref[:, t * pd:(t + 1) * pd] for t in range(p_ref.shape[1] // pd)], axis=0)
    else:
        pp = p_ref[...]
    x3 = x2 + _dot(pp.astype(BF16), wp_ref[...]) * gate
    y = _rms(x3, gfin_ref[...])
    if time_major:
        d = y.shape[1]
        rows = y_ref.shape[0]
        for t in range(y_ref.shape[1] // d):
            y_ref[:, t * d:(t + 1) * d] = y[t * rows:(t + 1) * rows, :]
    else:
        y_ref[...] = y

    if n_steps > 1:
        @pl.when(i == n_steps - 1)
        def _():
            drain(n_steps - 1, 1 - slot)


def _ple(d0, d1, x1, wcol, p2d, yb, g_ple, w_pg, w_p, g_final, *, time_major):
    n_tok, d = x1.shape
    tm = n_tok if time_major else PLE_TILE
    n_steps = n_tok // tm
    pd = w_p.shape[0]
    if time_major:
        n_seq = p2d.shape[0]
        p_spec = pl.BlockSpec(p2d.shape, lambda i, *_: (0, 0))
        y_spec = pl.BlockSpec((n_seq, (n_tok // n_seq) * d), lambda i, *_: (0, 0))
        y_shape = jax.ShapeDtypeStruct((n_seq, (n_tok // n_seq) * d), F32)
    else:
        p_spec = pl.BlockSpec((tm, pd), lambda i, *_: (i, 0))
        y_spec = pl.BlockSpec((tm, d), lambda i, *_: (i, 0))
        y_shape = jax.ShapeDtypeStruct((n_tok, d), F32)
    return pl.pallas_call(
        partial(_ple_kernel, time_major=time_major, n_steps=n_steps),
        grid_spec=pltpu.PrefetchScalarGridSpec(
            num_scalar_prefetch=2,
            grid=(n_steps,),
            in_specs=[
                pl.BlockSpec((tm, d), lambda i, *_: (i, 0)),
                pl.BlockSpec((tm, LANES), lambda i, *_: (i, 0)),
                p_spec,
                pl.BlockSpec(memory_space=pl.ANY),
                pl.BlockSpec((1, d), lambda i, *_: (0, 0)),
                pl.BlockSpec(w_pg.shape, lambda i, *_: (0, 0), pipeline_mode=pl.Buffered(1)),
                pl.BlockSpec(w_p.shape, lambda i, *_: (0, 0)),
                pl.BlockSpec((1, d), lambda i, *_: (0, 0)),
            ],
            out_specs=y_spec,
            scratch_shapes=[
                pltpu.VMEM((min(n_steps, 2), 2, tm, yb.shape[1]), yb.dtype),
                pltpu.VMEM(w_pg.shape, BF16),
                pltpu.SemaphoreType.DMA((min(n_steps, 2),)),
            ],
        ),
        out_shape=y_shape,
        compiler_params=_cparams(("arbitrary",)),
        name="ple_tm" if time_major else "ple",
    )(d0, d1, x1, wcol, p2d, yb, g_ple, w_pg, w_p, g_final)


def _pad_lanes(v, offset=0):
    row = jnp.zeros((1, LANES), F32)
    return row.at[0, offset:offset + v.shape[0]].set(v.astype(F32))


def kernel(x_prompt, x_sample, state_conv, state_delta, p_prompt, p_sample, g_mix, w_in, w_s, b_s, g_v,
           conv_w, a_log, dt_bias, g_out, w_out, g_ffn, w_group, b_group, w_router, b_router, w_gate, w_up,
           w_down, g_ple, w_ple_gate, w_ple, g_final):
    n_batch, seq, d = x_prompt.shape
    n_seq, n_pos, _ = x_sample.shape
    depth = g_mix.shape[0]
    assert depth == 1
    a_width = A_GROUPS * A_GROUP_DIM
    hd = B_HEADS * B_DIM
    n_main = 2 * a_width + 4 * hd
    n_p = n_batch * seq
    n_s = n_seq * n_pos

    xp = x_prompt.reshape(n_p, d)
    xs = x_sample.reshape(n_seq, n_pos * d)
    l = 0

    w_main = jnp.swapaxes(w_in[l], 0, 1).astype(BF16)
    w_tail = jnp.pad(w_main[n_main:], ((0, LANES - 2 * B_HEADS), (0, 0)))
    gm = g_mix[l].reshape(1, d)
    causal = jnp.tril(jnp.ones((A_CHUNK, A_CHUNK), dtype=bool))
    ws_tril = jnp.where(causal, w_s[l], 0).astype(BF16)
    b_t = b_s[l].T
    gv = g_v[l].reshape(1, a_width)
    alog_row = _pad_lanes(a_log[l])
    dtb_row = _pad_lanes(dt_bias[l])
    gout = g_out[l].reshape(1, B_DIM)
    w_o = w_out[l]
    gf = g_ffn[l].reshape(1, d)
    wr = jnp.pad(jnp.concatenate([w_group[l], w_router[l]], axis=1),
                 ((0, 0), (0, LANES - N_GROUPS - N_EXPERTS)))
    wr_hi = wr.astype(BF16)
    wr_t = jnp.concatenate([wr_hi, (wr - wr_hi.astype(F32)).astype(BF16)], axis=1)
    br_col = jnp.broadcast_to(
        jnp.pad(jnp.concatenate([b_group[l], b_router[l]]), (0, ROUTER_ROWS - N_GROUPS - N_EXPERTS))[:, None],
        (ROUTER_ROWS, LANES))
    gp = g_ple[l].reshape(1, d)
    w_pg = w_ple_gate[l]
    w_p = w_ple[l].astype(BF16)
    gfin = g_final.reshape(1, d)

    proj_p, tail_p, oa_p = _inproj_mixa(xp, gm, w_main, w_tail, ws_tril, b_t, gv, n_main)
    proj_s, tail_s = _inproj(xs, gm, w_main, w_tail, n_main, time_major=True)
    oa_s, va_s = _mixa_sample(proj_s, w_s[l], b_s[l], gv, n_seq, n_pos)
    ob_p, sd_p, cb_p = _gdn_prompt(proj_p, tail_p, conv_w[l], alog_row, dtb_row, gout, n_batch, seq)
    cbuf = state_conv[l].reshape(n_seq, (CONV_W - 1) * 3 * hd)
    ob_s, sd_s, cb_s = _gdn_decode(proj_s, tail_s, cbuf, state_delta[l], conv_w[l], alog_row, dtb_row,
                                   gout, n_seq, n_pos)
    ob_s = ob_s.reshape(n_s, hd)

    x1_p, h2_p, lt_p = _outproj(xp, oa_p, ob_p, w_o, gf, wr_t, br_col, time_major=False)
    x1_s, h2_s, lt_s = _outproj(xs, oa_s, ob_s, w_o, gf, wr_t, br_col, time_major=True)
    dest, wcol, meta = _router(lt_p, lt_s)
    d0, d1 = dest[0], dest[1]
    n_active = meta[0, 0:1]
    block_start = meta[1, :N_EXPERTS]
    block_count = meta[2, :N_EXPERTS]

    n_tok = n_p + n_s
    n_blocks = (n_tok * 2) // MOE_BLOCK + N_EXPERTS
    xb = _dispatch(d0, d1, block_start, block_count, n_active, h2_p, h2_s, n_blocks)
    yb = _ffn(block_start, block_count, n_active, xb, w_gate[l], w_up[l], w_down[l])

    pp = p_prompt[l].reshape(n_p, -1)
    ps = p_sample[l].reshape(n_seq, -1)
    y_p = _ple(d0[:n_p], d1[:n_p], x1_p, wcol[:n_p], pp, yb, gp, w_pg, w_p, gfin, time_major=False)
    y_s = _ple(d0[n_p:], d1[n_p:], x1_s, wcol[n_p:], ps, yb, gp, w_pg, w_p, gfin, time_major=True)

    return (
        y_p.reshape(n_batch, seq, d),
        y_s.reshape(n_seq, n_pos, d),
        cb_p[None],
        sd_p[None],
        cb_s.reshape(n_seq, CONV_W - 1, 3 * hd)[None],
        sd_s[None],
        va_s.reshape(n_seq, n_pos, a_width)[None],
    )
```
